```python
import jax
import jax.numpy as jnp
from jax import lax
import numpy as np

D_MODEL = 1024
BATCH = 16
SEQ = 2048
DEPTH = 1

CTX_LEN = 256
GRID_W = 64

RET_HEADS = 8
RET_DK = 64
RET_DV = 64
QK_WIDTH = RET_HEADS * RET_DK
RET_WIDTH = RET_HEADS * RET_DV
RET_CHUNK = 128
ROPE_PAIRS_PER_AXIS = RET_DK // 4
ROPE_BASE = 10000.0

FNET_GROUPS = 4
FNET_GROUP_DIM = 128
FNET_WIDTH = FNET_GROUPS * FNET_GROUP_DIM

MIX_WIDTH = RET_WIDTH + FNET_WIDTH

Q_OFF = 0
K_OFF = Q_OFF + QK_WIDTH
V_OFF = K_OFF + QK_WIDTH
G_OFF = V_OFF + RET_WIDTH
F_OFF = G_OFF + RET_WIDTH
IN_WIDTH = F_OFF + FNET_WIDTH

PEER_HEADS = 8
PEER_D_KEY = 256
PEER_N_KEYS = 128
PEER_N_EXPERTS = PEER_N_KEYS * PEER_N_KEYS
PEER_TOPK = 16
PEER_BLOCK = 128

N_MOD = 6
EPS = 1e-6

kernel_name = 'hybrid_retention_fnet_peer_dit_layer'


def _rms(x, gain):
    xf = x.astype(jnp.float32)
    y = xf * lax.rsqrt(jnp.mean(xf * xf, axis=-1, keepdims=True) + EPS)
    return (y * gain.astype(jnp.float32)).astype(x.dtype)


def _modulate(h, shift, scale):
    return h * (1 + scale) + shift


def _heads(t, d):
    b, n, _ = t.shape
    return t.reshape(b, n, -1, d).astype(jnp.float32)


def _axial_rope(rows):
    row_ids = jnp.repeat(jnp.arange(rows, dtype=jnp.float32), GRID_W)
    col_ids = jnp.tile(jnp.arange(GRID_W, dtype=jnp.float32), rows)
    freqs = ROPE_BASE ** (-jnp.arange(ROPE_PAIRS_PER_AXIS, dtype=jnp.float32) / ROPE_PAIRS_PER_AXIS)
    ang = jnp.concatenate([row_ids[:, None] * freqs[None, :], col_ids[:, None] * freqs[None, :]], axis=-1)
    return jnp.cos(ang), jnp.sin(ang)


def _rotate(t, cos, sin):
    half = t.shape[-1] // 2
    t1, t2 = t[..., :half], t[..., half:]
    cs = cos[None, :, None, :]
    sn = sin[None, :, None, :]
    return jnp.concatenate([t1 * cs - t2 * sn, t2 * cs + t1 * sn], axis=-1)


def _log_gamma(a):
    return jnp.log1p(-jnp.exp(a.astype(jnp.float32)))


def _retention_chunked(q, k, v, log_gamma, state0, strict):
    b, n, h, _ = q.shape
    c = RET_CHUNK
    nc = n // c

    def to_chunks(t):
        return t.reshape(b, nc, c, h, t.shape[-1]).transpose(1, 0, 3, 2, 4)

    qc, kc, vc = to_chunks(q), to_chunks(k), to_chunks(v)
    pos = jnp.arange(c, dtype=jnp.float32)
    diff = pos[:, None] - pos[None, :]
    allowed = diff > 0 if strict else diff >= 0
    decay_mask = jnp.where(allowed[None], jnp.exp(log_gamma[:, None, None] * jnp.maximum(diff, 0.0)[None]), 0.0)
    q_decay = jnp.exp(log_gamma[:, None] * (pos + 1.0)[None, :])
    k_decay = jnp.exp(log_gamma[:, None] * (c - 1.0 - pos)[None, :])
    chunk_decay = jnp.exp(log_gamma * c)

    def step(state, inp):
        qi, ki, vi = inp
        scores = jnp.einsum('bhid,bhjd->bhij', qi, ki) * decay_mask[None]
        inner = jnp.einsum('bhij,bhjv->bhiv', scores, vi)
        cross = jnp.einsum('bhid,bhdv->bhiv', qi, state) * q_decay[None, :, :, None]
        new_state = state * chunk_decay[None, :, None, None] + jnp.einsum('bhjd,bhjv->bhdv', ki * k_decay[None, :, :, None], vi)
        return new_state, inner + cross

    final, out = lax.scan(step, state0, (qc, kc, vc))
    out = out.transpose(1, 0, 3, 2, 4).reshape(b, n, h, v.shape[-1])
    return out, final


def _bidir_retention(q, k, v, log_gf, log_gb, state_f0, state_b0):
    out_f, st_f = _retention_chunked(q, k, v, log_gf, state_f0, False)
    flip = lambda t: jnp.flip(t, axis=1)
    out_b, st_b = _retention_chunked(flip(q), flip(k), flip(v), log_gb, state_b0, True)
    return out_f + flip(out_b), st_f, st_b


def _context_states(k, v, log_gf, log_gb):
    n = k.shape[1]
    pos = jnp.arange(n, dtype=jnp.float32)
    wf = jnp.exp(log_gf[:, None] * (n - 1.0 - pos)[None, :])
    wb = jnp.exp(log_gb[:, None] * pos[None, :])
    st_f = jnp.einsum('bnhd,hn,bnhv->bhdv', k, wf, v)
    st_b = jnp.einsum('bnhd,hn,bnhv->bhdv', k, wb, v)
    return st_f, st_b


def _fourier_mix(f):
    b, n, _ = f.shape
    fg = f.astype(jnp.float32).reshape(b, n, FNET_GROUPS, FNET_GROUP_DIM)
    return jnp.fft.fftn(fg, axes=(1, 3), norm='ortho').real.reshape(b, n, FNET_WIDTH)


def _mixer(h, w_in_l, log_gf, log_gb, ret_gain, state_f0, state_b0, rope):
    b, n, _ = h.shape
    proj = h @ w_in_l
    q = _heads(proj[..., Q_OFF:K_OFF], RET_DK)
    k = _heads(proj[..., K_OFF:V_OFF], RET_DK) * (RET_DK ** -0.5)
    v = _heads(proj[..., V_OFF:G_OFF], RET_DV)
    gate = proj[..., G_OFF:F_OFF].astype(jnp.float32)
    if rope is not None:
        cos, sin = rope
        q = _rotate(q, cos, sin)
        k = _rotate(k, cos, sin)
    ret, st_f, st_b = _bidir_retention(q, k, v, log_gf, log_gb, state_f0, state_b0)
    ret = ret * lax.rsqrt(jnp.mean(ret * ret, axis=-1, keepdims=True) + EPS)
    ret = ret.reshape(b, n, RET_WIDTH) * ret_gain.astype(jnp.float32) * jax.nn.silu(gate)
    four = _fourier_mix(proj[..., F_OFF:])
    mix = jnp.concatenate([ret, four], axis=-1).astype(h.dtype)
    return mix, st_f, st_b


def _peer(h, w_q, keys1, keys2, u, v):
    b, n, d = h.shape
    q = (h @ w_q).reshape(b, n, PEER_HEADS, PEER_D_KEY).astype(jnp.float32)
    half = PEER_D_KEY // 2
    s1 = jnp.einsum('bnhd,kd->bnhk', q[..., :half], keys1.astype(jnp.float32))
    s2 = jnp.einsum('bnhd,kd->bnhk', q[..., half:], keys2.astype(jnp.float32))
    v1, i1 = lax.top_k(s1, PEER_TOPK)
    v2, i2 = lax.top_k(s2, PEER_TOPK)
    cand = (v1[..., :, None] + v2[..., None, :]).reshape(b, n, PEER_HEADS, PEER_TOPK * PEER_TOPK)
    top_v, top_i = lax.top_k(cand, PEER_TOPK)
    e1 = jnp.take_along_axis(i1, top_i // PEER_TOPK, axis=-1)
    e2 = jnp.take_along_axis(i2, top_i % PEER_TOPK, axis=-1)
    ids = e1 * PEER_N_KEYS + e2
    wts = jax.nn.softmax(top_v, axis=-1)
    t = b * n
    nb = t // PEER_BLOCK
    k_all = PEER_HEADS * PEER_TOPK
    hb = h.reshape(nb, PEER_BLOCK, d)
    idb = ids.reshape(nb, PEER_BLOCK, k_all)
    wb = wts.reshape(nb, PEER_BLOCK, k_all)

    def block(args):
        hx, idx, w = args
        ue = jnp.take(u, idx, axis=0)
        act = jnp.einsum('td,tkd->tk', hx, ue).astype(jnp.float32)
        coef = (w * jax.nn.gelu(act, approximate=False)).astype(h.dtype)
        ve = jnp.take(v, idx, axis=0)
        return jnp.einsum('tk,tkd->td', coef, ve)

    out = lax.map(block, (hb, idb, wb))
    return out.reshape(b, n, d).astype(h.dtype)


def setup_inputs(seed: int = 0) -> dict:
    key = jax.random.key(seed)
    ks = jax.random.split(key, 24)
    f32 = jnp.float32

    def nrm(k, shape, scale):
        return jax.random.normal(k, shape, f32) * scale

    def gain(k, shape):
        return 1.0 + 0.05 * jax.random.normal(k, shape, f32)

    base_decay = -(5.0 + jnp.arange(RET_HEADS, dtype=f32)) * np.float32(np.log(2.0))
    return {
        'x': nrm(ks[0], (BATCH, SEQ, D_MODEL), 1.0),
        'c': nrm(ks[1], (BATCH, D_MODEL), 1.0),
        'ctx': nrm(ks[2], (BATCH, CTX_LEN, D_MODEL), 1.0),
        'c_ctx': nrm(ks[3], (D_MODEL,), 1.0),
        'w_ada': nrm(ks[4], (DEPTH, D_MODEL, N_MOD * D_MODEL), 0.5 * D_MODEL ** -0.5),
        'b_ada': nrm(ks[5], (DEPTH, N_MOD * D_MODEL), 0.01),
        'norm_pre_mix': gain(ks[6], (DEPTH, D_MODEL)),
        'norm_post_mix': gain(ks[7], (DEPTH, D_MODEL)),
        'w_in': nrm(ks[8], (DEPTH, D_MODEL, IN_WIDTH), D_MODEL ** -0.5),
        'ret_decay_fwd': base_decay[None, :] + nrm(ks[9], (DEPTH, RET_HEADS), 0.05),
        'ret_decay_bwd': base_decay[None, :] + nrm(ks[10], (DEPTH, RET_HEADS), 0.05),
        'ret_norm_gain': gain(ks[11], (DEPTH, RET_WIDTH)),
        'w_out': nrm(ks[12], (DEPTH, MIX_WIDTH, D_MODEL), MIX_WIDTH ** -0.5),
        'norm_pre_ffn': gain(ks[13], (DEPTH, D_MODEL)),
        'norm_post_ffn': gain(ks[14], (DEPTH, D_MODEL)),
        'peer_w_query': nrm(ks[15], (DEPTH, D_MODEL, PEER_HEADS * PEER_D_KEY), D_MODEL ** -0.5),
        'peer_sub_keys_1': nrm(ks[16], (DEPTH, PEER_N_KEYS, PEER_D_KEY // 2), (PEER_D_KEY // 2) ** -0.5),
        'peer_sub_keys_2': nrm(ks[17], (DEPTH, PEER_N_KEYS, PEER_D_KEY // 2), (PEER_D_KEY // 2) ** -0.5),
        'peer_u': nrm(ks[18], (DEPTH, PEER_N_EXPERTS, D_MODEL), D_MODEL ** -0.5),
        'peer_v': nrm(ks[19], (DEPTH, PEER_N_EXPERTS, D_MODEL), D_MODEL ** -0.5),
    }


def reference(x, c, ctx, c_ctx, w_ada, b_ada, norm_pre_mix, norm_post_mix, w_in, ret_decay_fwd, ret_decay_bwd, ret_norm_gain, w_out, norm_pre_ffn, norm_post_ffn, peer_w_query, peer_sub_keys_1, peer_sub_keys_2, peer_u, peer_v):
    n = x.shape[1]
    rows = n // GRID_W
    rope = _axial_rope(rows)
    for i in range(DEPTH):
        mod_x = (jax.nn.silu(c) @ w_ada[i] + b_ada[i])[:, None, :]
        mod_c = (jax.nn.silu(c_ctx) @ w_ada[i] + b_ada[i])[None, None, :]
        sh1, sc1, g1, sh2, sc2, g2 = jnp.split(mod_x, N_MOD, axis=-1)
        csh1, csc1, cg1, csh2, csc2, cg2 = jnp.split(mod_c, N_MOD, axis=-1)
        log_gf = _log_gamma(ret_decay_fwd[i])
        log_gb = _log_gamma(ret_decay_bwd[i])
        update_ctx = i + 1 < DEPTH

        hc = _modulate(_rms(ctx, norm_pre_mix[i]), csh1, csc1)
        if update_ctx:
            zeros = jnp.zeros((ctx.shape[0], RET_HEADS, RET_DK, RET_DV), jnp.float32)
            mix_c, st_f, st_b = _mixer(hc, w_in[i], log_gf, log_gb, ret_norm_gain[i], zeros, zeros, None)
        else:
            kv = hc @ w_in[i][:, K_OFF:G_OFF]
            k_c = _heads(kv[..., :QK_WIDTH], RET_DK) * (RET_DK ** -0.5)
            v_c = _heads(kv[..., QK_WIDTH:], RET_DV)
            st_f, st_b = _context_states(k_c, v_c, log_gf, log_gb)

        hx = _modulate(_rms(x, norm_pre_mix[i]), sh1, sc1)
        mix_x, _, _ = _mixer(hx, w_in[i], log_gf, log_gb, ret_norm_gain[i], st_f, st_b, rope)
        x = x + g1 * _rms(mix_x @ w_out[i], norm_post_mix[i])

        h2 = _modulate(_rms(x, norm_pre_ffn[i]), sh2, sc2)
        x = x + g2 * _rms(_peer(h2, peer_w_query[i], peer_sub_keys_1[i], peer_sub_keys_2[i], peer_u[i], peer_v[i]), norm_post_ffn[i])

        if update_ctx:
            ctx = ctx + cg1 * _rms(mix_c @ w_out[i], norm_post_mix[i])
            hc2 = _modulate(_rms(ctx, norm_pre_ffn[i]), csh2, csc2)
            ctx = ctx + cg2 * _rms(_peer(hc2, peer_w_query[i], peer_sub_keys_1[i], peer_sub_keys_2[i], peer_u[i], peer_v[i]), norm_post_ffn[i])
    return x
```

```python
import functools

import numpy as np
import jax
import jax.numpy as jnp
from jax import lax
from jax.experimental import pallas as pl
from jax.experimental.pallas import tpu as pltpu

F32 = jnp.float32
BF16 = jnp.bfloat16
I32 = jnp.int32
U32 = jnp.uint32

D_MODEL = 1024
GRID_W = 64
RET_HEADS = 8
RET_DK = 64
RET_WIDTH = 512
QK_WIDTH = 512
RET_CHUNK = 128
ROPE_PAIRS = RET_DK // 4
ROPE_BASE = 10000.0
FNET_GROUPS = 4
FNET_GROUP_DIM = 128
FNET_WIDTH = 512
IN_WIDTH = 2560
PEER_HEADS = 8
PEER_D_KEY = 256
PEER_N_KEYS = 128
PEER_TOPK = 16
N_MOD = 6
EPS = 1e-6

LANES = 128
SUBLANES = 8
VMEM_LIMIT = 56 * 1024 * 1024
N_PAIRS = RET_HEADS // 2


def _cparams(sem):
    return pltpu.CompilerParams(dimension_semantics=sem, vmem_limit_bytes=VMEM_LIMIT)


def _rms_rows(x, gain):
    ms = jnp.mean(x * x, axis=-1, keepdims=True)
    return x * lax.rsqrt(ms + EPS) * gain


def _silu(x):
    return x * jax.nn.sigmoid(x)


def _log_gamma(a):
    return jnp.log1p(-jnp.exp(a))


def _ada_kernel(c_ref, w_ref, b_ref, o_ref):
    s = _silu(c_ref[...])
    o_ref[...] = jnp.dot(s, w_ref[...], preferred_element_type=F32) + b_ref[...]


def _ada(c_all, w_ada, b_ada):
    rows = c_all.shape[0]
    width = w_ada.shape[1]
    tn = 1536
    return pl.pallas_call(
        _ada_kernel,
        out_shape=jax.ShapeDtypeStruct((rows, width), F32),
        grid=(width // tn,),
        in_specs=[
            pl.BlockSpec((rows, D_MODEL), lambda j: (0, 0)),
            pl.BlockSpec((D_MODEL, tn), lambda j: (0, j)),
            pl.BlockSpec((1, tn), lambda j: (0, j)),
        ],
        out_specs=pl.BlockSpec((rows, tn), lambda j: (0, j)),
        compiler_params=_cparams(("arbitrary",)),
        name="ada",
    )(c_all, w_ada, b_ada)


def _ctx_kernel(ctx_ref, g_ref, sh_ref, sc_ref, w_ref, af_ref, ab_ref, stf_ref, stb_ref):
    x = ctx_ref[0]
    n = x.shape[0]
    h = (_rms_rows(x, g_ref[...]) * (1.0 + sc_ref[...]) + sh_ref[...]).astype(BF16)
    kv = jnp.dot(h, w_ref[...], preferred_element_type=F32)
    k = kv[:, :QK_WIDTH] * (RET_DK ** -0.5)
    v = kv[:, QK_WIDTH:].astype(BF16)
    pos = lax.broadcasted_iota(I32, (n, QK_WIDTH), 0).astype(F32)
    lgf = _log_gamma(af_ref[...])
    lgb = _log_gamma(ab_ref[...])
    kf = (k * jnp.exp(lgf * (n - 1.0 - pos))).astype(BF16)
    kb = (k * jnp.exp(lgb * pos)).astype(BF16)
    tn = (((0,), (0,)), ((), ()))
    gf = lax.dot_general(kf, v, tn, preferred_element_type=F32)
    gb = lax.dot_general(kb, v, tn, preferred_element_type=F32)
    ri = lax.broadcasted_iota(I32, (LANES, LANES), 0) // RET_DK
    ci = lax.broadcasted_iota(I32, (LANES, LANES), 1) // RET_DK
    bd = ri == ci
    for p in range(N_PAIRS):
        sl = slice(p * LANES, (p + 1) * LANES)
        stf_ref[0, p] = jnp.where(bd, gf[sl, sl], 0.0)
        stb_ref[0, p] = jnp.where(bd, gb[sl, sl], 0.0)


def _ctx_states(ctx, gain, csh, csc, w_kv, a_f, a_b):
    b, n, _ = ctx.shape
    vec = lambda: pl.BlockSpec((1, D_MODEL), lambda i: (0, 0))
    lane = lambda: pl.BlockSpec((1, QK_WIDTH), lambda i: (0, 0))
    st = jax.ShapeDtypeStruct((b, N_PAIRS, LANES, LANES), F32)
    st_spec = lambda: pl.BlockSpec((1, N_PAIRS, LANES, LANES), lambda i: (i, 0, 0, 0))
    return pl.pallas_call(
        _ctx_kernel,
        out_shape=(st, st),
        grid=(b,),
        in_specs=[
            pl.BlockSpec((1, n, D_MODEL), lambda i: (i, 0, 0)),
            vec(), vec(), vec(),
            pl.BlockSpec((D_MODEL, 2 * QK_WIDTH), lambda i: (0, 0)),
            lane(), lane(),
        ],
        out_specs=(st_spec(), st_spec()),
        compiler_params=_cparams(("arbitrary",)),
        name="ctx_states",
    )(ctx, gain, csh, csc, w_kv, a_f, a_b)


def _inproj_kernel(x_ref, sh_ref, sc_ref, g_ref, w_ref, cos_ref, sin_ref,
                   q_ref, k_ref, v_ref, gate_ref, f_ref):
    x = x_ref[0]
    tm = x.shape[0]
    h = (_rms_rows(x, g_ref[...]) * (1.0 + sc_ref[0]) + sh_ref[0]).astype(BF16)

    def proj(lo, hi):
        return jnp.dot(h, w_ref[:, lo:hi], preferred_element_type=F32)

    cos = cos_ref[...]
    sin = sin_ref[...]
    lane = lax.broadcasted_iota(I32, (tm, QK_WIDTH), 1)
    first = (lane % RET_DK) < (RET_DK // 2)

    def rope(t):
        swapped = jnp.where(first,
                            pltpu.roll(t, QK_WIDTH - RET_DK // 2, 1),
                            pltpu.roll(t, RET_DK // 2, 1))
        return t * cos + swapped * sin

    q_ref[0] = rope(proj(0, 512)).astype(BF16)
    k_ref[0] = rope(proj(512, 1024) * (RET_DK ** -0.5)).astype(BF16)
    v_ref[0] = proj(1024, 1536).astype(BF16)
    gate_ref[0] = proj(1536, 2048)
    f_ref[0] = proj(2048, 2560).astype(BF16)


def _inproj(x, sh, sc, gain, w_in, cos, sin, tm):
    b, n, _ = x.shape
    mod = lambda: pl.BlockSpec((1, 1, D_MODEL), lambda i, j: (i, 0, 0))
    tab = lambda: pl.BlockSpec((tm, QK_WIDTH), lambda i, j: (j, 0))
    out = lambda: pl.BlockSpec((1, tm, 512), lambda i, j: (i, j, 0))
    sd = lambda dt: jax.ShapeDtypeStruct((b, n, 512), dt)
    return pl.pallas_call(
        _inproj_kernel,
        out_shape=(sd(BF16), sd(BF16), sd(BF16), sd(F32), sd(BF16)),
        grid=(b, n // tm),
        in_specs=[
            pl.BlockSpec((1, tm, D_MODEL), lambda i, j: (i, j, 0)),
            mod(), mod(),
            pl.BlockSpec((1, D_MODEL), lambda i, j: (0, 0)),
            pl.BlockSpec((D_MODEL, IN_WIDTH), lambda i, j: (0, 0)),
            tab(), tab(),
        ],
        out_specs=(out(), out(), out(), out(), out()),
        compiler_params=_cparams(("arbitrary", "arbitrary")),
        name="inproj",
    )(x, sh, sc, gain, w_in, cos, sin)


def _ret_kernel(af_s, ab_s, q_ref, k_ref, v_ref, gate_ref, stf_ref, stb_ref, af_ref, ab_ref,
                gain_ref, o_ref, mask_scr, qdf_scr, qdb_scr, kdf_scr, kdb_scr, sf_scr, sb_scr,
                sb_store):
    c = RET_CHUNK
    n = q_ref.shape[1]
    nc = n // c
    lgf = _log_gamma(af_ref[...])
    lgb = _log_gamma(ab_ref[...])
    cdf = jnp.exp(lgf * float(c))
    cdb = jnp.exp(lgb * float(c))

    @pl.when(pl.program_id(0) == 0)
    def _():
        r = lax.broadcasted_iota(I32, (c, RET_WIDTH), 0).astype(F32)
        qdf_scr[...] = jnp.exp(lgf * (r + 1.0))
        kdf_scr[...] = jnp.exp(lgf * (c - 1.0 - r))
        qdb_scr[...] = jnp.exp(lgb * (c - r))
        kdb_scr[...] = jnp.exp(lgb * r)
        ri = lax.broadcasted_iota(I32, (c, c), 0)
        ci = lax.broadcasted_iota(I32, (c, c), 1)
        d = (ri - ci).astype(F32)
        for hh in range(RET_HEADS):
            gf = _log_gamma(jnp.full((c, c), af_s[hh], F32))
            gb = _log_gamma(jnp.full((c, c), ab_s[hh], F32))
            mask_scr[hh] = jnp.where(d >= 0.0, jnp.exp(gf * jnp.maximum(d, 0.0)),
                                     jnp.exp(gb * jnp.maximum(-d, 0.0)))

    ri = lax.broadcasted_iota(I32, (LANES, LANES), 0) // RET_DK
    ci = lax.broadcasted_iota(I32, (LANES, LANES), 1) // RET_DK
    bd = ri == ci
    bd_ones = jnp.where(bd, 1.0, 0.0).astype(BF16)
    lane_head = lax.broadcasted_iota(I32, (c, LANES), 1) // RET_DK
    tn = (((0,), (0,)), ((), ()))
    nt = (((1,), (1,)), ((), ()))

    for p in range(N_PAIRS):
        sf_scr[p] = stf_ref[0, p]
        sb_scr[p] = stb_ref[0, p]

    def bwd_states(i, carry):
        cc = nc - 1 - i
        rows = pl.ds(pl.multiple_of(cc * c, c), c)
        for p in range(N_PAIRS):
            sl = slice(p * LANES, (p + 1) * LANES)
            sb = sb_scr[p]
            sb_store[cc, p] = sb.astype(BF16)
            kw = (k_ref[0, rows, sl].astype(F32) * kdb_scr[:, sl]).astype(BF16)
            g = lax.dot_general(kw, v_ref[0, rows, sl], tn, preferred_element_type=F32)
            sb_scr[p] = sb * cdb[:, sl] + jnp.where(bd, g, 0.0)
        return carry

    lax.fori_loop(0, nc, bwd_states, 0)

    def fwd(cc, carry):
        rows = pl.ds(pl.multiple_of(cc * c, c), c)
        for p in range(N_PAIRS):
            sl = slice(p * LANES, (p + 1) * LANES)
            qp = q_ref[0, rows, sl]
            kp = k_ref[0, rows, sl]
            vp = v_ref[0, rows, sl]
            qf = qp.astype(F32)
            sf = sf_scr[p]
            acc = jnp.dot((qf * qdf_scr[:, sl]).astype(BF16), sf.astype(BF16),
                          preferred_element_type=F32)
            acc += jnp.dot((qf * qdb_scr[:, sl]).astype(BF16), sb_store[cc, p],
                           preferred_element_type=F32)
            for j in range(2):
                own = lane_head == j
                qm = jnp.where(own, qp, jnp.zeros_like(qp))
                s = lax.dot_general(qm, kp, nt, preferred_element_type=F32)
                pm = (s * mask_scr[2 * p + j]).astype(BF16)
                vm = jnp.where(own, vp, jnp.zeros_like(vp))
                acc += jnp.dot(pm, vm, preferred_element_type=F32)
            sq = acc * acc
            sq_hi = sq.astype(BF16)
            sq_lo = (sq - sq_hi.astype(F32)).astype(BF16)
            ms = (jnp.dot(sq_hi, bd_ones, preferred_element_type=F32)
                  + jnp.dot(sq_lo, bd_ones, preferred_element_type=F32)) * (1.0 / RET_DK)
            y = acc * lax.rsqrt(ms + EPS) * gain_ref[:, sl] * _silu(gate_ref[0, rows, sl])
            o_ref[0, rows, sl] = y.astype(BF16)
            kw = (kp.astype(F32) * kdf_scr[:, sl]).astype(BF16)
            g = lax.dot_general(kw, vp, tn, preferred_element_type=F32)
            sf_scr[p] = sf * cdf[:, sl] + jnp.where(bd, g, 0.0)
        return carry

    lax.fori_loop(0, nc, fwd, 0)


def _retention(q, k, v, gate, stf, stb, a_f, a_b, a_f_lane, a_b_lane, gain):
    b, n, _ = q.shape
    nc = n // RET_CHUNK
    seq = lambda: pl.BlockSpec((1, n, RET_WIDTH), lambda i, *_: (i, 0, 0))
    st = lambda: pl.BlockSpec((1, N_PAIRS, LANES, LANES), lambda i, *_: (i, 0, 0, 0))
    lane = lambda: pl.BlockSpec((1, RET_WIDTH), lambda i, *_: (0, 0))
    grid_spec = pltpu.PrefetchScalarGridSpec(
        num_scalar_prefetch=2,
        grid=(b,),
        in_specs=[seq(), seq(), seq(), seq(), st(), st(), lane(), lane(), lane()],
        out_specs=seq(),
        scratch_shapes=[
            pltpu.VMEM((RET_HEADS, RET_CHUNK, RET_CHUNK), F32),
            pltpu.VMEM((RET_CHUNK, RET_WIDTH), F32),
            pltpu.VMEM((RET_CHUNK, RET_WIDTH), F32),
            pltpu.VMEM((RET_CHUNK, RET_WIDTH), F32),
            pltpu.VMEM((RET_CHUNK, RET_WIDTH), F32),
            pltpu.VMEM((N_PAIRS, LANES, LANES), F32),
            pltpu.VMEM((N_PAIRS, LANES, LANES), F32),
            pltpu.VMEM((nc, N_PAIRS, LANES, LANES), BF16),
        ],
    )
    return pl.pallas_call(
        _ret_kernel,
        out_shape=jax.ShapeDtypeStruct((b, n, RET_WIDTH), BF16),
        grid_spec=grid_spec,
        compiler_params=_cparams(("arbitrary",)),
        name="retention",
    )(a_f, a_b, q, k, v, gate, stf, stb, a_f_lane, a_b_lane, gain)


def _four_kernel(f_ref, bc_ref, bs_ref, cs_ref, o_ref, z_scr):
    n = f_ref.shape[1]

    @pl.when(pl.program_id(1) == 0)
    def _():
        f = f_ref[0]
        z_scr[0:n, :] = jnp.dot(f, bc_ref[...], preferred_element_type=F32).astype(BF16)
        z_scr[n:2 * n, :] = jnp.dot(f, bs_ref[...], preferred_element_type=F32).astype(BF16)

    o_ref[0] = jnp.dot(cs_ref[...], z_scr[...], preferred_element_type=F32).astype(BF16)


def _fourier(f, bc, bs, cs, tn):
    b, n, _ = f.shape
    return pl.pallas_call(
        _four_kernel,
        out_shape=jax.ShapeDtypeStruct((b, n, FNET_WIDTH), BF16),
        grid=(b, n // tn),
        in_specs=[
            pl.BlockSpec((1, n, FNET_WIDTH), lambda i, j: (i, 0, 0)),
            pl.BlockSpec((FNET_WIDTH, FNET_WIDTH), lambda i, j: (0, 0)),
            pl.BlockSpec((FNET_WIDTH, FNET_WIDTH), lambda i, j: (0, 0)),
            pl.BlockSpec((tn, 2 * n), lambda i, j: (j, 0)),
        ],
        out_specs=pl.BlockSpec((1, tn, FNET_WIDTH), lambda i, j: (i, j, 0)),
        scratch_shapes=[pltpu.VMEM((2 * n, FNET_WIDTH), BF16)],
        compiler_params=_cparams(("arbitrary", "arbitrary")),
        name="fourier",
    )(f, bc, bs, cs)


def _outp_kernel(ret_ref, four_ref, x_ref, g1_ref, sh2_ref, sc2_ref, gpost_ref, gpre_ref,
                 wo_ref, wq_ref, x1_ref, h2_ref, qp_ref):
    mix = jnp.dot(ret_ref[0], wo_ref[0:RET_WIDTH, :], preferred_element_type=F32)
    mix += jnp.dot(four_ref[0], wo_ref[RET_WIDTH:, :], preferred_element_type=F32)
    x1 = x_ref[0] + g1_ref[0] * _rms_rows(mix, gpost_ref[...])
    x1_ref[0] = x1
    h2 = _rms_rows(x1, gpre_ref[...]) * (1.0 + sc2_ref[0]) + sh2_ref[0]
    h2_ref[0] = h2
    qp_ref[0] = jnp.dot(h2.astype(BF16), wq_ref[...], preferred_element_type=F32).astype(BF16)


def _outproj(ret, four, x, g1, sh2, sc2, gpost, gpre, w_out, w_q, tm):
    b, n, _ = x.shape
    qw = w_q.shape[1]
    mod = lambda: pl.BlockSpec((1, 1, D_MODEL), lambda i, j: (i, 0, 0))
    vec = lambda: pl.BlockSpec((1, D_MODEL), lambda i, j: (0, 0))
    half = lambda: pl.BlockSpec((1, tm, 512), lambda i, j: (i, j, 0))
    full = lambda: pl.BlockSpec((1, tm, D_MODEL), lambda i, j: (i, j, 0))
    return pl.pallas_call(
        _outp_kernel,
        out_shape=(jax.ShapeDtypeStruct((b, n, D_MODEL), F32),
                   jax.ShapeDtypeStruct((b, n, D_MODEL), F32),
                   jax.ShapeDtypeStruct((b, n, qw), BF16)),
        grid=(b, n // tm),
        in_specs=[half(), half(), full(), mod(), mod(), mod(), vec(), vec(),
                  pl.BlockSpec(w_out.shape, lambda i, j: (0, 0)),
                  pl.BlockSpec(w_q.shape, lambda i, j: (0, 0))],
        out_specs=(full(), full(), pl.BlockSpec((1, tm, qw), lambda i, j: (i, j, 0))),
        compiler_params=_cparams(("arbitrary", "arbitrary")),
        name="outproj",
    )(ret, four, x, g1, sh2, sc2, gpost, gpre, w_out, w_q)


def _top_rows(s, kk):
    rows, tq = s.shape
    rid = lax.broadcasted_iota(I32, (rows, tq), 0).astype(F32)
    slot = lax.broadcasted_iota(I32, (kk, tq), 0)
    vals = jnp.zeros((kk, tq), F32)
    idxs = jnp.zeros((kk, tq), F32)
    for it in range(kk):
        m = jnp.max(s, axis=0, keepdims=True)
        am = jnp.min(jnp.where(s == m, rid, float(rows)), axis=0, keepdims=True)
        vals = jnp.where(slot == it, m, vals)
        idxs = jnp.where(slot == it, am, idxs)
        s = jnp.where(rid == am, -jnp.inf, s)
    return vals, idxs


def _pick_rows(table, sel, kk):
    out = jnp.zeros_like(table)
    for i in range(kk):
        out = jnp.where(sel == float(i), table[i:i + 1, :], out)
    return out


def _topk_kernel(qp_ref, k1_ref, k2_ref, slab_ref, shift_ref, wts_ref):
    kk = PEER_TOPK
    half = PEER_D_KEY // 2
    nt = (((1,), (1,)), ((), ()))
    k1 = k1_ref[...]
    k2 = k2_ref[...]
    ids_all, w_all = [], []
    for hh in range(PEER_HEADS):
        q1 = qp_ref[:, hh * PEER_D_KEY: hh * PEER_D_KEY + half]
        q2 = qp_ref[:, hh * PEER_D_KEY + half: (hh + 1) * PEER_D_KEY]
        s1 = lax.dot_general(k1, q1, nt, preferred_element_type=F32)
        s2 = lax.dot_general(k2, q2, nt, preferred_element_type=F32)
        v1, i1 = _top_rows(s1, kk)
        v2, i2 = _top_rows(s2, kk)
        cand = jnp.concatenate([v1[i:i + 1, :] + v2 for i in range(kk)], axis=0)
        tv, ti = _top_rows(cand, kk)
        hi = jnp.floor(ti * (1.0 / kk))
        lo = ti - hi * kk
        ids = _pick_rows(i1, hi, kk) * float(PEER_N_KEYS) + _pick_rows(i2, lo, kk)
        p = jnp.exp(tv - tv[0:1, :])
        w = p / jnp.sum(p, axis=0, keepdims=True)
        ids_all.append(ids)
        w_all.append(w)
    ids_t = jnp.concatenate(ids_all, axis=0).T.astype(I32)
    slab_ref[...] = ids_t >> 1
    shift_ref[...] = (1 - (ids_t & 1)) * 16
    wts_ref[...] = jnp.concatenate(w_all, axis=0).T


def _topk(qp, k1, k2):
    t, qw = qp.shape
    tq = LANES
    k_all = PEER_HEADS * PEER_TOPK
    out = lambda: pl.BlockSpec((tq, k_all), lambda i: (i, 0))
    key = lambda: pl.BlockSpec((PEER_N_KEYS, PEER_D_KEY // 2), lambda i: (0, 0))
    return pl.pallas_call(
        _topk_kernel,
        out_shape=(jax.ShapeDtypeStruct((t, k_all), I32),
                   jax.ShapeDtypeStruct((t, k_all), I32),
                   jax.ShapeDtypeStruct((t, k_all), F32)),
        grid=(t // tq,),
        in_specs=[pl.BlockSpec((tq, qw), lambda i: (i, 0)), key(), key()],
        out_specs=(out(), out(), out()),
        compiler_params=_cparams(("arbitrary",)),
        name="peer_topk",
    )(qp, k1, k2)


PEER_TOKENS_PER_STEP = 16
HI_MASK = np.uint32(0xFFFF0000)


def _expert_row(tbl_ref, slab_s, shift_s, j, k):
    w = tbl_ref[slab_s[j, k]]
    return lax.bitcast_convert_type((w << shift_s[j, k].astype(U32)) & HI_MASK, F32)


def _gelu_exact(x):
    return 0.5 * x * (1.0 + lax.erf(x * (2.0 ** -0.5)))


def _peer_u_kernel(slab_s, shift_s, h_ref, wts_ref, tbl_ref, coef_ref):
    k_all = PEER_HEADS * PEER_TOPK
    nt = (((1,), (1,)), ((), ()))
    ones = jnp.ones((SUBLANES, LANES), BF16)

    def token(j, carry):
        hj = h_ref[j]
        groups = []
        for g in range(k_all // SUBLANES):
            rows = []
            for i in range(SUBLANES):
                val = _expert_row(tbl_ref, slab_s, shift_s, j, g * SUBLANES + i)
                rows.append(jnp.sum(val * hj, axis=0, keepdims=True))
            groups.append(jnp.concatenate(rows, axis=0))
        part = jnp.concatenate(groups, axis=0)
        hi = part.astype(BF16)
        lo = (part - hi.astype(F32)).astype(BF16)
        act = (lax.dot_general(ones, hi, nt, preferred_element_type=F32)
               + lax.dot_general(ones, lo, nt, preferred_element_type=F32))[0:1, :]
        coef_ref[pl.ds(j, 1), :] = wts_ref[pl.ds(j, 1), :] * _gelu_exact(act)
        return carry

    lax.fori_loop(0, PEER_TOKENS_PER_STEP, token, 0)


def _peer_u(slab, shift, h3, wts, tbl):
    t = h3.shape[0]
    tb = PEER_TOKENS_PER_STEP
    k_all = PEER_HEADS * PEER_TOPK
    smem = lambda: pl.BlockSpec((tb, k_all), lambda i: (i, 0), memory_space=pltpu.SMEM)
    return pl.pallas_call(
        _peer_u_kernel,
        out_shape=jax.ShapeDtypeStruct((t, k_all), F32),
        grid=(t // tb,),
        in_specs=[smem(), smem(),
                  pl.BlockSpec((tb, SUBLANES, LANES), lambda i: (i, 0, 0)),
                  pl.BlockSpec((tb, k_all), lambda i: (i, 0)),
                  pl.BlockSpec(memory_space=pltpu.VMEM)],
        out_specs=pl.BlockSpec((tb, k_all), lambda i: (i, 0)),
        compiler_params=_cparams(("arbitrary",)),
        name="peer_u",
    )(slab, shift, h3, wts, tbl)


def _peer_v_kernel(slab_s, shift_s, coef_s, x1_ref, g2_ref, gain_ref, tbl_ref, o_ref):
    k_all = PEER_HEADS * PEER_TOPK
    n_acc = 4

    def token(j, carry):
        accs = [jnp.zeros((SUBLANES, LANES), F32) for _ in range(n_acc)]
        for k in range(k_all):
            val = _expert_row(tbl_ref, slab_s, shift_s, j, k)
            accs[k % n_acc] = accs[k % n_acc] + coef_s[j, k] * val
        out = (accs[0] + accs[1]) + (accs[2] + accs[3])
        ms = jnp.sum(out * out, axis=(0, 1), keepdims=True) * (1.0 / D_MODEL)
        y = out * lax.rsqrt(ms + EPS) * gain_ref[...]
        o_ref[j] = x1_ref[j] + g2_ref[0] * y
        return carry

    lax.fori_loop(0, PEER_TOKENS_PER_STEP, token, 0)


def _peer_v(slab, shift, coef, x1_3, g2_3, gain_3, tbl, tokens_per_sample):
    t = x1_3.shape[0]
    tb = PEER_TOKENS_PER_STEP
    k_all = PEER_HEADS * PEER_TOPK
    steps_per_sample = tokens_per_sample // tb
    smem = lambda: pl.BlockSpec((tb, k_all), lambda i: (i, 0), memory_space=pltpu.SMEM)
    tok = lambda: pl.BlockSpec((tb, SUBLANES, LANES), lambda i: (i, 0, 0))
    return pl.pallas_call(
        _peer_v_kernel,
        out_shape=jax.ShapeDtypeStruct((t, SUBLANES, LANES), F32),
        grid=(t // tb,),
        in_specs=[smem(), smem(), smem(), tok(),
                  pl.BlockSpec((1, SUBLANES, LANES), lambda i: (i // steps_per_sample, 0, 0)),
                  pl.BlockSpec((SUBLANES, LANES), lambda i: (0, 0)),
                  pl.BlockSpec(memory_space=pltpu.VMEM)],
        out_specs=tok(),
        compiler_params=_cparams(("arbitrary",)),
        name="peer_v",
    )(slab, shift, coef, x1_3, g2_3, gain_3, tbl)


def _pack_table(w):
    e = w.shape[0]
    bits = lax.bitcast_convert_type(w.astype(BF16), jnp.uint16).astype(U32)
    bits = bits.reshape(e // 2, 2, SUBLANES, LANES)
    return bits[:, 0] | (bits[:, 1] << 16)


def _rope_tables(n):
    rows = n // GRID_W
    row_ids = jnp.repeat(jnp.arange(rows, dtype=F32), GRID_W)
    col_ids = jnp.tile(jnp.arange(GRID_W, dtype=F32), rows)
    freqs = ROPE_BASE ** (-jnp.arange(ROPE_PAIRS, dtype=F32) / ROPE_PAIRS)
    ang = jnp.concatenate([row_ids[:, None] * freqs[None, :], col_ids[:, None] * freqs[None, :]], axis=-1)
    cos, sin = jnp.cos(ang), jnp.sin(ang)
    cos_h = jnp.concatenate([cos, cos], axis=-1)
    sin_h = jnp.concatenate([-sin, sin], axis=-1)
    return jnp.tile(cos_h, (1, RET_HEADS)), jnp.tile(sin_h, (1, RET_HEADS))


def _dft_tables(n):
    scale = (n * FNET_GROUP_DIM) ** -0.5
    jn = jnp.arange(n, dtype=I32)
    ang_n = ((jn[:, None] * jn[None, :]) % n).astype(F32) * (2.0 * np.pi / n)
    cs = jnp.concatenate([jnp.cos(ang_n), -jnp.sin(ang_n)], axis=1).astype(BF16)
    jc = jnp.arange(FNET_GROUP_DIM, dtype=I32)
    ang_c = ((jc[:, None] * jc[None, :]) % FNET_GROUP_DIM).astype(F32) * (2.0 * np.pi / FNET_GROUP_DIM)
    eye = jnp.eye(FNET_GROUPS, dtype=F32)
    bc = jnp.kron(eye, jnp.cos(ang_c) * scale).astype(BF16)
    bs = jnp.kron(eye, jnp.sin(ang_c) * scale).astype(BF16)
    return bc, bs, cs


def kernel(x, c, ctx, c_ctx, w_ada, b_ada, norm_pre_mix, norm_post_mix, w_in, ret_decay_fwd, ret_decay_bwd, ret_norm_gain, w_out, norm_pre_ffn, norm_post_ffn, peer_w_query, peer_sub_keys_1, peer_sub_keys_2, peer_u, peer_v):
    b, n, d = x.shape
    depth = w_ada.shape[0]
    assert depth == 1 and d == D_MODEL and n % RET_CHUNK == 0 and n % GRID_W == 0
    t = b * n
    tm = min(256, n)
    row = lambda a: a.reshape(1, -1)

    cos, sin = _rope_tables(n)
    bc, bs, cs = _dft_tables(n)

    pad = (-(b + 1)) % SUBLANES
    c_all = jnp.concatenate([c, c_ctx[None, :], jnp.zeros((pad, d), F32)], axis=0)
    mod = _ada(c_all, w_ada[0], row(b_ada[0]))
    sh1, sc1, g1, sh2, sc2, g2 = [mod[:b, i * d:(i + 1) * d].reshape(b, 1, d) for i in range(N_MOD)]
    csh1 = mod[b:b + 1, 0:d]
    csc1 = mod[b:b + 1, d:2 * d]

    a_f, a_b = ret_decay_fwd[0], ret_decay_bwd[0]
    a_f_lane = row(jnp.repeat(a_f, RET_DK))
    a_b_lane = row(jnp.repeat(a_b, RET_DK))

    w_in_b = w_in[0].astype(BF16)
    stf, stb = _ctx_states(ctx, row(norm_pre_mix[0]), csh1, csc1,
                           w_in_b[:, QK_WIDTH:QK_WIDTH + 2 * QK_WIDTH], a_f_lane, a_b_lane)

    q, k, v, gate, f = _inproj(x, sh1, sc1, row(norm_pre_mix[0]), w_in_b, cos, sin, tm)
    ret = _retention(q, k, v, gate, stf, stb, a_f, a_b, a_f_lane, a_b_lane, row(ret_norm_gain[0]))
    four = _fourier(f, bc, bs, cs, tm)

    x1, h2, qp = _outproj(ret, four, x, g1, sh2, sc2, row(norm_post_mix[0]), row(norm_pre_ffn[0]),
                          w_out[0].astype(BF16), peer_w_query[0].astype(BF16), tm)

    slab, shift, wts = _topk(qp.reshape(t, -1), peer_sub_keys_1[0].astype(BF16),
                             peer_sub_keys_2[0].astype(BF16))

    coef = _peer_u(slab, shift, h2.reshape(t, SUBLANES, LANES), wts, _pack_table(peer_u[0]))
    out = _peer_v(slab, shift, coef, x1.reshape(t, SUBLANES, LANES), g2.reshape(b, SUBLANES, LANES),
                  norm_post_ffn[0].reshape(SUBLANES, LANES), _pack_table(peer_v[0]), n)
    return out.reshape(b, n, d)
```

```python
import functools

import numpy as np
import jax
import jax.numpy as jnp
from jax import lax
from jax.experimental import pallas as pl
from jax.experimental.pallas import tpu as pltpu

F32 = jnp.float32
BF16 = jnp.bfloat16
I32 = jnp.int32
U32 = jnp.uint32

D_MODEL = 1024
GRID_W = 64
RET_HEADS = 8
RET_DK = 64
RET_WIDTH = 512
QK_WIDTH = 512
RET_CHUNK = 128
ROPE_PAIRS = RET_DK // 4
ROPE_BASE = 10000.0
FNET_GROUPS = 4
FNET_GROUP_DIM = 128
FNET_WIDTH = 512
IN_WIDTH = 2560
PEER_HEADS = 8
PEER_D_KEY = 256
PEER_N_KEYS = 128
PEER_TOPK = 16
N_MOD = 6
EPS = 1e-6

LANES = 128
SUBLANES = 8
VMEM_LIMIT = 56 * 1024 * 1024
N_PAIRS = RET_HEADS // 2


def _cparams(sem):
    return pltpu.CompilerParams(dimension_semantics=sem, vmem_limit_bytes=VMEM_LIMIT)


def _rms_rows(x, gain):
    ms = jnp.mean(x * x, axis=-1, keepdims=True)
    return x * lax.rsqrt(ms + EPS) * gain


def _silu(x):
    return x * jax.nn.sigmoid(x)


def _log_gamma(a):
    return jnp.log1p(-jnp.exp(a))


def _ada_kernel(c_ref, w_ref, b_ref, o_ref):
    s = _silu(c_ref[...])
    o_ref[...] = jnp.dot(s, w_ref[...], preferred_element_type=F32) + b_ref[...]


def _ada(c_all, w_ada, b_ada):
    rows = c_all.shape[0]
    width = w_ada.shape[1]
    tn = 1536
    return pl.pallas_call(
        _ada_kernel,
        out_shape=jax.ShapeDtypeStruct((rows, width), F32),
        grid=(width // tn,),
        in_specs=[
            pl.BlockSpec((rows, D_MODEL), lambda j: (0, 0)),
            pl.BlockSpec((D_MODEL, tn), lambda j: (0, j)),
            pl.BlockSpec((1, tn), lambda j: (0, j)),
        ],
        out_specs=pl.BlockSpec((rows, tn), lambda j: (0, j)),
        compiler_params=_cparams(("arbitrary",)),
        name="ada",
    )(c_all, w_ada, b_ada)


def _ctx_kernel(ctx_ref, g_ref, sh_ref, sc_ref, w_ref, af_ref, ab_ref, stf_ref, stb_ref):
    x = ctx_ref[0]
    n = x.shape[0]
    h = (_rms_rows(x, g_ref[...]) * (1.0 + sc_ref[...]) + sh_ref[...]).astype(BF16)
    kv = jnp.dot(h, w_ref[...], preferred_element_type=F32)
    k = kv[:, :QK_WIDTH] * (RET_DK ** -0.5)
    v = kv[:, QK_WIDTH:].astype(BF16)
    pos = lax.broadcasted_iota(I32, (n, QK_WIDTH), 0).astype(F32)
    lgf = _log_gamma(af_ref[...])
    lgb = _log_gamma(ab_ref[...])
    kf = (k * jnp.exp(lgf * (n - 1.0 - pos))).astype(BF16)
    kb = (k * jnp.exp(lgb * pos)).astype(BF16)
    tn = (((0,), (0,)), ((), ()))
    gf = lax.dot_general(kf, v, tn, preferred_element_type=F32)
    gb = lax.dot_general(kb, v, tn, preferred_element_type=F32)
    ri = lax.broadcasted_iota(I32, (LANES, LANES), 0) // RET_DK
    ci = lax.broadcasted_iota(I32, (LANES, LANES), 1) // RET_DK
    bd = ri == ci
    for p in range(N_PAIRS):
        sl = slice(p * LANES, (p + 1) * LANES)
        stf_ref[0, p] = jnp.where(bd, gf[sl, sl], 0.0)
        stb_ref[0, p] = jnp.where(bd, gb[sl, sl], 0.0)


def _ctx_states(ctx, gain, csh, csc, w_kv, a_f, a_b):
    b, n, _ = ctx.shape
    vec = lambda: pl.BlockSpec((1, D_MODEL), lambda i: (0, 0))
    lane = lambda: pl.BlockSpec((1, QK_WIDTH), lambda i: (0, 0))
    st = jax.ShapeDtypeStruct((b, N_PAIRS, LANES, LANES), F32)
    st_spec = lambda: pl.BlockSpec((1, N_PAIRS, LANES, LANES), lambda i: (i, 0, 0, 0))
    return pl.pallas_call(
        _ctx_kernel,
        out_shape=(st, st),
        grid=(b,),
        in_specs=[
            pl.BlockSpec((1, n, D_MODEL), lambda i: (i, 0, 0)),
            vec(), vec(), vec(),
            pl.BlockSpec((D_MODEL, 2 * QK_WIDTH), lambda i: (0, 0)),
            lane(), lane(),
        ],
        out_specs=(st_spec(), st_spec()),
        compiler_params=_cparams(("arbitrary",)),
        name="ctx_states",
    )(ctx, gain, csh, csc, w_kv, a_f, a_b)


def _inproj_kernel(x_ref, sh_ref, sc_ref, g_ref, w_ref, cos_ref, sin_ref,
                   q_ref, k_ref, v_ref, gate_ref, f_ref):
    x = x_ref[0]
    tm = x.shape[0]
    h = (_rms_rows(x, g_ref[...]) * (1.0 + sc_ref[0]) + sh_ref[0]).astype(BF16)

    def proj(lo, hi):
        return jnp.dot(h, w_ref[:, lo:hi], preferred_element_type=F32)

    cos = cos_ref[...]
    sin = sin_ref[...]
    lane = lax.broadcasted_iota(I32, (tm, QK_WIDTH), 1)
    first = (lane % RET_DK) < (RET_DK // 2)

    def rope(t):
        swapped = jnp.where(first,
                            pltpu.roll(t, QK_WIDTH - RET_DK // 2, 1),
                            pltpu.roll(t, RET_DK // 2, 1))
        return t * cos + swapped * sin

    q_ref[0] = rope(proj(0, 512)).astype(BF16)
    k_ref[0] = rope(proj(512, 1024) * (RET_DK ** -0.5)).astype(BF16)
    v_ref[0] = proj(1024, 1536).astype(BF16)
    gate_ref[0] = proj(1536, 2048)
    f_ref[0] = proj(2048, 2560).astype(BF16)


def _inproj(x, sh, sc, gain, w_in, cos, sin, tm):
    b, n, _ = x.shape
    mod = lambda: pl.BlockSpec((1, 1, D_MODEL), lambda i, j: (i, 0, 0))
    tab = lambda: pl.BlockSpec((tm, QK_WIDTH), lambda i, j: (j, 0))
    out = lambda: pl.BlockSpec((1, tm, 512), lambda i, j: (i, j, 0))
    sd = lambda dt: jax.ShapeDtypeStruct((b, n, 512), dt)
    return pl.pallas_call(
        _inproj_kernel,
        out_shape=(sd(BF16), sd(BF16), sd(BF16), sd(F32), sd(BF16)),
        grid=(b, n // tm),
        in_specs=[
            pl.BlockSpec((1, tm, D_MODEL), lambda i, j: (i, j, 0)),
            mod(), mod(),
            pl.BlockSpec((1, D_MODEL), lambda i, j: (0, 0)),
            pl.BlockSpec((D_MODEL, IN_WIDTH), lambda i, j: (0, 0)),
            tab(), tab(),
        ],
        out_specs=(out(), out(), out(), out(), out()),
        compiler_params=_cparams(("arbitrary", "arbitrary")),
        name="inproj",
    )(x, sh, sc, gain, w_in, cos, sin)


def _ret_kernel(af_s, ab_s, q_ref, k_ref, v_ref, gate_ref, stf_ref, stb_ref, af_ref, ab_ref,
                gain_ref, o_ref, mask_scr, qdf_scr, qdb_scr, kdf_scr, kdb_scr, sf_scr, sb_scr,
                sb_store):
    c = RET_CHUNK
    n = q_ref.shape[1]
    nc = n // c
    lgf = _log_gamma(af_ref[...])
    lgb = _log_gamma(ab_ref[...])
    cdf = jnp.exp(lgf * float(c))
    cdb = jnp.exp(lgb * float(c))

    @pl.when(pl.program_id(0) == 0)
    def _():
        r = lax.broadcasted_iota(I32, (c, RET_WIDTH), 0).astype(F32)
        qdf_scr[...] = jnp.exp(lgf * (r + 1.0))
        kdf_scr[...] = jnp.exp(lgf * (c - 1.0 - r))
        qdb_scr[...] = jnp.exp(lgb * (c - r))
        kdb_scr[...] = jnp.exp(lgb * r)
        ri = lax.broadcasted_iota(I32, (c, c), 0)
        ci = lax.broadcasted_iota(I32, (c, c), 1)
        d = (ri - ci).astype(F32)
        for hh in range(RET_HEADS):
            gf = _log_gamma(jnp.full((c, c), af_s[hh], F32))
            gb = _log_gamma(jnp.full((c, c), ab_s[hh], F32))
            mask_scr[hh] = jnp.where(d >= 0.0, jnp.exp(gf * jnp.maximum(d, 0.0)),
                                     jnp.exp(gb * jnp.maximum(-d, 0.0)))

    ri = lax.broadcasted_iota(I32, (LANES, LANES), 0) // RET_DK
    ci = lax.broadcasted_iota(I32, (LANES, LANES), 1) // RET_DK
    bd = ri == ci
    bd_ones = jnp.where(bd, 1.0, 0.0).astype(BF16)
    lane_head = lax.broadcasted_iota(I32, (c, LANES), 1) // RET_DK
    tn = (((0,), (0,)), ((), ()))
    nt = (((1,), (1,)), ((), ()))

    for p in range(N_PAIRS):
        sf_scr[p] = stf_ref[0, p]
        sb_scr[p] = stb_ref[0, p]

    def bwd_states(i, carry):
        cc = nc - 1 - i
        rows = pl.ds(pl.multiple_of(cc * c, c), c)
        for p in range(N_PAIRS):
            sl = slice(p * LANES, (p + 1) * LANES)
            sb = sb_scr[p]
            sb_store[cc, p] = sb.astype(BF16)
            kw = (k_ref[0, rows, sl].astype(F32) * kdb_scr[:, sl]).astype(BF16)
            g = lax.dot_general(kw, v_ref[0, rows, sl], tn, preferred_element_type=F32)
            sb_scr[p] = sb * cdb[:, sl] + jnp.where(bd, g, 0.0)
        return carry

    lax.fori_loop(0, nc, bwd_states, 0)

    def fwd(cc, carry):
        rows = pl.ds(pl.multiple_of(cc * c, c), c)
        for p in range(N_PAIRS):
            sl = slice(p * LANES, (p + 1) * LANES)
            qp = q_ref[0, rows, sl]
            kp = k_ref[0, rows, sl]
            vp = v_ref[0, rows, sl]
            qf = qp.astype(F32)
            sf = sf_scr[p]
            acc = jnp.dot((qf * qdf_scr[:, sl]).astype(BF16), sf.astype(BF16),
                          preferred_element_type=F32)
            acc += jnp.dot((qf * qdb_scr[:, sl]).astype(BF16), sb_store[cc, p],
                           preferred_element_type=F32)
            for j in range(2):
                own = lane_head == j
                qm = jnp.where(own, qp, jnp.zeros_like(qp))
                s = lax.dot_general(qm, kp, nt, preferred_element_type=F32)
                pm = (s * mask_scr[2 * p + j]).astype(BF16)
                vm = jnp.where(own, vp, jnp.zeros_like(vp))
                acc += jnp.dot(pm, vm, preferred_element_type=F32)
            sq = acc * acc
            sq_hi = sq.astype(BF16)
            sq_lo = (sq - sq_hi.astype(F32)).astype(BF16)
            ms = (jnp.dot(sq_hi, bd_ones, preferred_element_type=F32)
                  + jnp.dot(sq_lo, bd_ones, preferred_element_type=F32)) * (1.0 / RET_DK)
            y = acc * lax.rsqrt(ms + EPS) * gain_ref[:, sl] * _silu(gate_ref[0, rows, sl])
            o_ref[0, rows, sl] = y.astype(BF16)
            kw = (kp.astype(F32) * kdf_scr[:, sl]).astype(BF16)
            g = lax.dot_general(kw, vp, tn, preferred_element_type=F32)
            sf_scr[p] = sf * cdf[:, sl] + jnp.where(bd, g, 0.0)
        return carry

    lax.fori_loop(0, nc, fwd, 0)


def _retention(q, k, v, gate, stf, stb, a_f, a_b, a_f_lane, a_b_lane, gain):
    b, n, _ = q.shape
    nc = n // RET_CHUNK
    seq = lambda: pl.BlockSpec((1, n, RET_WIDTH), lambda i, *_: (i, 0, 0))
    st = lambda: pl.BlockSpec((1, N_PAIRS, LANES, LANES), lambda i, *_: (i, 0, 0, 0))
    lane = lambda: pl.BlockSpec((1, RET_WIDTH), lambda i, *_: (0, 0))
    grid_spec = pltpu.PrefetchScalarGridSpec(
        num_scalar_prefetch=2,
        grid=(b,),
        in_specs=[seq(), seq(), seq(), seq(), st(), st(), lane(), lane(), lane()],
        out_specs=seq(),
        scratch_shapes=[
            pltpu.VMEM((RET_HEADS, RET_CHUNK, RET_CHUNK), F32),
            pltpu.VMEM((RET_CHUNK, RET_WIDTH), F32),
            pltpu.VMEM((RET_CHUNK, RET_WIDTH), F32),
            pltpu.VMEM((RET_CHUNK, RET_WIDTH), F32),
            pltpu.VMEM((RET_CHUNK, RET_WIDTH), F32),
            pltpu.VMEM((N_PAIRS, LANES, LANES), F32),
            pltpu.VMEM((N_PAIRS, LANES, LANES), F32),
            pltpu.VMEM((nc, N_PAIRS, LANES, LANES), BF16),
        ],
    )
    return pl.pallas_call(
        _ret_kernel,
        out_shape=jax.ShapeDtypeStruct((b, n, RET_WIDTH), BF16),
        grid_spec=grid_spec,
        compiler_params=_cparams(("arbitrary",)),
        name="retention",
    )(a_f, a_b, q, k, v, gate, stf, stb, a_f_lane, a_b_lane, gain)


def _four_kernel(f_ref, bc_ref, bs_ref, cs_ref, o_ref, z_scr):
    n = f_ref.shape[1]

    @pl.when(pl.program_id(1) == 0)
    def _():
        f = f_ref[0]
        z_scr[0:n, :] = jnp.dot(f, bc_ref[...], preferred_element_type=F32).astype(BF16)
        z_scr[n:2 * n, :] = jnp.dot(f, bs_ref[...], preferred_element_type=F32).astype(BF16)

    o_ref[0] = jnp.dot(cs_ref[...], z_scr[...], preferred_element_type=F32).astype(BF16)


def _fourier(f, bc, bs, cs, tn):
    b, n, _ = f.shape
    return pl.pallas_call(
        _four_kernel,
        out_shape=jax.ShapeDtypeStruct((b, n, FNET_WIDTH), BF16),
        grid=(b, n // tn),
        in_specs=[
            pl.BlockSpec((1, n, FNET_WIDTH), lambda i, j: (i, 0, 0)),
            pl.BlockSpec((FNET_WIDTH, FNET_WIDTH), lambda i, j: (0, 0)),
            pl.BlockSpec((FNET_WIDTH, FNET_WIDTH), lambda i, j: (0, 0)),
            pl.BlockSpec((tn, 2 * n), lambda i, j: (j, 0)),
        ],
        out_specs=pl.BlockSpec((1, tn, FNET_WIDTH), lambda i, j: (i, j, 0)),
        scratch_shapes=[pltpu.VMEM((2 * n, FNET_WIDTH), BF16)],
        compiler_params=_cparams(("arbitrary", "arbitrary")),
        name="fourier",
    )(f, bc, bs, cs)


def _outp_kernel(ret_ref, four_ref, x_ref, g1_ref, sh2_ref, sc2_ref, gpost_ref, gpre_ref,
                 wo_ref, wq_ref, x1_ref, h2_ref, qp_ref):
    mix = jnp.dot(ret_ref[0], wo_ref[0:RET_WIDTH, :], preferred_element_type=F32)
    mix += jnp.dot(four_ref[0], wo_ref[RET_WIDTH:, :], preferred_element_type=F32)
    x1 = x_ref[0] + g1_ref[0] * _rms_rows(mix, gpost_ref[...])
    x1_ref[0] = x1
    h2 = _rms_rows(x1, gpre_ref[...]) * (1.0 + sc2_ref[0]) + sh2_ref[0]
    h2_ref[0] = h2
    qp_ref[0] = jnp.dot(h2.astype(BF16), wq_ref[...], preferred_element_type=F32).astype(BF16)


def _outproj(ret, four, x, g1, sh2, sc2, gpost, gpre, w_out, w_q, tm):
    b, n, _ = x.shape
    qw = w_q.shape[1]
    mod = lambda: pl.BlockSpec((1, 1, D_MODEL), lambda i, j: (i, 0, 0))
    vec = lambda: pl.BlockSpec((1, D_MODEL), lambda i, j: (0, 0))
    half = lambda: pl.BlockSpec((1, tm, 512), lambda i, j: (i, j, 0))
    full = lambda: pl.BlockSpec((1, tm, D_MODEL), lambda i, j: (i, j, 0))
    return pl.pallas_call(
        _outp_kernel,
        out_shape=(jax.ShapeDtypeStruct((b, n, D_MODEL), F32),
                   jax.ShapeDtypeStruct((b, n, D_MODEL), F32),
                   jax.ShapeDtypeStruct((b, n, qw), BF16)),
        grid=(b, n // tm),
        in_specs=[half(), half(), full(), mod(), mod(), mod(), vec(), vec(),
                  pl.BlockSpec(w_out.shape, lambda i, j: (0, 0)),
                  pl.BlockSpec(w_q.shape, lambda i, j: (0, 0))],
        out_specs=(full(), full(), pl.BlockSpec((1, tm, qw), lambda i, j: (i, j, 0))),
        compiler_params=_cparams(("arbitrary", "arbitrary")),
        name="outproj",
    )(ret, four, x, g1, sh2, sc2, gpost, gpre, w_out, w_q)


def _top_rows(s, kk):
    rows, tq = s.shape
    rid = lax.broadcasted_iota(I32, (rows, tq), 0).astype(F32)
    slot = lax.broadcasted_iota(I32, (kk, tq), 0)
    vals = jnp.zeros((kk, tq), F32)
    idxs = jnp.zeros((kk, tq), F32)
    for it in range(kk):
        m = jnp.max(s, axis=0, keepdims=True)
        am = jnp.min(jnp.where(s == m, rid, float(rows)), axis=0, keepdims=True)
        vals = jnp.where(slot == it, m, vals)
        idxs = jnp.where(slot == it, am, idxs)
        s = jnp.where(rid == am, -jnp.inf, s)
    return vals, idxs


def _pick_rows(table, sel, kk):
    out = jnp.zeros_like(table)
    for i in range(kk):
        out = jnp.where(sel == float(i), table[i:i + 1, :], out)
    return out


def _topk_kernel(qp_ref, k1_ref, k2_ref, off_ref, wts_ref):
    kk = PEER_TOPK
    half = PEER_D_KEY // 2
    nt = (((1,), (1,)), ((), ()))
    k1 = k1_ref[...]
    k2 = k2_ref[...]
    ids_all, w_all = [], []
    for hh in range(PEER_HEADS):
        q1 = qp_ref[:, hh * PEER_D_KEY: hh * PEER_D_KEY + half]
        q2 = qp_ref[:, hh * PEER_D_KEY + half: (hh + 1) * PEER_D_KEY]
        s1 = lax.dot_general(k1, q1, nt, preferred_element_type=F32)
        s2 = lax.dot_general(k2, q2, nt, preferred_element_type=F32)
        v1, i1 = _top_rows(s1, kk)
        v2, i2 = _top_rows(s2, kk)
        cand = jnp.concatenate([v1[i:i + 1, :] + v2 for i in range(kk)], axis=0)
        tv, ti = _top_rows(cand, kk)
        hi = jnp.floor(ti * (1.0 / kk))
        lo = ti - hi * kk
        ids = _pick_rows(i1, hi, kk) * float(PEER_N_KEYS) + _pick_rows(i2, lo, kk)
        p = jnp.exp(tv - tv[0:1, :])
        w = p / jnp.sum(p, axis=0, keepdims=True)
        ids_all.append(ids)
        w_all.append(w)
    ids_t = jnp.concatenate(ids_all, axis=0).T.astype(I32)
    slot = lax.broadcasted_iota(I32, ids_t.shape, 1)
    off_ref[...] = ids_t * TABLE_ROWS_PER_EXPERT + jnp.where((slot % SUBLANES) < 4, TABLE_FRONT_PAD, 0)
    wts_ref[...] = jnp.concatenate(w_all, axis=0).T


def _topk(qp, k1, k2):
    t, qw = qp.shape
    tq = LANES
    out = lambda: pl.BlockSpec((tq, PEER_SLOTS), lambda i: (i, 0))
    key = lambda: pl.BlockSpec((PEER_N_KEYS, PEER_D_KEY // 2), lambda i: (0, 0))
    return pl.pallas_call(
        _topk_kernel,
        out_shape=(jax.ShapeDtypeStruct((t, PEER_SLOTS), I32),
                   jax.ShapeDtypeStruct((t, PEER_SLOTS), F32)),
        grid=(t // tq,),
        in_specs=[pl.BlockSpec((tq, qw), lambda i: (i, 0)), key(), key()],
        out_specs=(out(), out()),
        compiler_params=_cparams(("arbitrary",)),
        name="peer_topk",
    )(qp, k1, k2)


PEER_SLOTS = PEER_HEADS * PEER_TOPK
PEER_TOKENS_PER_STEP = 16
TABLE_ROWS_PER_EXPERT = 4
TABLE_FRONT_PAD = 4
TABLE_BACK_PAD = 8
HI_MASK = np.uint32(0xFFFF0000)


def _pack_table(w):
    e = w.shape[0]
    bits = lax.bitcast_convert_type(w.astype(BF16), jnp.uint16).astype(U32)
    bits = bits.reshape(e, 2, TABLE_ROWS_PER_EXPERT, LANES)
    words = (bits[:, 0] | (bits[:, 1] << 16)).reshape(e * TABLE_ROWS_PER_EXPERT, LANES)
    return jnp.concatenate([jnp.zeros((TABLE_FRONT_PAD, LANES), U32), words,
                            jnp.zeros((TABLE_BACK_PAD, LANES), U32)], axis=0)


def _expert_pair(tbl_ref, off_s, j, g, i, low4):
    wa = tbl_ref[pl.ds(off_s[j, g * SUBLANES + i], SUBLANES), :]
    wb = tbl_ref[pl.ds(off_s[j, g * SUBLANES + i + 4], SUBLANES), :]
    return jnp.where(low4, wa, wb)


def _gelu_exact(x):
    return 0.5 * x * (1.0 + lax.erf(x * (2.0 ** -0.5)))


def _peer_u_kernel(off_s, h_ref, wts_ref, tbl_ref, coef_ref):
    nt = (((1,), (1,)), ((), ()))
    ones = jnp.ones((SUBLANES, LANES), BF16)
    sub = lax.broadcasted_iota(I32, (SUBLANES, LANES), 0)
    low4 = sub < 4
    m2 = (sub % 4) < 2
    m1 = (sub % 2) == 0

    def token_act(j):
        hj = h_ref[j]
        hr = pltpu.roll(hj, 4, 0)
        hlo = jnp.where(low4, hj, hr)
        hhi = jnp.where(low4, hr, hj)

        def prod(g, i):
            w = _expert_pair(tbl_ref, off_s, j, g, i, low4)
            lo = lax.bitcast_convert_type(w << 16, F32)
            hi = lax.bitcast_convert_type(w & HI_MASK, F32)
            return lo * hlo + hi * hhi

        groups = []
        for g in range(PEER_SLOTS // SUBLANES):
            q1, q2, q3, q4 = prod(g, 0), prod(g, 2), prod(g, 1), prod(g, 3)
            t1 = jnp.where(m2, q1 + pltpu.roll(q1, 6, 0), q2 + pltpu.roll(q2, 2, 0))
            t2 = jnp.where(m2, q3 + pltpu.roll(q3, 6, 0), q4 + pltpu.roll(q4, 2, 0))
            groups.append(jnp.where(m1, t1 + pltpu.roll(t1, 7, 0), t2 + pltpu.roll(t2, 1, 0)))
        part = jnp.concatenate(groups, axis=0)
        hi = part.astype(BF16)
        lo = (part - hi.astype(F32)).astype(BF16)
        act = (lax.dot_general(ones, hi, nt, preferred_element_type=F32)
               + lax.dot_general(ones, lo, nt, preferred_element_type=F32))
        return act[0:1, :]

    acts = jnp.concatenate([token_act(j) for j in range(PEER_TOKENS_PER_STEP)], axis=0)
    coef_ref[...] = wts_ref[...] * _gelu_exact(acts)


def _peer_u(off, h3, wts, tbl):
    t = h3.shape[0]
    tb = PEER_TOKENS_PER_STEP
    return pl.pallas_call(
        _peer_u_kernel,
        out_shape=jax.ShapeDtypeStruct((t, PEER_SLOTS), F32),
        grid=(t // tb,),
        in_specs=[pl.BlockSpec((tb, PEER_SLOTS), lambda i: (i, 0), memory_space=pltpu.SMEM),
                  pl.BlockSpec((tb, SUBLANES, LANES), lambda i: (i, 0, 0)),
                  pl.BlockSpec((tb, PEER_SLOTS), lambda i: (i, 0)),
                  pl.BlockSpec(memory_space=pltpu.VMEM)],
        out_specs=pl.BlockSpec((tb, PEER_SLOTS), lambda i: (i, 0)),
        compiler_params=_cparams(("arbitrary",)),
        name="peer_u",
    )(off, h3, wts, tbl)


def _peer_v_tables():
    col = np.arange(PEER_SLOTS * SUBLANES)
    q, r = col // 16, col % 16
    slot = SUBLANES * (q // 4) + (q % 4) + np.where(r < 8, 0, 4)
    expand = (slot[None, :] == np.arange(PEER_SLOTS)[:, None]).astype(np.float32)
    out_row = (r // 2) % 4 + 4 * (r % 2)
    rowsel = (out_row[None, :] == np.arange(SUBLANES)[:, None]).astype(np.float32)
    return jnp.asarray(expand, BF16), jnp.asarray(rowsel, F32)


def _peer_v_kernel(off_s, coef_ref, x1_ref, g2_ref, gain_ref, expand_ref, rowsel_ref, tbl_ref, o_ref):
    sub = lax.broadcasted_iota(I32, (SUBLANES, LANES), 0)
    low4 = sub < 4
    cexp = jnp.dot(coef_ref[...].astype(BF16), expand_ref[...], preferred_element_type=F32)
    rowsel = rowsel_ref[...]

    def token_out(j):
        tiles = []
        for q in range(PEER_SLOTS // 2):
            w = _expert_pair(tbl_ref, off_s, j, q // 4, q % 4, low4)
            tiles.append(pltpu.bitcast(w, BF16))
        wmat = jnp.concatenate(tiles, axis=0)
        c = (cexp[j:j + 1, :] * rowsel).astype(BF16)
        return jnp.dot(c, wmat, preferred_element_type=F32)

    out = jnp.stack([token_out(j) for j in range(PEER_TOKENS_PER_STEP)], axis=0)
    ms = jnp.sum(out * out, axis=(1, 2), keepdims=True) * (1.0 / D_MODEL)
    y = out * lax.rsqrt(ms + EPS) * gain_ref[...]
    o_ref[...] = x1_ref[...] + g2_ref[...] * y


def _peer_v(off, coef, x1_3, g2_3, gain_3, tbl, tokens_per_sample):
    t = x1_3.shape[0]
    tb = PEER_TOKENS_PER_STEP
    steps_per_sample = tokens_per_sample // tb
    expand, rowsel = _peer_v_tables()
    tok = lambda: pl.BlockSpec((tb, SUBLANES, LANES), lambda i: (i, 0, 0))
    return pl.pallas_call(
        _peer_v_kernel,
        out_shape=jax.ShapeDtypeStruct((t, SUBLANES, LANES), F32),
        grid=(t // tb,),
        in_specs=[pl.BlockSpec((tb, PEER_SLOTS), lambda i: (i, 0), memory_space=pltpu.SMEM),
                  pl.BlockSpec((tb, PEER_SLOTS), lambda i: (i, 0)),
                  tok(),
                  pl.BlockSpec((1, SUBLANES, LANES), lambda i: (i // steps_per_sample, 0, 0)),
                  pl.BlockSpec((SUBLANES, LANES), lambda i: (0, 0)),
                  pl.BlockSpec(expand.shape, lambda i: (0, 0)),
                  pl.BlockSpec(rowsel.shape, lambda i: (0, 0)),
                  pl.BlockSpec(memory_space=pltpu.VMEM)],
        out_specs=tok(),
        compiler_params=_cparams(("arbitrary",)),
        name="peer_v",
    )(off, coef, x1_3, g2_3, gain_3, expand, rowsel, tbl)


def _rope_tables(n):
    rows = n // GRID_W
    row_ids = jnp.repeat(jnp.arange(rows, dtype=F32), GRID_W)
    col_ids = jnp.tile(jnp.arange(GRID_W, dtype=F32), rows)
    freqs = ROPE_BASE ** (-jnp.arange(ROPE_PAIRS, dtype=F32) / ROPE_PAIRS)
    ang = jnp.concatenate([row_ids[:, None] * freqs[None, :], col_ids[:, None] * freqs[None, :]], axis=-1)
    cos, sin = jnp.cos(ang), jnp.sin(ang)
    cos_h = jnp.concatenate([cos, cos], axis=-1)
    sin_h = jnp.concatenate([-sin, sin], axis=-1)
    return jnp.tile(cos_h, (1, RET_HEADS)), jnp.tile(sin_h, (1, RET_HEADS))


def _dft_tables(n):
    scale = (n * FNET_GROUP_DIM) ** -0.5
    jn = jnp.arange(n, dtype=I32)
    ang_n = ((jn[:, None] * jn[None, :]) % n).astype(F32) * (2.0 * np.pi / n)
    cs = jnp.concatenate([jnp.cos(ang_n), -jnp.sin(ang_n)], axis=1).astype(BF16)
    jc = jnp.arange(FNET_GROUP_DIM, dtype=I32)
    ang_c = ((jc[:, None] * jc[None, :]) % FNET_GROUP_DIM).astype(F32) * (2.0 * np.pi / FNET_GROUP_DIM)
    eye = jnp.eye(FNET_GROUPS, dtype=F32)
    bc = jnp.kron(eye, jnp.cos(ang_c) * scale).astype(BF16)
    bs = jnp.kron(eye, jnp.sin(ang_c) * scale).astype(BF16)
    return bc, bs, cs


def kernel(x, c, ctx, c_ctx, w_ada, b_ada, norm_pre_mix, norm_post_mix, w_in, ret_decay_fwd, ret_decay_bwd, ret_norm_gain, w_out, norm_pre_ffn, norm_post_ffn, peer_w_query, peer_sub_keys_1, peer_sub_keys_2, peer_u, peer_v):
    b, n, d = x.shape
    depth = w_ada.shape[0]
    assert depth == 1 and d == D_MODEL and n % RET_CHUNK == 0 and n % GRID_W == 0
    t = b * n
    tm = min(256, n)
    row = lambda a: a.reshape(1, -1)

    cos, sin = _rope_tables(n)
    bc, bs, cs = _dft_tables(n)

    pad = (-(b + 1)) % SUBLANES
    c_all = jnp.concatenate([c, c_ctx[None, :], jnp.zeros((pad, d), F32)], axis=0)
    mod = _ada(c_all, w_ada[0], row(b_ada[0]))
    sh1, sc1, g1, sh2, sc2, g2 = [mod[:b, i * d:(i + 1) * d].reshape(b, 1, d) for i in range(N_MOD)]
    csh1 = mod[b:b + 1, 0:d]
    csc1 = mod[b:b + 1, d:2 * d]

    a_f, a_b = ret_decay_fwd[0], ret_decay_bwd[0]
    a_f_lane = row(jnp.repeat(a_f, RET_DK))
    a_b_lane = row(jnp.repeat(a_b, RET_DK))

    w_in_b = w_in[0].astype(BF16)
    stf, stb = _ctx_states(ctx, row(norm_pre_mix[0]), csh1, csc1,
                           w_in_b[:, QK_WIDTH:QK_WIDTH + 2 * QK_WIDTH], a_f_lane, a_b_lane)

    q, k, v, gate, f = _inproj(x, sh1, sc1, row(norm_pre_mix[0]), w_in_b, cos, sin, tm)
    ret = _retention(q, k, v, gate, stf, stb, a_f, a_b, a_f_lane, a_b_lane, row(ret_norm_gain[0]))
    four = _fourier(f, bc, bs, cs, tm)

    x1, h2, qp = _outproj(ret, four, x, g1, sh2, sc2, row(norm_post_mix[0]), row(norm_pre_ffn[0]),
                          w_out[0].astype(BF16), peer_w_query[0].astype(BF16), tm)

    off, wts = _topk(qp.reshape(t, -1), peer_sub_keys_1[0].astype(BF16), peer_sub_keys_2[0].astype(BF16))

    coef = _peer_u(off, h2.reshape(t, SUBLANES, LANES), wts, _pack_table(peer_u[0]))
    out = _peer_v(off, coef, x1.reshape(t, SUBLANES, LANES), g2.reshape(b, SUBLANES, LANES),
                  norm_post_ffn[0].reshape(SUBLANES, LANES), _pack_table(peer_v[0]), n)
    return out.reshape(b, n, d)
```

```python
import functools

import numpy as np
import jax
import jax.numpy as jnp
from jax import lax
from jax.experimental import pallas as pl
from jax.experimental.pallas import tpu as pltpu

F32 = jnp.float32
BF16 = jnp.bfloat16
I32 = jnp.int32
U32 = jnp.uint32

D_MODEL = 1024
GRID_W = 64
RET_HEADS = 8
RET_DK = 64
RET_WIDTH = 512
QK_WIDTH = 512
RET_CHUNK = 128
ROPE_PAIRS = RET_DK // 4
ROPE_BASE = 10000.0
FNET_GROUPS = 4
FNET_GROUP_DIM = 128
FNET_WIDTH = 512
IN_WIDTH = 2560
PEER_HEADS = 8
PEER_D_KEY = 256
PEER_N_KEYS = 128
PEER_TOPK = 16
N_MOD = 6
EPS = 1e-6

LANES = 128
SUBLANES = 8
VMEM_LIMIT = 56 * 1024 * 1024
N_PAIRS = RET_HEADS // 2


def _cparams(sem):
    return pltpu.CompilerParams(dimension_semantics=sem, vmem_limit_bytes=VMEM_LIMIT)


def _rms_rows(x, gain):
    ms = jnp.mean(x * x, axis=-1, keepdims=True)
    return x * lax.rsqrt(ms + EPS) * gain


def _silu(x):
    return x * jax.nn.sigmoid(x)


def _log_gamma(a):
    return jnp.log1p(-jnp.exp(a))


def _ada_kernel(c_ref, w_ref, b_ref, o_ref):
    s = _silu(c_ref[...])
    o_ref[...] = jnp.dot(s, w_ref[...], preferred_element_type=F32) + b_ref[...]


def _ada(c_all, w_ada, b_ada):
    rows = c_all.shape[0]
    width = w_ada.shape[1]
    tn = 1536
    return pl.pallas_call(
        _ada_kernel,
        out_shape=jax.ShapeDtypeStruct((rows, width), F32),
        grid=(width // tn,),
        in_specs=[
            pl.BlockSpec((rows, D_MODEL), lambda j: (0, 0)),
            pl.BlockSpec((D_MODEL, tn), lambda j: (0, j)),
            pl.BlockSpec((1, tn), lambda j: (0, j)),
        ],
        out_specs=pl.BlockSpec((rows, tn), lambda j: (0, j)),
        compiler_params=_cparams(("arbitrary",)),
        name="ada",
    )(c_all, w_ada, b_ada)


def _ctx_kernel(ctx_ref, g_ref, sh_ref, sc_ref, w_ref, af_ref, ab_ref, stf_ref, stb_ref):
    x = ctx_ref[0]
    n = x.shape[0]
    h = (_rms_rows(x, g_ref[...]) * (1.0 + sc_ref[...]) + sh_ref[...]).astype(BF16)
    kv = jnp.dot(h, w_ref[...], preferred_element_type=F32)
    k = kv[:, :QK_WIDTH] * (RET_DK ** -0.5)
    v = kv[:, QK_WIDTH:].astype(BF16)
    pos = lax.broadcasted_iota(I32, (n, QK_WIDTH), 0).astype(F32)
    lgf = _log_gamma(af_ref[...])
    lgb = _log_gamma(ab_ref[...])
    kf = (k * jnp.exp(lgf * (n - 1.0 - pos))).astype(BF16)
    kb = (k * jnp.exp(lgb * pos)).astype(BF16)
    tn = (((0,), (0,)), ((), ()))
    gf = lax.dot_general(kf, v, tn, preferred_element_type=F32)
    gb = lax.dot_general(kb, v, tn, preferred_element_type=F32)
    ri = lax.broadcasted_iota(I32, (LANES, LANES), 0) // RET_DK
    ci = lax.broadcasted_iota(I32, (LANES, LANES), 1) // RET_DK
    bd = ri == ci
    for p in range(N_PAIRS):
        sl = slice(p * LANES, (p + 1) * LANES)
        stf_ref[0, p] = jnp.where(bd, gf[sl, sl], 0.0)
        stb_ref[0, p] = jnp.where(bd, gb[sl, sl], 0.0)


def _ctx_states(ctx, gain, csh, csc, w_kv, a_f, a_b):
    b, n, _ = ctx.shape
    vec = lambda: pl.BlockSpec((1, D_MODEL), lambda i: (0, 0))
    lane = lambda: pl.BlockSpec((1, QK_WIDTH), lambda i: (0, 0))
    st = jax.ShapeDtypeStruct((b, N_PAIRS, LANES, LANES), F32)
    st_spec = lambda: pl.BlockSpec((1, N_PAIRS, LANES, LANES), lambda i: (i, 0, 0, 0))
    return pl.pallas_call(
        _ctx_kernel,
        out_shape=(st, st),
        grid=(b,),
        in_specs=[
            pl.BlockSpec((1, n, D_MODEL), lambda i: (i, 0, 0)),
            vec(), vec(), vec(),
            pl.BlockSpec((D_MODEL, 2 * QK_WIDTH), lambda i: (0, 0)),
            lane(), lane(),
        ],
        out_specs=(st_spec(), st_spec()),
        compiler_params=_cparams(("arbitrary",)),
        name="ctx_states",
    )(ctx, gain, csh, csc, w_kv, a_f, a_b)


def _inproj_kernel(x_ref, sh_ref, sc_ref, g_ref, w_ref, cos_ref, sin_ref,
                   q_ref, k_ref, v_ref, gate_ref, f_ref):
    x = x_ref[0]
    tm = x.shape[0]
    h = (_rms_rows(x, g_ref[...]) * (1.0 + sc_ref[0]) + sh_ref[0]).astype(BF16)

    def proj(lo, hi):
        return jnp.dot(h, w_ref[:, lo:hi], preferred_element_type=F32)

    cos = cos_ref[...]
    sin = sin_ref[...]
    lane = lax.broadcasted_iota(I32, (tm, QK_WIDTH), 1)
    first = (lane % RET_DK) < (RET_DK // 2)

    def rope(t):
        swapped = jnp.where(first,
                            pltpu.roll(t, QK_WIDTH - RET_DK // 2, 1),
                            pltpu.roll(t, RET_DK // 2, 1))
        return t * cos + swapped * sin

    q_ref[0] = rope(proj(0, 512)).astype(BF16)
    k_ref[0] = rope(proj(512, 1024) * (RET_DK ** -0.5)).astype(BF16)
    v_ref[0] = proj(1024, 1536).astype(BF16)
    gate_ref[0] = proj(1536, 2048)
    f_ref[0] = proj(2048, 2560).astype(BF16)


def _inproj(x, sh, sc, gain, w_in, cos, sin, tm):
    b, n, _ = x.shape
    mod = lambda: pl.BlockSpec((1, 1, D_MODEL), lambda i, j: (i, 0, 0))
    tab = lambda: pl.BlockSpec((tm, QK_WIDTH), lambda i, j: (j, 0))
    out = lambda: pl.BlockSpec((1, tm, 512), lambda i, j: (i, j, 0))
    sd = lambda dt: jax.ShapeDtypeStruct((b, n, 512), dt)
    return pl.pallas_call(
        _inproj_kernel,
        out_shape=(sd(BF16), sd(BF16), sd(BF16), sd(F32), sd(BF16)),
        grid=(b, n // tm),
        in_specs=[
            pl.BlockSpec((1, tm, D_MODEL), lambda i, j: (i, j, 0)),
            mod(), mod(),
            pl.BlockSpec((1, D_MODEL), lambda i, j: (0, 0)),
            pl.BlockSpec((D_MODEL, IN_WIDTH), lambda i, j: (0, 0)),
            tab(), tab(),
        ],
        out_specs=(out(), out(), out(), out(), out()),
        compiler_params=_cparams(("arbitrary", "arbitrary")),
        name="inproj",
    )(x, sh, sc, gain, w_in, cos, sin)


def _ret_kernel(af_s, ab_s, q_ref, k_ref, v_ref, gate_ref, stf_ref, stb_ref, af_ref, ab_ref,
                gain_ref, o_ref, mask_scr, qdf_scr, qdb_scr, kdf_scr, kdb_scr, sf_scr, sb_scr,
                sb_store):
    c = RET_CHUNK
    n = q_ref.shape[1]
    nc = n // c
    lgf = _log_gamma(af_ref[...])
    lgb = _log_gamma(ab_ref[...])
    cdf = jnp.exp(lgf * float(c))
    cdb = jnp.exp(lgb * float(c))

    @pl.when(pl.program_id(0) == 0)
    def _():
        r = lax.broadcasted_iota(I32, (c, RET_WIDTH), 0).astype(F32)
        qdf_scr[...] = jnp.exp(lgf * (r + 1.0))
        kdf_scr[...] = jnp.exp(lgf * (c - 1.0 - r))
        qdb_scr[...] = jnp.exp(lgb * (c - r))
        kdb_scr[...] = jnp.exp(lgb * r)
        ri = lax.broadcasted_iota(I32, (c, c), 0)
        ci = lax.broadcasted_iota(I32, (c, c), 1)
        d = (ri - ci).astype(F32)
        for hh in range(RET_HEADS):
            gf = _log_gamma(jnp.full((c, c), af_s[hh], F32))
            gb = _log_gamma(jnp.full((c, c), ab_s[hh], F32))
            mask_scr[hh] = jnp.where(d >= 0.0, jnp.exp(gf * jnp.maximum(d, 0.0)),
                                     jnp.exp(gb * jnp.maximum(-d, 0.0)))

    ri = lax.broadcasted_iota(I32, (LANES, LANES), 0) // RET_DK
    ci = lax.broadcasted_iota(I32, (LANES, LANES), 1) // RET_DK
    bd = ri == ci
    bd_ones = jnp.where(bd, 1.0, 0.0).astype(BF16)
    lane_head = lax.broadcasted_iota(I32, (c, LANES), 1) // RET_DK
    tn = (((0,), (0,)), ((), ()))
    nt = (((1,), (1,)), ((), ()))

    for p in range(N_PAIRS):
        sf_scr[p] = stf_ref[0, p]
        sb_scr[p] = stb_ref[0, p]

    def bwd_states(i, carry):
        cc = nc - 1 - i
        rows = pl.ds(pl.multiple_of(cc * c, c), c)
        for p in range(N_PAIRS):
            sl = slice(p * LANES, (p + 1) * LANES)
            sb = sb_scr[p]
            sb_store[cc, p] = sb.astype(BF16)
            kw = (k_ref[0, rows, sl].astype(F32) * kdb_scr[:, sl]).astype(BF16)
            g = lax.dot_general(kw, v_ref[0, rows, sl], tn, preferred_element_type=F32)
            sb_scr[p] = sb * cdb[:, sl] + jnp.where(bd, g, 0.0)
        return carry

    lax.fori_loop(0, nc, bwd_states, 0)

    def fwd(cc, carry):
        rows = pl.ds(pl.multiple_of(cc * c, c), c)
        for p in range(N_PAIRS):
            sl = slice(p * LANES, (p + 1) * LANES)
            qp = q_ref[0, rows, sl]
            kp = k_ref[0, rows, sl]
            vp = v_ref[0, rows, sl]
            qf = qp.astype(F32)
            sf = sf_scr[p]
            acc = jnp.dot((qf * qdf_scr[:, sl]).astype(BF16), sf.astype(BF16),
                          preferred_element_type=F32)
            acc += jnp.dot((qf * qdb_scr[:, sl]).astype(BF16), sb_store[cc, p],
                           preferred_element_type=F32)
            for j in range(2):
                own = lane_head == j
                qm = jnp.where(own, qp, jnp.zeros_like(qp))
                s = lax.dot_general(qm, kp, nt, preferred_element_type=F32)
                pm = (s * mask_scr[2 * p + j]).astype(BF16)
                vm = jnp.where(own, vp, jnp.zeros_like(vp))
                acc += jnp.dot(pm, vm, preferred_element_type=F32)
            sq = acc * acc
            sq_hi = sq.astype(BF16)
            sq_lo = (sq - sq_hi.astype(F32)).astype(BF16)
            ms = (jnp.dot(sq_hi, bd_ones, preferred_element_type=F32)
                  + jnp.dot(sq_lo, bd_ones, preferred_element_type=F32)) * (1.0 / RET_DK)
            y = acc * lax.rsqrt(ms + EPS) * gain_ref[:, sl] * _silu(gate_ref[0, rows, sl])
            o_ref[0, rows, sl] = y.astype(BF16)
            kw = (kp.astype(F32) * kdf_scr[:, sl]).astype(BF16)
            g = lax.dot_general(kw, vp, tn, preferred_element_type=F32)
            sf_scr[p] = sf * cdf[:, sl] + jnp.where(bd, g, 0.0)
        return carry

    lax.fori_loop(0, nc, fwd, 0)


def _retention(q, k, v, gate, stf, stb, a_f, a_b, a_f_lane, a_b_lane, gain):
    b, n, _ = q.shape
    nc = n // RET_CHUNK
    seq = lambda: pl.BlockSpec((1, n, RET_WIDTH), lambda i, *_: (i, 0, 0))
    st = lambda: pl.BlockSpec((1, N_PAIRS, LANES, LANES), lambda i, *_: (i, 0, 0, 0))
    lane = lambda: pl.BlockSpec((1, RET_WIDTH), lambda i, *_: (0, 0))
    grid_spec = pltpu.PrefetchScalarGridSpec(
        num_scalar_prefetch=2,
        grid=(b,),
        in_specs=[seq(), seq(), seq(), seq(), st(), st(), lane(), lane(), lane()],
        out_specs=seq(),
        scratch_shapes=[
            pltpu.VMEM((RET_HEADS, RET_CHUNK, RET_CHUNK), F32),
            pltpu.VMEM((RET_CHUNK, RET_WIDTH), F32),
            pltpu.VMEM((RET_CHUNK, RET_WIDTH), F32),
            pltpu.VMEM((RET_CHUNK, RET_WIDTH), F32),
            pltpu.VMEM((RET_CHUNK, RET_WIDTH), F32),
            pltpu.VMEM((N_PAIRS, LANES, LANES), F32),
            pltpu.VMEM((N_PAIRS, LANES, LANES), F32),
            pltpu.VMEM((nc, N_PAIRS, LANES, LANES), BF16),
        ],
    )
    return pl.pallas_call(
        _ret_kernel,
        out_shape=jax.ShapeDtypeStruct((b, n, RET_WIDTH), BF16),
        grid_spec=grid_spec,
        compiler_params=_cparams(("arbitrary",)),
        name="retention",
    )(a_f, a_b, q, k, v, gate, stf, stb, a_f_lane, a_b_lane, gain)


def _four_kernel(f_ref, bc_ref, bs_ref, cs_ref, o_ref, z_scr):
    n = f_ref.shape[1]

    @pl.when(pl.program_id(1) == 0)
    def _():
        f = f_ref[0]
        z_scr[0:n, :] = jnp.dot(f, bc_ref[...], preferred_element_type=F32).astype(BF16)
        z_scr[n:2 * n, :] = jnp.dot(f, bs_ref[...], preferred_element_type=F32).astype(BF16)

    o_ref[0] = jnp.dot(cs_ref[...], z_scr[...], preferred_element_type=F32).astype(BF16)


def _fourier(f, bc, bs, cs, tn):
    b, n, _ = f.shape
    return pl.pallas_call(
        _four_kernel,
        out_shape=jax.ShapeDtypeStruct((b, n, FNET_WIDTH), BF16),
        grid=(b, n // tn),
        in_specs=[
            pl.BlockSpec((1, n, FNET_WIDTH), lambda i, j: (i, 0, 0)),
            pl.BlockSpec((FNET_WIDTH, FNET_WIDTH), lambda i, j: (0, 0)),
            pl.BlockSpec((FNET_WIDTH, FNET_WIDTH), lambda i, j: (0, 0)),
            pl.BlockSpec((tn, 2 * n), lambda i, j: (j, 0)),
        ],
        out_specs=pl.BlockSpec((1, tn, FNET_WIDTH), lambda i, j: (i, j, 0)),
        scratch_shapes=[pltpu.VMEM((2 * n, FNET_WIDTH), BF16)],
        compiler_params=_cparams(("arbitrary", "arbitrary")),
        name="fourier",
    )(f, bc, bs, cs)


def _outp_kernel(ret_ref, four_ref, x_ref, g1_ref, sh2_ref, sc2_ref, gpost_ref, gpre_ref,
                 wo_ref, wq_ref, x1_ref, h2_ref, qp_ref):
    mix = jnp.dot(ret_ref[0], wo_ref[0:RET_WIDTH, :], preferred_element_type=F32)
    mix += jnp.dot(four_ref[0], wo_ref[RET_WIDTH:, :], preferred_element_type=F32)
    x1 = x_ref[0] + g1_ref[0] * _rms_rows(mix, gpost_ref[...])
    x1_ref[0] = x1
    h2 = _rms_rows(x1, gpre_ref[...]) * (1.0 + sc2_ref[0]) + sh2_ref[0]
    h2_ref[0] = h2
    qp_ref[0] = jnp.dot(h2.astype(BF16), wq_ref[...], preferred_element_type=F32).astype(BF16)


def _outproj(ret, four, x, g1, sh2, sc2, gpost, gpre, w_out, w_q, tm):
    b, n, _ = x.shape
    qw = w_q.shape[1]
    mod = lambda: pl.BlockSpec((1, 1, D_MODEL), lambda i, j: (i, 0, 0))
    vec = lambda: pl.BlockSpec((1, D_MODEL), lambda i, j: (0, 0))
    half = lambda: pl.BlockSpec((1, tm, 512), lambda i, j: (i, j, 0))
    full = lambda: pl.BlockSpec((1, tm, D_MODEL), lambda i, j: (i, j, 0))
    return pl.pallas_call(
        _outp_kernel,
        out_shape=(jax.ShapeDtypeStruct((b, n, D_MODEL), F32),
                   jax.ShapeDtypeStruct((b, n, D_MODEL), F32),
                   jax.ShapeDtypeStruct((b, n, qw), BF16)),
        grid=(b, n // tm),
        in_specs=[half(), half(), full(), mod(), mod(), mod(), vec(), vec(),
                  pl.BlockSpec(w_out.shape, lambda i, j: (0, 0)),
                  pl.BlockSpec(w_q.shape, lambda i, j: (0, 0))],
        out_specs=(full(), full(), pl.BlockSpec((1, tm, qw), lambda i, j: (i, j, 0))),
        compiler_params=_cparams(("arbitrary", "arbitrary")),
        name="outproj",
    )(ret, four, x, g1, sh2, sc2, gpost, gpre, w_out, w_q)


def _top_rows(s, kk, rid=None):
    rows, tq = s.shape
    if rid is None:
        rid = lax.broadcasted_iota(I32, (rows, tq), 0).astype(F32)
    slot = lax.broadcasted_iota(I32, (kk, tq), 0)
    vals = jnp.zeros((kk, tq), F32)
    idxs = jnp.zeros((kk, tq), F32)
    for it in range(kk):
        m = jnp.max(s, axis=0, keepdims=True)
        am = jnp.min(jnp.where(s == m, rid, float(2 ** 20)), axis=0, keepdims=True)
        vals = jnp.where(slot == it, m, vals)
        idxs = jnp.where(slot == it, am, idxs)
        s = jnp.where(rid == am, -jnp.inf, s)
    return vals, idxs


def _pick_rows(table, sel, kk):
    out = jnp.zeros_like(table)
    for i in range(kk):
        out = jnp.where(sel == float(i), table[i:i + 1, :], out)
    return out


def _pair_candidates(v1, v2):
    kk = PEER_TOPK
    tq = v1.shape[1]
    sub = lax.broadcasted_iota(I32, (SUBLANES, tq), 0)
    subf = sub.astype(F32)
    upper = sub >= 4
    vals = [v1[0:1] + v2[0:8], v1[0:1] + v2[8:16], v1[1:2] + v2[0:8], v1[2:3] + v2[0:8], v1[3:4] + v2[0:8]]
    ids = [subf, subf + 8.0, subf + float(kk), subf + float(2 * kk), subf + float(3 * kk)]
    for j in range(3):
        vals.append(jnp.where(upper, v1[0:8] + v2[j:j + 1], -jnp.inf))
        ids.append(subf * float(kk) + float(j))
    vals.append(v1[8:16] + v2[0:1])
    ids.append((subf + 8.0) * float(kk))
    return jnp.concatenate(vals, axis=0), jnp.concatenate(ids, axis=0)


def _topk_kernel(qp_ref, k1_ref, k2_ref, off_ref, wts_ref):
    kk = PEER_TOPK
    half = PEER_D_KEY // 2
    nt = (((1,), (1,)), ((), ()))
    k1 = k1_ref[...]
    k2 = k2_ref[...]
    ids_all, w_all = [], []
    for hh in range(PEER_HEADS):
        q1 = qp_ref[:, hh * PEER_D_KEY: hh * PEER_D_KEY + half]
        q2 = qp_ref[:, hh * PEER_D_KEY + half: (hh + 1) * PEER_D_KEY]
        s1 = lax.dot_general(k1, q1, nt, preferred_element_type=F32)
        s2 = lax.dot_general(k2, q2, nt, preferred_element_type=F32)
        v1, i1 = _top_rows(s1, kk)
        v2, i2 = _top_rows(s2, kk)
        cand, flat = _pair_candidates(v1, v2)
        tv, ti = _top_rows(cand, kk, flat)
        hi = jnp.floor(ti * (1.0 / kk))
        lo = ti - hi * kk
        ids = _pick_rows(i1, hi, kk) * float(PEER_N_KEYS) + _pick_rows(i2, lo, kk)
        p = jnp.exp(tv - tv[0:1, :])
        w = p / jnp.sum(p, axis=0, keepdims=True)
        ids_all.append(ids)
        w_all.append(w)
    ids_t = jnp.concatenate(ids_all, axis=0).T.astype(I32)
    slot = lax.broadcasted_iota(I32, ids_t.shape, 1)
    off_ref[...] = ids_t * TABLE_ROWS_PER_EXPERT + jnp.where((slot % SUBLANES) < 4, TABLE_FRONT_PAD, 0)
    wts_ref[...] = jnp.concatenate(w_all, axis=0).T


def _topk(qp, k1, k2):
    t, qw = qp.shape
    tq = LANES
    out = lambda: pl.BlockSpec((tq, PEER_SLOTS), lambda i: (i, 0))
    key = lambda: pl.BlockSpec((PEER_N_KEYS, PEER_D_KEY // 2), lambda i: (0, 0))
    return pl.pallas_call(
        _topk_kernel,
        out_shape=(jax.ShapeDtypeStruct((t, PEER_SLOTS), I32),
                   jax.ShapeDtypeStruct((t, PEER_SLOTS), F32)),
        grid=(t // tq,),
        in_specs=[pl.BlockSpec((tq, qw), lambda i: (i, 0)), key(), key()],
        out_specs=(out(), out()),
        compiler_params=_cparams(("arbitrary",)),
        name="peer_topk",
    )(qp, k1, k2)


PEER_SLOTS = PEER_HEADS * PEER_TOPK
PEER_TOKENS_PER_STEP = 16
TABLE_ROWS_PER_EXPERT = 4
TABLE_FRONT_PAD = 4
TABLE_BACK_PAD = 8
HI_MASK = np.uint32(0xFFFF0000)


def _pack_table(w):
    e = w.shape[0]
    bits = lax.bitcast_convert_type(w.astype(BF16), jnp.uint16).astype(U32)
    bits = bits.reshape(e, 2, TABLE_ROWS_PER_EXPERT, LANES)
    words = (bits[:, 0] | (bits[:, 1] << 16)).reshape(e * TABLE_ROWS_PER_EXPERT, LANES)
    return jnp.pad(words, ((TABLE_FRONT_PAD, TABLE_BACK_PAD), (0, 0)))


def _expert_pair(tbl_ref, off_s, j, g, i, low4):
    wa = tbl_ref[pl.ds(off_s[j, g * SUBLANES + i], SUBLANES), :]
    wb = tbl_ref[pl.ds(off_s[j, g * SUBLANES + i + 4], SUBLANES), :]
    return jnp.where(low4, wa, wb)


def _gelu_exact(x):
    return 0.5 * x * (1.0 + lax.erf(x * (2.0 ** -0.5)))


def _peer_u_kernel(off_s, h_ref, wts_ref, tbl_ref, coef_ref):
    nt = (((1,), (1,)), ((), ()))
    ones = jnp.ones((SUBLANES, LANES), BF16)
    sub = lax.broadcasted_iota(I32, (SUBLANES, LANES), 0)
    low4 = sub < 4
    m2 = (sub % 4) < 2
    m1 = (sub % 2) == 0

    r4 = sub % TABLE_ROWS_PER_EXPERT

    def token_act(j):
        def chunk(c):
            return jnp.broadcast_to(h_ref[j:j + 1, c * LANES:(c + 1) * LANES], (SUBLANES, LANES))

        def by_row(c0):
            return jnp.where(r4 == 0, chunk(c0), jnp.where(r4 == 1, chunk(c0 + 1),
                             jnp.where(r4 == 2, chunk(c0 + 2), chunk(c0 + 3))))

        hlo = by_row(0)
        hhi = by_row(TABLE_ROWS_PER_EXPERT)

        def prod(g, i):
            w = _expert_pair(tbl_ref, off_s, j, g, i, low4)
            lo = lax.bitcast_convert_type(w << 16, F32)
            hi = lax.bitcast_convert_type(w & HI_MASK, F32)
            return lo * hlo + hi * hhi

        groups = []
        for g in range(PEER_SLOTS // SUBLANES):
            q1, q2, q3, q4 = prod(g, 0), prod(g, 2), prod(g, 1), prod(g, 3)
            t1 = jnp.where(m2, q1 + pltpu.roll(q1, 6, 0), q2 + pltpu.roll(q2, 2, 0))
            t2 = jnp.where(m2, q3 + pltpu.roll(q3, 6, 0), q4 + pltpu.roll(q4, 2, 0))
            groups.append(jnp.where(m1, t1 + pltpu.roll(t1, 7, 0), t2 + pltpu.roll(t2, 1, 0)))
        part = jnp.concatenate(groups, axis=0)
        hi = part.astype(BF16)
        lo = (part - hi.astype(F32)).astype(BF16)
        act = (lax.dot_general(ones, hi, nt, preferred_element_type=F32)
               + lax.dot_general(ones, lo, nt, preferred_element_type=F32))
        return act[0:1, :]

    acts = jnp.concatenate([token_act(j) for j in range(PEER_TOKENS_PER_STEP)], axis=0)
    coef_ref[...] = wts_ref[...] * _gelu_exact(acts)


def _peer_u(off, h, wts, tbl):
    t = h.shape[0]
    tb = PEER_TOKENS_PER_STEP
    return pl.pallas_call(
        _peer_u_kernel,
        out_shape=jax.ShapeDtypeStruct((t, PEER_SLOTS), F32),
        grid=(t // tb,),
        in_specs=[pl.BlockSpec((tb, PEER_SLOTS), lambda i: (i, 0), memory_space=pltpu.SMEM),
                  pl.BlockSpec((tb, D_MODEL), lambda i: (i, 0)),
                  pl.BlockSpec((tb, PEER_SLOTS), lambda i: (i, 0)),
                  pl.BlockSpec(memory_space=pltpu.VMEM)],
        out_specs=pl.BlockSpec((tb, PEER_SLOTS), lambda i: (i, 0)),
        compiler_params=_cparams(("arbitrary",)),
        name="peer_u",
    )(off, h, wts, tbl)


def _peer_v_tables():
    col = np.arange(PEER_SLOTS * SUBLANES)
    q, r = col // 16, col % 16
    slot = SUBLANES * (q // 4) + (q % 4) + np.where(r < 8, 0, 4)
    expand = (slot[None, :] == np.arange(PEER_SLOTS)[:, None]).astype(np.float32)
    out_row = (r // 2) % 4 + 4 * (r % 2)
    rowsel = (out_row[None, :] == np.arange(SUBLANES)[:, None]).astype(np.float32)
    return jnp.asarray(expand, BF16), jnp.asarray(rowsel, F32)


def _peer_v_kernel(off_s, coef_ref, x1_ref, g2_ref, gain_ref, expand_ref, rowsel_ref, tbl_ref, o_ref, out_scr):
    sub = lax.broadcasted_iota(I32, (SUBLANES, LANES), 0)
    low4 = sub < 4
    cexp = jnp.dot(coef_ref[...].astype(BF16), expand_ref[...], preferred_element_type=F32)
    rowsel = rowsel_ref[...]

    for j in range(PEER_TOKENS_PER_STEP):
        tiles = []
        for q in range(PEER_SLOTS // 2):
            w = _expert_pair(tbl_ref, off_s, j, q // 4, q % 4, low4)
            tiles.append(pltpu.bitcast(w, BF16))
        wmat = jnp.concatenate(tiles, axis=0)
        c = (cexp[j:j + 1, :] * rowsel).astype(BF16)
        out = jnp.dot(c, wmat, preferred_element_type=F32)
        for s in range(SUBLANES):
            out_scr[j:j + 1, s * LANES:(s + 1) * LANES] = out[s:s + 1, :]

    y = _rms_rows(out_scr[...], gain_ref[...])
    o_ref[...] = x1_ref[...] + g2_ref[0] * y


def _peer_v(off, coef, x1, g2, gain, tbl, tokens_per_sample):
    t = x1.shape[0]
    tb = PEER_TOKENS_PER_STEP
    steps_per_sample = tokens_per_sample // tb
    expand, rowsel = _peer_v_tables()
    tok = lambda: pl.BlockSpec((tb, D_MODEL), lambda i: (i, 0))
    return pl.pallas_call(
        _peer_v_kernel,
        out_shape=jax.ShapeDtypeStruct((t, D_MODEL), F32),
        grid=(t // tb,),
        in_specs=[pl.BlockSpec((tb, PEER_SLOTS), lambda i: (i, 0), memory_space=pltpu.SMEM),
                  pl.BlockSpec((tb, PEER_SLOTS), lambda i: (i, 0)),
                  tok(),
                  pl.BlockSpec((1, 1, D_MODEL), lambda i: (i // steps_per_sample, 0, 0)),
                  pl.BlockSpec((1, D_MODEL), lambda i: (0, 0)),
                  pl.BlockSpec(expand.shape, lambda i: (0, 0)),
                  pl.BlockSpec(rowsel.shape, lambda i: (0, 0)),
                  pl.BlockSpec(memory_space=pltpu.VMEM)],
        out_specs=tok(),
        scratch_shapes=[pltpu.VMEM((tb, D_MODEL), F32)],
        compiler_params=_cparams(("arbitrary",)),
        name="peer_v",
    )(off, coef, x1, g2, gain, expand, rowsel, tbl)


@functools.lru_cache(maxsize=None)
def _rope_tables_np(n):
    rows = n // GRID_W
    row_ids = np.repeat(np.arange(rows, dtype=np.float64), GRID_W)
    col_ids = np.tile(np.arange(GRID_W, dtype=np.float64), rows)
    freqs = ROPE_BASE ** (-np.arange(ROPE_PAIRS, dtype=np.float64) / ROPE_PAIRS)
    ang = np.concatenate([row_ids[:, None] * freqs[None, :], col_ids[:, None] * freqs[None, :]], axis=-1)
    cos, sin = np.cos(ang), np.sin(ang)
    cos_h = np.concatenate([cos, cos], axis=-1)
    sin_h = np.concatenate([-sin, sin], axis=-1)
    return (np.tile(cos_h, (1, RET_HEADS)).astype(np.float32),
            np.tile(sin_h, (1, RET_HEADS)).astype(np.float32))


@functools.lru_cache(maxsize=None)
def _dft_tables_np(n):
    scale = (n * FNET_GROUP_DIM) ** -0.5
    jn = np.arange(n, dtype=np.int64)
    ang_n = ((jn[:, None] * jn[None, :]) % n).astype(np.float64) * (2.0 * np.pi / n)
    cs = np.concatenate([np.cos(ang_n), -np.sin(ang_n)], axis=1).astype(np.float32)
    jc = np.arange(FNET_GROUP_DIM, dtype=np.int64)
    ang_c = ((jc[:, None] * jc[None, :]) % FNET_GROUP_DIM).astype(np.float64) * (2.0 * np.pi / FNET_GROUP_DIM)
    eye = np.eye(FNET_GROUPS)
    bc = np.kron(eye, np.cos(ang_c) * scale).astype(np.float32)
    bs = np.kron(eye, np.sin(ang_c) * scale).astype(np.float32)
    return bc, bs, cs


def kernel(x, c, ctx, c_ctx, w_ada, b_ada, norm_pre_mix, norm_post_mix, w_in, ret_decay_fwd, ret_decay_bwd, ret_norm_gain, w_out, norm_pre_ffn, norm_post_ffn, peer_w_query, peer_sub_keys_1, peer_sub_keys_2, peer_u, peer_v):
    b, n, d = x.shape
    depth = w_ada.shape[0]
    assert depth == 1 and d == D_MODEL and n % RET_CHUNK == 0 and n % GRID_W == 0
    t = b * n
    tm = min(256, n)
    row = lambda a: a.reshape(1, -1)

    cos, sin = [jnp.asarray(a) for a in _rope_tables_np(n)]
    bc, bs, cs = [jnp.asarray(a).astype(BF16) for a in _dft_tables_np(n)]

    pad = (-(b + 1)) % SUBLANES
    c_all = jnp.concatenate([c, c_ctx[None, :], jnp.zeros((pad, d), F32)], axis=0)
    mod = _ada(c_all, w_ada[0], row(b_ada[0]))
    sh1, sc1, g1, sh2, sc2, g2 = [mod[:b, i * d:(i + 1) * d].reshape(b, 1, d) for i in range(N_MOD)]
    csh1 = mod[b:b + 1, 0:d]
    csc1 = mod[b:b + 1, d:2 * d]

    a_f, a_b = ret_decay_fwd[0], ret_decay_bwd[0]
    a_f_lane = row(jnp.repeat(a_f, RET_DK))
    a_b_lane = row(jnp.repeat(a_b, RET_DK))

    w_in_b = w_in[0].astype(BF16)
    stf, stb = _ctx_states(ctx, row(norm_pre_mix[0]), csh1, csc1,
                           w_in_b[:, QK_WIDTH:QK_WIDTH + 2 * QK_WIDTH], a_f_lane, a_b_lane)

    q, k, v, gate, f = _inproj(x, sh1, sc1, row(norm_pre_mix[0]), w_in_b, cos, sin, tm)
    ret = _retention(q, k, v, gate, stf, stb, a_f, a_b, a_f_lane, a_b_lane, row(ret_norm_gain[0]))
    four = _fourier(f, bc, bs, cs, tm)

    x1, h2, qp = _outproj(ret, four, x, g1, sh2, sc2, row(norm_post_mix[0]), row(norm_pre_ffn[0]),
                          w_out[0].astype(BF16), peer_w_query[0].astype(BF16), tm)

    off, wts = _topk(qp.reshape(t, -1), peer_sub_keys_1[0].astype(BF16), peer_sub_keys_2[0].astype(BF16))

    coef = _peer_u(off, h2.reshape(t, d), wts, _pack_table(peer_u[0]))
    out = _peer_v(off, coef, x1.reshape(t, d), g2, row(norm_post_ffn[0]), _pack_table(peer_v[0]), n)
    return out.reshape(b, n, d)
```

```python
import functools

import numpy as np
import jax
import jax.numpy as jnp
from jax import lax
from jax.experimental import pallas as pl
from jax.experimental.pallas import tpu as pltpu

F32 = jnp.float32
BF16 = jnp.bfloat16
I32 = jnp.int32
U32 = jnp.uint32

D_MODEL = 1024
GRID_W = 64
RET_HEADS = 8
RET_DK = 64
RET_WIDTH = 512
QK_WIDTH = 512
RET_CHUNK = 128
ROPE_PAIRS = RET_DK // 4
ROPE_BASE = 10000.0
FNET_GROUPS = 4
FNET_GROUP_DIM = 128
FNET_WIDTH = 512
IN_WIDTH = 2560
PEER_HEADS = 8
PEER_D_KEY = 256
PEER_N_KEYS = 128
PEER_TOPK = 16
N_MOD = 6
EPS = 1e-6

LANES = 128
SUBLANES = 8
VMEM_LIMIT = 56 * 1024 * 1024
N_PAIRS = RET_HEADS // 2


def _cparams(sem):
    return pltpu.CompilerParams(dimension_semantics=sem, vmem_limit_bytes=VMEM_LIMIT)


def _rms_rows(x, gain):
    ms = jnp.mean(x * x, axis=-1, keepdims=True)
    return x * lax.rsqrt(ms + EPS) * gain


def _silu(x):
    return x * jax.nn.sigmoid(x)


def _log_gamma(a):
    return jnp.log1p(-jnp.exp(a))


def _ada_kernel(c_ref, w_ref, b_ref, o_ref):
    s = _silu(c_ref[...])
    o_ref[...] = jnp.dot(s, w_ref[...], preferred_element_type=F32) + b_ref[...]


def _ada(c_all, w_ada, b_ada):
    rows = c_all.shape[0]
    width = w_ada.shape[1]
    tn = 1536
    return pl.pallas_call(
        _ada_kernel,
        out_shape=jax.ShapeDtypeStruct((rows, width), F32),
        grid=(width // tn,),
        in_specs=[
            pl.BlockSpec((rows, D_MODEL), lambda j: (0, 0)),
            pl.BlockSpec((D_MODEL, tn), lambda j: (0, j)),
            pl.BlockSpec((1, tn), lambda j: (0, j)),
        ],
        out_specs=pl.BlockSpec((rows, tn), lambda j: (0, j)),
        compiler_params=_cparams(("arbitrary",)),
        name="ada",
    )(c_all, w_ada, b_ada)


def _ctx_kernel(ctx_ref, g_ref, sh_ref, sc_ref, w_ref, af_ref, ab_ref, stf_ref, stb_ref):
    x = ctx_ref[0]
    n = x.shape[0]
    h = (_rms_rows(x, g_ref[...]) * (1.0 + sc_ref[...]) + sh_ref[...]).astype(BF16)
    kv = jnp.dot(h, w_ref[...], preferred_element_type=F32)
    k = kv[:, :QK_WIDTH] * (RET_DK ** -0.5)
    v = kv[:, QK_WIDTH:].astype(BF16)
    pos = lax.broadcasted_iota(I32, (n, QK_WIDTH), 0).astype(F32)
    lgf = _log_gamma(af_ref[...])
    lgb = _log_gamma(ab_ref[...])
    kf = (k * jnp.exp(lgf * (n - 1.0 - pos))).astype(BF16)
    kb = (k * jnp.exp(lgb * pos)).astype(BF16)
    tn = (((0,), (0,)), ((), ()))
    gf = lax.dot_general(kf, v, tn, preferred_element_type=F32)
    gb = lax.dot_general(kb, v, tn, preferred_element_type=F32)
    ri = lax.broadcasted_iota(I32, (LANES, LANES), 0) // RET_DK
    ci = lax.broadcasted_iota(I32, (LANES, LANES), 1) // RET_DK
    bd = ri == ci
    for p in range(N_PAIRS):
        sl = slice(p * LANES, (p + 1) * LANES)
        stf_ref[0, p] = jnp.where(bd, gf[sl, sl], 0.0)
        stb_ref[0, p] = jnp.where(bd, gb[sl, sl], 0.0)


def _ctx_states(ctx, gain, csh, csc, w_kv, a_f, a_b):
    b, n, _ = ctx.shape
    vec = lambda: pl.BlockSpec((1, D_MODEL), lambda i: (0, 0))
    lane = lambda: pl.BlockSpec((1, QK_WIDTH), lambda i: (0, 0))
    st = jax.ShapeDtypeStruct((b, N_PAIRS, LANES, LANES), F32)
    st_spec = lambda: pl.BlockSpec((1, N_PAIRS, LANES, LANES), lambda i: (i, 0, 0, 0))
    return pl.pallas_call(
        _ctx_kernel,
        out_shape=(st, st),
        grid=(b,),
        in_specs=[
            pl.BlockSpec((1, n, D_MODEL), lambda i: (i, 0, 0)),
            vec(), vec(), vec(),
            pl.BlockSpec((D_MODEL, 2 * QK_WIDTH), lambda i: (0, 0)),
            lane(), lane(),
        ],
        out_specs=(st_spec(), st_spec()),
        compiler_params=_cparams(("arbitrary",)),
        name="ctx_states",
    )(ctx, gain, csh, csc, w_kv, a_f, a_b)


def _inproj_kernel(x_ref, sh_ref, sc_ref, g_ref, w_ref, cos_ref, sin_ref,
                   q_ref, k_ref, v_ref, gate_ref, f_ref):
    x = x_ref[0]
    tm = x.shape[0]
    h = (_rms_rows(x, g_ref[...]) * (1.0 + sc_ref[0]) + sh_ref[0]).astype(BF16)

    def proj(lo, hi):
        return jnp.dot(h, w_ref[:, lo:hi], preferred_element_type=F32)

    cos = cos_ref[...]
    sin = sin_ref[...]
    lane = lax.broadcasted_iota(I32, (tm, QK_WIDTH), 1)
    first = (lane % RET_DK) < (RET_DK // 2)

    def rope(t):
        swapped = jnp.where(first,
                            pltpu.roll(t, QK_WIDTH - RET_DK // 2, 1),
                            pltpu.roll(t, RET_DK // 2, 1))
        return t * cos + swapped * sin

    q_ref[0] = rope(proj(0, 512)).astype(BF16)
    k_ref[0] = rope(proj(512, 1024) * (RET_DK ** -0.5)).astype(BF16)
    v_ref[0] = proj(1024, 1536).astype(BF16)
    gate_ref[0] = proj(1536, 2048)
    f_ref[0] = proj(2048, 2560).astype(BF16)


def _inproj(x, sh, sc, gain, w_in, cos, sin, tm):
    b, n, _ = x.shape
    mod = lambda: pl.BlockSpec((1, 1, D_MODEL), lambda i, j: (i, 0, 0))
    tab = lambda: pl.BlockSpec((tm, QK_WIDTH), lambda i, j: (j, 0))
    out = lambda: pl.BlockSpec((1, tm, 512), lambda i, j: (i, j, 0))
    sd = lambda dt: jax.ShapeDtypeStruct((b, n, 512), dt)
    return pl.pallas_call(
        _inproj_kernel,
        out_shape=(sd(BF16), sd(BF16), sd(BF16), sd(F32), sd(BF16)),
        grid=(b, n // tm),
        in_specs=[
            pl.BlockSpec((1, tm, D_MODEL), lambda i, j: (i, j, 0)),
            mod(), mod(),
            pl.BlockSpec((1, D_MODEL), lambda i, j: (0, 0)),
            pl.BlockSpec((D_MODEL, IN_WIDTH), lambda i, j: (0, 0)),
            tab(), tab(),
        ],
        out_specs=(out(), out(), out(), out(), out()),
        compiler_params=_cparams(("arbitrary", "arbitrary")),
        name="inproj",
    )(x, sh, sc, gain, w_in, cos, sin)


def _ret_kernel(af_s, ab_s, q_ref, k_ref, v_ref, gate_ref, stf_ref, stb_ref, af_ref, ab_ref,
                gain_ref, o_ref, mask_scr, qdf_scr, qdb_scr, kdf_scr, kdb_scr, sf_scr, sb_scr,
                sb_store):
    c = RET_CHUNK
    n = q_ref.shape[1]
    nc = n // c
    lgf = _log_gamma(af_ref[...])
    lgb = _log_gamma(ab_ref[...])
    cdf = jnp.exp(lgf * float(c))
    cdb = jnp.exp(lgb * float(c))

    @pl.when(pl.program_id(0) == 0)
    def _():
        r = lax.broadcasted_iota(I32, (c, RET_WIDTH), 0).astype(F32)
        qdf_scr[...] = jnp.exp(lgf * (r + 1.0))
        kdf_scr[...] = jnp.exp(lgf * (c - 1.0 - r))
        qdb_scr[...] = jnp.exp(lgb * (c - r))
        kdb_scr[...] = jnp.exp(lgb * r)
        ri = lax.broadcasted_iota(I32, (c, c), 0)
        ci = lax.broadcasted_iota(I32, (c, c), 1)
        d = (ri - ci).astype(F32)
        for hh in range(RET_HEADS):
            gf = _log_gamma(jnp.full((c, c), af_s[hh], F32))
            gb = _log_gamma(jnp.full((c, c), ab_s[hh], F32))
            mask_scr[hh] = jnp.where(d >= 0.0, jnp.exp(gf * jnp.maximum(d, 0.0)),
                                     jnp.exp(gb * jnp.maximum(-d, 0.0)))

    ri = lax.broadcasted_iota(I32, (LANES, LANES), 0) // RET_DK
    ci = lax.broadcasted_iota(I32, (LANES, LANES), 1) // RET_DK
    bd = ri == ci
    bd_ones = jnp.where(bd, 1.0, 0.0).astype(BF16)
    lane_head = lax.broadcasted_iota(I32, (c, LANES), 1) // RET_DK
    tn = (((0,), (0,)), ((), ()))
    nt = (((1,), (1,)), ((), ()))

    for p in range(N_PAIRS):
        sf_scr[p] = stf_ref[0, p]
        sb_scr[p] = stb_ref[0, p]

    def bwd_states(i, carry):
        cc = nc - 1 - i
        rows = pl.ds(pl.multiple_of(cc * c, c), c)
        for p in range(N_PAIRS):
            sl = slice(p * LANES, (p + 1) * LANES)
            sb = sb_scr[p]
            sb_store[cc, p] = sb.astype(BF16)
            kw = (k_ref[0, rows, sl].astype(F32) * kdb_scr[:, sl]).astype(BF16)
            g = lax.dot_general(kw, v_ref[0, rows, sl], tn, preferred_element_type=F32)
            sb_scr[p] = sb * cdb[:, sl] + jnp.where(bd, g, 0.0)
        return carry

    lax.fori_loop(0, nc, bwd_states, 0)

    def fwd(cc, carry):
        rows = pl.ds(pl.multiple_of(cc * c, c), c)
        for p in range(N_PAIRS):
            sl = slice(p * LANES, (p + 1) * LANES)
            qp = q_ref[0, rows, sl]
            kp = k_ref[0, rows, sl]
            vp = v_ref[0, rows, sl]
            qf = qp.astype(F32)
            sf = sf_scr[p]
            acc = jnp.dot((qf * qdf_scr[:, sl]).astype(BF16), sf.astype(BF16),
                          preferred_element_type=F32)
            acc += jnp.dot((qf * qdb_scr[:, sl]).astype(BF16), sb_store[cc, p],
                           preferred_element_type=F32)
            for j in range(2):
                own = lane_head == j
                qm = jnp.where(own, qp, jnp.zeros_like(qp))
                s = lax.dot_general(qm, kp, nt, preferred_element_type=F32)
                pm = (s * mask_scr[2 * p + j]).astype(BF16)
                vm = jnp.where(own, vp, jnp.zeros_like(vp))
                acc += jnp.dot(pm, vm, preferred_element_type=F32)
            sq = acc * acc
            sq_hi = sq.astype(BF16)
            sq_lo = (sq - sq_hi.astype(F32)).astype(BF16)
            ms = (jnp.dot(sq_hi, bd_ones, preferred_element_type=F32)
                  + jnp.dot(sq_lo, bd_ones, preferred_element_type=F32)) * (1.0 / RET_DK)
            y = acc * lax.rsqrt(ms + EPS) * gain_ref[:, sl] * _silu(gate_ref[0, rows, sl])
            o_ref[0, rows, sl] = y.astype(BF16)
            kw = (kp.astype(F32) * kdf_scr[:, sl]).astype(BF16)
            g = lax.dot_general(kw, vp, tn, preferred_element_type=F32)
            sf_scr[p] = sf * cdf[:, sl] + jnp.where(bd, g, 0.0)
        return carry

    lax.fori_loop(0, nc, fwd, 0)


def _retention(q, k, v, gate, stf, stb, a_f, a_b, a_f_lane, a_b_lane, gain):
    b, n, _ = q.shape
    nc = n // RET_CHUNK
    seq = lambda: pl.BlockSpec((1, n, RET_WIDTH), lambda i, *_: (i, 0, 0))
    st = lambda: pl.BlockSpec((1, N_PAIRS, LANES, LANES), lambda i, *_: (i, 0, 0, 0))
    lane = lambda: pl.BlockSpec((1, RET_WIDTH), lambda i, *_: (0, 0))
    grid_spec = pltpu.PrefetchScalarGridSpec(
        num_scalar_prefetch=2,
        grid=(b,),
        in_specs=[seq(), seq(), seq(), seq(), st(), st(), lane(), lane(), lane()],
        out_specs=seq(),
        scratch_shapes=[
            pltpu.VMEM((RET_HEADS, RET_CHUNK, RET_CHUNK), F32),
            pltpu.VMEM((RET_CHUNK, RET_WIDTH), F32),
            pltpu.VMEM((RET_CHUNK, RET_WIDTH), F32),
            pltpu.VMEM((RET_CHUNK, RET_WIDTH), F32),
            pltpu.VMEM((RET_CHUNK, RET_WIDTH), F32),
            pltpu.VMEM((N_PAIRS, LANES, LANES), F32),
            pltpu.VMEM((N_PAIRS, LANES, LANES), F32),
            pltpu.VMEM((nc, N_PAIRS, LANES, LANES), BF16),
        ],
    )
    return pl.pallas_call(
        _ret_kernel,
        out_shape=jax.ShapeDtypeStruct((b, n, RET_WIDTH), BF16),
        grid_spec=grid_spec,
        compiler_params=_cparams(("arbitrary",)),
        name="retention",
    )(a_f, a_b, q, k, v, gate, stf, stb, a_f_lane, a_b_lane, gain)


def _four_kernel(f_ref, bc_ref, bs_ref, cs_ref, o_ref, z_scr):
    n = f_ref.shape[1]

    @pl.when(pl.program_id(1) == 0)
    def _():
        f = f_ref[0]
        z_scr[0:n, :] = jnp.dot(f, bc_ref[...], preferred_element_type=F32).astype(BF16)
        z_scr[n:2 * n, :] = jnp.dot(f, bs_ref[...], preferred_element_type=F32).astype(BF16)

    o_ref[0] = jnp.dot(cs_ref[...], z_scr[...], preferred_element_type=F32).astype(BF16)


def _fourier(f, bc, bs, cs, tn):
    b, n, _ = f.shape
    return pl.pallas_call(
        _four_kernel,
        out_shape=jax.ShapeDtypeStruct((b, n, FNET_WIDTH), BF16),
        grid=(b, n // tn),
        in_specs=[
            pl.BlockSpec((1, n, FNET_WIDTH), lambda i, j: (i, 0, 0)),
            pl.BlockSpec((FNET_WIDTH, FNET_WIDTH), lambda i, j: (0, 0)),
            pl.BlockSpec((FNET_WIDTH, FNET_WIDTH), lambda i, j: (0, 0)),
            pl.BlockSpec((tn, 2 * n), lambda i, j: (j, 0)),
        ],
        out_specs=pl.BlockSpec((1, tn, FNET_WIDTH), lambda i, j: (i, j, 0)),
        scratch_shapes=[pltpu.VMEM((2 * n, FNET_WIDTH), BF16)],
        compiler_params=_cparams(("arbitrary", "arbitrary")),
        name="fourier",
    )(f, bc, bs, cs)


def _outp_kernel(ret_ref, four_ref, x_ref, g1_ref, sh2_ref, sc2_ref, gpost_ref, gpre_ref,
                 wo_ref, wq_ref, x1_ref, h2_ref, qp_ref):
    mix = jnp.dot(ret_ref[0], wo_ref[0:RET_WIDTH, :], preferred_element_type=F32)
    mix += jnp.dot(four_ref[0], wo_ref[RET_WIDTH:, :], preferred_element_type=F32)
    x1 = x_ref[0] + g1_ref[0] * _rms_rows(mix, gpost_ref[...])
    x1_ref[0] = x1
    h2 = _rms_rows(x1, gpre_ref[...]) * (1.0 + sc2_ref[0]) + sh2_ref[0]
    h2_ref[0] = h2
    qp_ref[0] = jnp.dot(h2.astype(BF16), wq_ref[...], preferred_element_type=F32).astype(BF16)


def _outproj(ret, four, x, g1, sh2, sc2, gpost, gpre, w_out, w_q, tm):
    b, n, _ = x.shape
    qw = w_q.shape[1]
    mod = lambda: pl.BlockSpec((1, 1, D_MODEL), lambda i, j: (i, 0, 0))
    vec = lambda: pl.BlockSpec((1, D_MODEL), lambda i, j: (0, 0))
    half = lambda: pl.BlockSpec((1, tm, 512), lambda i, j: (i, j, 0))
    full = lambda: pl.BlockSpec((1, tm, D_MODEL), lambda i, j: (i, j, 0))
    return pl.pallas_call(
        _outp_kernel,
        out_shape=(jax.ShapeDtypeStruct((b, n, D_MODEL), F32),
                   jax.ShapeDtypeStruct((b, n, D_MODEL), F32),
                   jax.ShapeDtypeStruct((b, n, qw), BF16)),
        grid=(b, n // tm),
        in_specs=[half(), half(), full(), mod(), mod(), mod(), vec(), vec(),
                  pl.BlockSpec(w_out.shape, lambda i, j: (0, 0)),
                  pl.BlockSpec(w_q.shape, lambda i, j: (0, 0))],
        out_specs=(full(), full(), pl.BlockSpec((1, tm, qw), lambda i, j: (i, j, 0))),
        compiler_params=_cparams(("arbitrary", "arbitrary")),
        name="outproj",
    )(ret, four, x, g1, sh2, sc2, gpost, gpre, w_out, w_q)


def _top_rows(s, kk, rid=None):
    rows, tq = s.shape
    if rid is None:
        rid = lax.broadcasted_iota(I32, (rows, tq), 0).astype(F32)
    slot = lax.broadcasted_iota(I32, (kk, tq), 0)
    vals = jnp.zeros((kk, tq), F32)
    idxs = jnp.zeros((kk, tq), F32)
    for it in range(kk):
        m = jnp.max(s, axis=0, keepdims=True)
        am = jnp.min(jnp.where(s == m, rid, float(2 ** 20)), axis=0, keepdims=True)
        vals = jnp.where(slot == it, m, vals)
        idxs = jnp.where(slot == it, am, idxs)
        s = jnp.where(rid == am, -jnp.inf, s)
    return vals, idxs


def _pick_rows(table, sel, kk):
    out = jnp.zeros_like(table)
    for i in range(kk):
        out = jnp.where(sel == float(i), table[i:i + 1, :], out)
    return out


def _pair_candidates(v1, v2):
    kk = PEER_TOPK
    tq = v1.shape[1]
    sub = lax.broadcasted_iota(I32, (SUBLANES, tq), 0)
    subf = sub.astype(F32)
    upper = sub >= 4
    vals = [v1[0:1] + v2[0:8], v1[0:1] + v2[8:16], v1[1:2] + v2[0:8], v1[2:3] + v2[0:8], v1[3:4] + v2[0:8]]
    ids = [subf, subf + 8.0, subf + float(kk), subf + float(2 * kk), subf + float(3 * kk)]
    for j in range(3):
        vals.append(jnp.where(upper, v1[0:8] + v2[j:j + 1], -jnp.inf))
        ids.append(subf * float(kk) + float(j))
    vals.append(v1[8:16] + v2[0:1])
    ids.append((subf + 8.0) * float(kk))
    return jnp.concatenate(vals, axis=0), jnp.concatenate(ids, axis=0)


def _topk_kernel(qp_ref, k1_ref, k2_ref, off_ref, wts_ref):
    kk = PEER_TOPK
    half = PEER_D_KEY // 2
    nt = (((1,), (1,)), ((), ()))
    k1 = k1_ref[...]
    k2 = k2_ref[...]
    ids_all, w_all = [], []
    for hh in range(PEER_HEADS):
        q1 = qp_ref[:, hh * PEER_D_KEY: hh * PEER_D_KEY + half]
        q2 = qp_ref[:, hh * PEER_D_KEY + half: (hh + 1) * PEER_D_KEY]
        s1 = lax.dot_general(k1, q1, nt, preferred_element_type=F32)
        s2 = lax.dot_general(k2, q2, nt, preferred_element_type=F32)
        v1, i1 = _top_rows(s1, kk)
        v2, i2 = _top_rows(s2, kk)
        cand, flat = _pair_candidates(v1, v2)
        tv, ti = _top_rows(cand, kk, flat)
        hi = jnp.floor(ti * (1.0 / kk))
        lo = ti - hi * kk
        ids = _pick_rows(i1, hi, kk) * float(PEER_N_KEYS) + _pick_rows(i2, lo, kk)
        p = jnp.exp(tv - tv[0:1, :])
        w = p / jnp.sum(p, axis=0, keepdims=True)
        ids_all.append(ids)
        w_all.append(w)
    ids_t = jnp.concatenate(ids_all, axis=0).T.astype(I32)
    slot = lax.broadcasted_iota(I32, ids_t.shape, 1)
    off_ref[...] = ids_t * TABLE_ROWS_PER_EXPERT + jnp.where((slot % SUBLANES) < 4, TABLE_FRONT_PAD, 0)
    wts_ref[...] = jnp.concatenate(w_all, axis=0).T


def _topk(qp, k1, k2):
    t, qw = qp.shape
    tq = LANES
    out = lambda: pl.BlockSpec((tq, PEER_SLOTS), lambda i: (i, 0))
    key = lambda: pl.BlockSpec((PEER_N_KEYS, PEER_D_KEY // 2), lambda i: (0, 0))
    return pl.pallas_call(
        _topk_kernel,
        out_shape=(jax.ShapeDtypeStruct((t, PEER_SLOTS), I32),
                   jax.ShapeDtypeStruct((t, PEER_SLOTS), F32)),
        grid=(t // tq,),
        in_specs=[pl.BlockSpec((tq, qw), lambda i: (i, 0)), key(), key()],
        out_specs=(out(), out()),
        compiler_params=_cparams(("arbitrary",)),
        name="peer_topk",
    )(qp, k1, k2)


PEER_SLOTS = PEER_HEADS * PEER_TOPK
PEER_TOKENS_PER_STEP = 16
TABLE_ROWS_PER_EXPERT = 4
TABLE_FRONT_PAD = 4
TABLE_BACK_PAD = 8
HI_MASK = np.uint32(0xFFFF0000)


def _pack_table(w):
    e = w.shape[0]
    bits = lax.bitcast_convert_type(w.astype(BF16), jnp.uint16).astype(U32)
    bits = bits.reshape(e, 2, TABLE_ROWS_PER_EXPERT, LANES)
    words = (bits[:, 0] | (bits[:, 1] << 16)).reshape(e * TABLE_ROWS_PER_EXPERT, LANES)
    return jnp.pad(words, ((TABLE_FRONT_PAD, TABLE_BACK_PAD), (0, 0)))


def _expert_pair(tbl_ref, off_s, j, g, i, low4):
    wa = tbl_ref[pl.ds(off_s[j, g * SUBLANES + i], SUBLANES), :]
    wb = tbl_ref[pl.ds(off_s[j, g * SUBLANES + i + 4], SUBLANES), :]
    return jnp.where(low4, wa, wb)


def _gelu_exact(x):
    return 0.5 * x * (1.0 + lax.erf(x * (2.0 ** -0.5)))


def _peer_u_kernel(off_s, h_ref, wts_ref, tbl_ref, coef_ref):
    nt = (((1,), (1,)), ((), ()))
    ones = jnp.ones((SUBLANES, LANES), BF16)
    sub = lax.broadcasted_iota(I32, (SUBLANES, LANES), 0)
    low4 = sub < 4
    m2 = (sub % 4) < 2
    m1 = (sub % 2) == 0

    r4 = sub % TABLE_ROWS_PER_EXPERT

    def token_act(j):
        def chunk(c):
            return jnp.broadcast_to(h_ref[j:j + 1, c * LANES:(c + 1) * LANES], (SUBLANES, LANES))

        def by_row(c0):
            return jnp.where(r4 == 0, chunk(c0), jnp.where(r4 == 1, chunk(c0 + 1),
                             jnp.where(r4 == 2, chunk(c0 + 2), chunk(c0 + 3))))

        hlo = by_row(0)
        hhi = by_row(TABLE_ROWS_PER_EXPERT)

        def prod(g, i):
            w = _expert_pair(tbl_ref, off_s, j, g, i, low4)
            lo = lax.bitcast_convert_type(w << 16, F32)
            hi = lax.bitcast_convert_type(w & HI_MASK, F32)
            return lo * hlo + hi * hhi

        groups = []
        for g in range(PEER_SLOTS // SUBLANES):
            q1, q2, q3, q4 = prod(g, 0), prod(g, 2), prod(g, 1), prod(g, 3)
            t1 = jnp.where(m2, q1 + pltpu.roll(q1, 6, 0), q2 + pltpu.roll(q2, 2, 0))
            t2 = jnp.where(m2, q3 + pltpu.roll(q3, 6, 0), q4 + pltpu.roll(q4, 2, 0))
            groups.append(jnp.where(m1, t1 + pltpu.roll(t1, 7, 0), t2 + pltpu.roll(t2, 1, 0)))
        part = jnp.concatenate(groups, axis=0)
        hi = part.astype(BF16)
        lo = (part - hi.astype(F32)).astype(BF16)
        act = (lax.dot_general(ones, hi, nt, preferred_element_type=F32)
               + lax.dot_general(ones, lo, nt, preferred_element_type=F32))
        return act[0:1, :]

    acts = jnp.concatenate([token_act(j) for j in range(PEER_TOKENS_PER_STEP)], axis=0)
    coef_ref[...] = wts_ref[...] * _gelu_exact(acts)


def _peer_u(off, h, wts, tbl):
    t = h.shape[0]
    tb = PEER_TOKENS_PER_STEP
    return pl.pallas_call(
        _peer_u_kernel,
        out_shape=jax.ShapeDtypeStruct((t, PEER_SLOTS), F32),
        grid=(t // tb,),
        in_specs=[pl.BlockSpec((tb, PEER_SLOTS), lambda i: (i, 0), memory_space=pltpu.SMEM,
                               pipeline_mode=pl.Buffered(1)),
                  pl.BlockSpec((tb, D_MODEL), lambda i: (i, 0)),
                  pl.BlockSpec((tb, PEER_SLOTS), lambda i: (i, 0)),
                  pl.BlockSpec(memory_space=pltpu.VMEM)],
        out_specs=pl.BlockSpec((tb, PEER_SLOTS), lambda i: (i, 0)),
        compiler_params=_cparams(("arbitrary",)),
        name="peer_u",
    )(off, h, wts, tbl)


def _peer_v_tables():
    col = np.arange(PEER_SLOTS * SUBLANES)
    q, r = col // 16, col % 16
    slot = SUBLANES * (q // 4) + (q % 4) + np.where(r < 8, 0, 4)
    expand = (slot[None, :] == np.arange(PEER_SLOTS)[:, None]).astype(np.float32)
    out_row = (r // 2) % 4 + 4 * (r % 2)
    rowsel = (out_row[None, :] == np.arange(SUBLANES)[:, None]).astype(np.float32)
    return jnp.asarray(expand, BF16), jnp.asarray(rowsel, F32)


def _peer_v_kernel(off_s, coef_ref, x1_ref, g2_ref, gain_ref, expand_ref, rowsel_ref, tbl_ref, o_ref, out_scr):
    sub = lax.broadcasted_iota(I32, (SUBLANES, LANES), 0)
    low4 = sub < 4
    cexp = jnp.dot(coef_ref[...].astype(BF16), expand_ref[...], preferred_element_type=F32)
    rowsel = rowsel_ref[...]

    for j in range(PEER_TOKENS_PER_STEP):
        tiles = []
        for q in range(PEER_SLOTS // 2):
            w = _expert_pair(tbl_ref, off_s, j, q // 4, q % 4, low4)
            tiles.append(pltpu.bitcast(w, BF16))
        wmat = jnp.concatenate(tiles, axis=0)
        c = (cexp[j:j + 1, :] * rowsel).astype(BF16)
        out = jnp.dot(c, wmat, preferred_element_type=F32)
        for s in range(SUBLANES):
            out_scr[j:j + 1, s * LANES:(s + 1) * LANES] = out[s:s + 1, :]

    y = _rms_rows(out_scr[...], gain_ref[...])
    o_ref[...] = x1_ref[...] + g2_ref[0] * y


def _peer_v(off, coef, x1, g2, gain, tbl, tokens_per_sample):
    t = x1.shape[0]
    tb = PEER_TOKENS_PER_STEP
    steps_per_sample = tokens_per_sample // tb
    expand, rowsel = _peer_v_tables()
    tok = lambda: pl.BlockSpec((tb, D_MODEL), lambda i: (i, 0))
    return pl.pallas_call(
        _peer_v_kernel,
        out_shape=jax.ShapeDtypeStruct((t, D_MODEL), F32),
        grid=(t // tb,),
        in_specs=[pl.BlockSpec((tb, PEER_SLOTS), lambda i: (i, 0), memory_space=pltpu.SMEM,
                               pipeline_mode=pl.Buffered(1)),
                  pl.BlockSpec((tb, PEER_SLOTS), lambda i: (i, 0)),
                  tok(),
                  pl.BlockSpec((1, 1, D_MODEL), lambda i: (i // steps_per_sample, 0, 0)),
                  pl.BlockSpec((1, D_MODEL), lambda i: (0, 0)),
                  pl.BlockSpec(expand.shape, lambda i: (0, 0)),
                  pl.BlockSpec(rowsel.shape, lambda i: (0, 0)),
                  pl.BlockSpec(memory_space=pltpu.VMEM)],
        out_specs=tok(),
        scratch_shapes=[pltpu.VMEM((tb, D_MODEL), F32)],
        compiler_params=_cparams(("arbitrary",)),
        name="peer_v",
    )(off, coef, x1, g2, gain, expand, rowsel, tbl)


@functools.lru_cache(maxsize=None)
def _rope_tables_np(n):
    rows = n // GRID_W
    row_ids = np.repeat(np.arange(rows, dtype=np.float64), GRID_W)
    col_ids = np.tile(np.arange(GRID_W, dtype=np.float64), rows)
    freqs = ROPE_BASE ** (-np.arange(ROPE_PAIRS, dtype=np.float64) / ROPE_PAIRS)
    ang = np.concatenate([row_ids[:, None] * freqs[None, :], col_ids[:, None] * freqs[None, :]], axis=-1)
    cos, sin = np.cos(ang), np.sin(ang)
    cos_h = np.concatenate([cos, cos], axis=-1)
    sin_h = np.concatenate([-sin, sin], axis=-1)
    return (np.tile(cos_h, (1, RET_HEADS)).astype(np.float32),
            np.tile(sin_h, (1, RET_HEADS)).astype(np.float32))


@functools.lru_cache(maxsize=None)
def _dft_tables_np(n):
    scale = (n * FNET_GROUP_DIM) ** -0.5
    jn = np.arange(n, dtype=np.int64)
    ang_n = ((jn[:, None] * jn[None, :]) % n).astype(np.float64) * (2.0 * np.pi / n)
    cs = np.concatenate([np.cos(ang_n), -np.sin(ang_n)], axis=1).astype(np.float32)
    jc = np.arange(FNET_GROUP_DIM, dtype=np.int64)
    ang_c = ((jc[:, None] * jc[None, :]) % FNET_GROUP_DIM).astype(np.float64) * (2.0 * np.pi / FNET_GROUP_DIM)
    eye = np.eye(FNET_GROUPS)
    bc = np.kron(eye, np.cos(ang_c) * scale).astype(np.float32)
    bs = np.kron(eye, np.sin(ang_c) * scale).astype(np.float32)
    return bc, bs, cs


def kernel(x, c, ctx, c_ctx, w_ada, b_ada, norm_pre_mix, norm_post_mix, w_in, ret_decay_fwd, ret_decay_bwd, ret_norm_gain, w_out, norm_pre_ffn, norm_post_ffn, peer_w_query, peer_sub_keys_1, peer_sub_keys_2, peer_u, peer_v):
    b, n, d = x.shape
    depth = w_ada.shape[0]
    assert depth == 1 and d == D_MODEL and n % RET_CHUNK == 0 and n % GRID_W == 0
    t = b * n
    tm = min(256, n)
    row = lambda a: a.reshape(1, -1)

    cos, sin = [jnp.asarray(a) for a in _rope_tables_np(n)]
    bc, bs, cs = [jnp.asarray(a).astype(BF16) for a in _dft_tables_np(n)]

    pad = (-(b + 1)) % SUBLANES
    c_all = jnp.concatenate([c, c_ctx[None, :], jnp.zeros((pad, d), F32)], axis=0)
    mod = _ada(c_all, w_ada[0], row(b_ada[0]))
    sh1, sc1, g1, sh2, sc2, g2 = [mod[:b, i * d:(i + 1) * d].reshape(b, 1, d) for i in range(N_MOD)]
    csh1 = mod[b:b + 1, 0:d]
    csc1 = mod[b:b + 1, d:2 * d]

    a_f, a_b = ret_decay_fwd[0], ret_decay_bwd[0]
    a_f_lane = row(jnp.repeat(a_f, RET_DK))
    a_b_lane = row(jnp.repeat(a_b, RET_DK))

    w_in_b = w_in[0].astype(BF16)
    stf, stb = _ctx_states(ctx, row(norm_pre_mix[0]), csh1, csc1,
                           w_in_b[:, QK_WIDTH:QK_WIDTH + 2 * QK_WIDTH], a_f_lane, a_b_lane)

    q, k, v, gate, f = _inproj(x, sh1, sc1, row(norm_pre_mix[0]), w_in_b, cos, sin, tm)
    ret = _retention(q, k, v, gate, stf, stb, a_f, a_b, a_f_lane, a_b_lane, row(ret_norm_gain[0]))
    four = _fourier(f, bc, bs, cs, tm)

    x1, h2, qp = _outproj(ret, four, x, g1, sh2, sc2, row(norm_post_mix[0]), row(norm_pre_ffn[0]),
                          w_out[0].astype(BF16), peer_w_query[0].astype(BF16), tm)

    off, wts = _topk(qp.reshape(t, -1), peer_sub_keys_1[0].astype(BF16), peer_sub_keys_2[0].astype(BF16))

    coef = _peer_u(off, h2.reshape(t, d), wts, _pack_table(peer_u[0]))
    out = _peer_v(off, coef, x1.reshape(t, d), g2, row(norm_post_ffn[0]), _pack_table(peer_v[0]), n)
    return out.reshape(b, n, d)
```

```python
import functools

import numpy as np
import jax
import jax.numpy as jnp
from jax import lax
from jax.experimental import pallas as pl
from jax.experimental.pallas import tpu as pltpu

F32 = jnp.float32
BF16 = jnp.bfloat16
I32 = jnp.int32
U32 = jnp.uint32

D_MODEL = 1024
GRID_W = 64
RET_HEADS = 8
RET_DK = 64
RET_WIDTH = 512
QK_WIDTH = 512
RET_CHUNK = 128
ROPE_PAIRS = RET_DK // 4
ROPE_BASE = 10000.0
FNET_GROUPS = 4
FNET_GROUP_DIM = 128
FNET_WIDTH = 512
IN_WIDTH = 2560
PEER_HEADS = 8
PEER_D_KEY = 256
PEER_N_KEYS = 128
PEER_TOPK = 16
N_MOD = 6
EPS = 1e-6

LANES = 128
SUBLANES = 8
VMEM_LIMIT = 56 * 1024 * 1024
N_PAIRS = RET_HEADS // 2


def _cparams(sem):
    return pltpu.CompilerParams(dimension_semantics=sem, vmem_limit_bytes=VMEM_LIMIT)


def _rms_rows(x, gain):
    ms = jnp.mean(x * x, axis=-1, keepdims=True)
    return x * lax.rsqrt(ms + EPS) * gain


def _silu(x):
    return x * jax.nn.sigmoid(x)


def _log_gamma(a):
    return jnp.log1p(-jnp.exp(a))


def _ada_kernel(c_ref, w_ref, b_ref, o_ref):
    s = _silu(c_ref[...])
    o_ref[...] = jnp.dot(s, w_ref[...], preferred_element_type=F32) + b_ref[...]


def _ada(c_all, w_ada, b_ada):
    rows = c_all.shape[0]
    width = w_ada.shape[1]
    tn = 1536
    return pl.pallas_call(
        _ada_kernel,
        out_shape=jax.ShapeDtypeStruct((rows, width), F32),
        grid=(width // tn,),
        in_specs=[
            pl.BlockSpec((rows, D_MODEL), lambda j: (0, 0)),
            pl.BlockSpec((D_MODEL, tn), lambda j: (0, j)),
            pl.BlockSpec((1, tn), lambda j: (0, j)),
        ],
        out_specs=pl.BlockSpec((rows, tn), lambda j: (0, j)),
        compiler_params=_cparams(("arbitrary",)),
        name="ada",
    )(c_all, w_ada, b_ada)


def _ctx_kernel(ctx_ref, g_ref, sh_ref, sc_ref, w_ref, af_ref, ab_ref, stf_ref, stb_ref):
    x = ctx_ref[0]
    n = x.shape[0]
    h = (_rms_rows(x, g_ref[...]) * (1.0 + sc_ref[...]) + sh_ref[...]).astype(BF16)
    kv = jnp.dot(h, w_ref[...], preferred_element_type=F32)
    k = kv[:, :QK_WIDTH] * (RET_DK ** -0.5)
    v = kv[:, QK_WIDTH:].astype(BF16)
    pos = lax.broadcasted_iota(I32, (n, QK_WIDTH), 0).astype(F32)
    lgf = _log_gamma(af_ref[...])
    lgb = _log_gamma(ab_ref[...])
    kf = (k * jnp.exp(lgf * (n - 1.0 - pos))).astype(BF16)
    kb = (k * jnp.exp(lgb * pos)).astype(BF16)
    tn = (((0,), (0,)), ((), ()))
    gf = lax.dot_general(kf, v, tn, preferred_element_type=F32)
    gb = lax.dot_general(kb, v, tn, preferred_element_type=F32)
    ri = lax.broadcasted_iota(I32, (LANES, LANES), 0) // RET_DK
    ci = lax.broadcasted_iota(I32, (LANES, LANES), 1) // RET_DK
    bd = ri == ci
    for p in range(N_PAIRS):
        sl = slice(p * LANES, (p + 1) * LANES)
        stf_ref[0, p] = jnp.where(bd, gf[sl, sl], 0.0)
        stb_ref[0, p] = jnp.where(bd, gb[sl, sl], 0.0)


def _ctx_states(ctx, gain, csh, csc, w_kv, a_f, a_b):
    b, n, _ = ctx.shape
    vec = lambda: pl.BlockSpec((1, D_MODEL), lambda i: (0, 0))
    lane = lambda: pl.BlockSpec((1, QK_WIDTH), lambda i: (0, 0))
    st = jax.ShapeDtypeStruct((b, N_PAIRS, LANES, LANES), F32)
    st_spec = lambda: pl.BlockSpec((1, N_PAIRS, LANES, LANES), lambda i: (i, 0, 0, 0))
    return pl.pallas_call(
        _ctx_kernel,
        out_shape=(st, st),
        grid=(b,),
        in_specs=[
            pl.BlockSpec((1, n, D_MODEL), lambda i: (i, 0, 0)),
            vec(), vec(), vec(),
            pl.BlockSpec((D_MODEL, 2 * QK_WIDTH), lambda i: (0, 0)),
            lane(), lane(),
        ],
        out_specs=(st_spec(), st_spec()),
        compiler_params=_cparams(("arbitrary",)),
        name="ctx_states",
    )(ctx, gain, csh, csc, w_kv, a_f, a_b)


def _inproj_kernel(x_ref, sh_ref, sc_ref, g_ref, w_ref, cos_ref, sin_ref,
                   q_ref, k_ref, v_ref, gate_ref, f_ref):
    x = x_ref[0]
    tm = x.shape[0]
    h = (_rms_rows(x, g_ref[...]) * (1.0 + sc_ref[0]) + sh_ref[0]).astype(BF16)

    def proj(lo, hi):
        return jnp.dot(h, w_ref[:, lo:hi], preferred_element_type=F32)

    cos = cos_ref[...]
    sin = sin_ref[...]
    lane = lax.broadcasted_iota(I32, (tm, QK_WIDTH), 1)
    first = (lane % RET_DK) < (RET_DK // 2)

    def rope(t):
        swapped = jnp.where(first,
                            pltpu.roll(t, QK_WIDTH - RET_DK // 2, 1),
                            pltpu.roll(t, RET_DK // 2, 1))
        return t * cos + swapped * sin

    q_ref[0] = rope(proj(0, 512)).astype(BF16)
    k_ref[0] = rope(proj(512, 1024) * (RET_DK ** -0.5)).astype(BF16)
    v_ref[0] = proj(1024, 1536).astype(BF16)
    gate_ref[0] = proj(1536, 2048)
    f_ref[0] = proj(2048, 2560).astype(BF16)


def _inproj(x, sh, sc, gain, w_in, cos, sin, tm):
    b, n, _ = x.shape
    mod = lambda: pl.BlockSpec((1, 1, D_MODEL), lambda i, j: (i, 0, 0))
    tab = lambda: pl.BlockSpec((tm, QK_WIDTH), lambda i, j: (j, 0))
    out = lambda: pl.BlockSpec((1, tm, 512), lambda i, j: (i, j, 0))
    sd = lambda dt: jax.ShapeDtypeStruct((b, n, 512), dt)
    return pl.pallas_call(
        _inproj_kernel,
        out_shape=(sd(BF16), sd(BF16), sd(BF16), sd(F32), sd(BF16)),
        grid=(b, n // tm),
        in_specs=[
            pl.BlockSpec((1, tm, D_MODEL), lambda i, j: (i, j, 0)),
            mod(), mod(),
            pl.BlockSpec((1, D_MODEL), lambda i, j: (0, 0)),
            pl.BlockSpec((D_MODEL, IN_WIDTH), lambda i, j: (0, 0)),
            tab(), tab(),
        ],
        out_specs=(out(), out(), out(), out(), out()),
        compiler_params=_cparams(("arbitrary", "arbitrary")),
        name="inproj",
    )(x, sh, sc, gain, w_in, cos, sin)


def _ret_kernel(af_s, ab_s, q_ref, k_ref, v_ref, gate_ref, stf_ref, stb_ref, af_ref, ab_ref,
                gain_ref, o_ref, mask_scr, qdf_scr, qdb_scr, kdf_scr, kdb_scr, sf_scr, sb_scr,
                sb_store):
    c = RET_CHUNK
    n = q_ref.shape[1]
    nc = n // c
    lgf = _log_gamma(af_ref[...])
    lgb = _log_gamma(ab_ref[...])
    cdf = jnp.exp(lgf * float(c))
    cdb = jnp.exp(lgb * float(c))

    @pl.when(pl.program_id(0) == 0)
    def _():
        r = lax.broadcasted_iota(I32, (c, RET_WIDTH), 0).astype(F32)
        qdf_scr[...] = jnp.exp(lgf * (r + 1.0))
        kdf_scr[...] = jnp.exp(lgf * (c - 1.0 - r))
        qdb_scr[...] = jnp.exp(lgb * (c - r))
        kdb_scr[...] = jnp.exp(lgb * r)
        ri = lax.broadcasted_iota(I32, (c, c), 0)
        ci = lax.broadcasted_iota(I32, (c, c), 1)
        d = (ri - ci).astype(F32)
        for hh in range(RET_HEADS):
            gf = _log_gamma(jnp.full((c, c), af_s[hh], F32))
            gb = _log_gamma(jnp.full((c, c), ab_s[hh], F32))
            mask_scr[hh] = jnp.where(d >= 0.0, jnp.exp(gf * jnp.maximum(d, 0.0)),
                                     jnp.exp(gb * jnp.maximum(-d, 0.0)))

    ri = lax.broadcasted_iota(I32, (LANES, LANES), 0) // RET_DK
    ci = lax.broadcasted_iota(I32, (LANES, LANES), 1) // RET_DK
    bd = ri == ci
    bd_ones = jnp.where(bd, 1.0, 0.0).astype(BF16)
    lane_head = lax.broadcasted_iota(I32, (c, LANES), 1) // RET_DK
    tn = (((0,), (0,)), ((), ()))
    nt = (((1,), (1,)), ((), ()))

    for p in range(N_PAIRS):
        sf_scr[p] = stf_ref[0, p]
        sb_scr[p] = stb_ref[0, p]

    def bwd_states(i, carry):
        cc = nc - 1 - i
        rows = pl.ds(pl.multiple_of(cc * c, c), c)
        for p in range(N_PAIRS):
            sl = slice(p * LANES, (p + 1) * LANES)
            sb = sb_scr[p]
            sb_store[cc, p] = sb.astype(BF16)
            kw = (k_ref[0, rows, sl].astype(F32) * kdb_scr[:, sl]).astype(BF16)
            g = lax.dot_general(kw, v_ref[0, rows, sl], tn, preferred_element_type=F32)
            sb_scr[p] = sb * cdb[:, sl] + jnp.where(bd, g, 0.0)
        return carry

    lax.fori_loop(0, nc, bwd_states, 0)

    def fwd(cc, carry):
        rows = pl.ds(pl.multiple_of(cc * c, c), c)
        for p in range(N_PAIRS):
            sl = slice(p * LANES, (p + 1) * LANES)
            qp = q_ref[0, rows, sl]
            kp = k_ref[0, rows, sl]
            vp = v_ref[0, rows, sl]
            qf = qp.astype(F32)
            sf = sf_scr[p]
            acc = jnp.dot((qf * qdf_scr[:, sl]).astype(BF16), sf.astype(BF16),
                          preferred_element_type=F32)
            acc += jnp.dot((qf * qdb_scr[:, sl]).astype(BF16), sb_store[cc, p],
                           preferred_element_type=F32)
            for j in range(2):
                own = lane_head == j
                qm = jnp.where(own, qp, jnp.zeros_like(qp))
                s = lax.dot_general(qm, kp, nt, preferred_element_type=F32)
                pm = (s * mask_scr[2 * p + j]).astype(BF16)
                vm = jnp.where(own, vp, jnp.zeros_like(vp))
                acc += jnp.dot(pm, vm, preferred_element_type=F32)
            sq = acc * acc
            sq_hi = sq.astype(BF16)
            sq_lo = (sq - sq_hi.astype(F32)).astype(BF16)
            ms = (jnp.dot(sq_hi, bd_ones, preferred_element_type=F32)
                  + jnp.dot(sq_lo, bd_ones, preferred_element_type=F32)) * (1.0 / RET_DK)
            y = acc * lax.rsqrt(ms + EPS) * gain_ref[:, sl] * _silu(gate_ref[0, rows, sl])
            o_ref[0, rows, sl] = y.astype(BF16)
            kw = (kp.astype(F32) * kdf_scr[:, sl]).astype(BF16)
            g = lax.dot_general(kw, vp, tn, preferred_element_type=F32)
            sf_scr[p] = sf * cdf[:, sl] + jnp.where(bd, g, 0.0)
        return carry

    lax.fori_loop(0, nc, fwd, 0)


def _retention(q, k, v, gate, stf, stb, a_f, a_b, a_f_lane, a_b_lane, gain):
    b, n, _ = q.shape
    nc = n // RET_CHUNK
    seq = lambda: pl.BlockSpec((1, n, RET_WIDTH), lambda i, *_: (i, 0, 0))
    st = lambda: pl.BlockSpec((1, N_PAIRS, LANES, LANES), lambda i, *_: (i, 0, 0, 0))
    lane = lambda: pl.BlockSpec((1, RET_WIDTH), lambda i, *_: (0, 0))
    grid_spec = pltpu.PrefetchScalarGridSpec(
        num_scalar_prefetch=2,
        grid=(b,),
        in_specs=[seq(), seq(), seq(), seq(), st(), st(), lane(), lane(), lane()],
        out_specs=seq(),
        scratch_shapes=[
            pltpu.VMEM((RET_HEADS, RET_CHUNK, RET_CHUNK), F32),
            pltpu.VMEM((RET_CHUNK, RET_WIDTH), F32),
            pltpu.VMEM((RET_CHUNK, RET_WIDTH), F32),
            pltpu.VMEM((RET_CHUNK, RET_WIDTH), F32),
            pltpu.VMEM((RET_CHUNK, RET_WIDTH), F32),
            pltpu.VMEM((N_PAIRS, LANES, LANES), F32),
            pltpu.VMEM((N_PAIRS, LANES, LANES), F32),
            pltpu.VMEM((nc, N_PAIRS, LANES, LANES), BF16),
        ],
    )
    return pl.pallas_call(
        _ret_kernel,
        out_shape=jax.ShapeDtypeStruct((b, n, RET_WIDTH), BF16),
        grid_spec=grid_spec,
        compiler_params=_cparams(("arbitrary",)),
        name="retention",
    )(a_f, a_b, q, k, v, gate, stf, stb, a_f_lane, a_b_lane, gain)


def _four_kernel(f_ref, bc_ref, bs_ref, cs_ref, o_ref, z_scr):
    n = f_ref.shape[1]

    @pl.when(pl.program_id(1) == 0)
    def _():
        f = f_ref[0]
        z_scr[0:n, :] = jnp.dot(f, bc_ref[...], preferred_element_type=F32).astype(BF16)
        z_scr[n:2 * n, :] = jnp.dot(f, bs_ref[...], preferred_element_type=F32).astype(BF16)

    o_ref[0] = jnp.dot(cs_ref[...], z_scr[...], preferred_element_type=F32).astype(BF16)


def _fourier(f, bc, bs, cs, tn):
    b, n, _ = f.shape
    return pl.pallas_call(
        _four_kernel,
        out_shape=jax.ShapeDtypeStruct((b, n, FNET_WIDTH), BF16),
        grid=(b, n // tn),
        in_specs=[
            pl.BlockSpec((1, n, FNET_WIDTH), lambda i, j: (i, 0, 0)),
            pl.BlockSpec((FNET_WIDTH, FNET_WIDTH), lambda i, j: (0, 0)),
            pl.BlockSpec((FNET_WIDTH, FNET_WIDTH), lambda i, j: (0, 0)),
            pl.BlockSpec((tn, 2 * n), lambda i, j: (j, 0)),
        ],
        out_specs=pl.BlockSpec((1, tn, FNET_WIDTH), lambda i, j: (i, j, 0)),
        scratch_shapes=[pltpu.VMEM((2 * n, FNET_WIDTH), BF16)],
        compiler_params=_cparams(("arbitrary", "arbitrary")),
        name="fourier",
    )(f, bc, bs, cs)


def _outp_kernel(ret_ref, four_ref, x_ref, g1_ref, sh2_ref, sc2_ref, gpost_ref, gpre_ref,
                 wo_ref, wq_ref, x1_ref, h2_ref, qp_ref):
    mix = jnp.dot(ret_ref[0], wo_ref[0:RET_WIDTH, :], preferred_element_type=F32)
    mix += jnp.dot(four_ref[0], wo_ref[RET_WIDTH:, :], preferred_element_type=F32)
    x1 = x_ref[0] + g1_ref[0] * _rms_rows(mix, gpost_ref[...])
    x1_ref[0] = x1
    h2 = _rms_rows(x1, gpre_ref[...]) * (1.0 + sc2_ref[0]) + sh2_ref[0]
    h2_ref[0] = h2
    qp_ref[0] = jnp.dot(h2.astype(BF16), wq_ref[...], preferred_element_type=F32).astype(BF16)


def _outproj(ret, four, x, g1, sh2, sc2, gpost, gpre, w_out, w_q, tm):
    b, n, _ = x.shape
    qw = w_q.shape[1]
    mod = lambda: pl.BlockSpec((1, 1, D_MODEL), lambda i, j: (i, 0, 0))
    vec = lambda: pl.BlockSpec((1, D_MODEL), lambda i, j: (0, 0))
    half = lambda: pl.BlockSpec((1, tm, 512), lambda i, j: (i, j, 0))
    full = lambda: pl.BlockSpec((1, tm, D_MODEL), lambda i, j: (i, j, 0))
    return pl.pallas_call(
        _outp_kernel,
        out_shape=(jax.ShapeDtypeStruct((b, n, D_MODEL), F32),
                   jax.ShapeDtypeStruct((b, n, D_MODEL), F32),
                   jax.ShapeDtypeStruct((b, n, qw), BF16)),
        grid=(b, n // tm),
        in_specs=[half(), half(), full(), mod(), mod(), mod(), vec(), vec(),
                  pl.BlockSpec(w_out.shape, lambda i, j: (0, 0)),
                  pl.BlockSpec(w_q.shape, lambda i, j: (0, 0))],
        out_specs=(full(), full(), pl.BlockSpec((1, tm, qw), lambda i, j: (i, j, 0))),
        compiler_params=_cparams(("arbitrary", "arbitrary")),
        name="outproj",
    )(ret, four, x, g1, sh2, sc2, gpost, gpre, w_out, w_q)


def _top_rows(s, kk, rid=None):
    rows, tq = s.shape
    if rid is None:
        rid = lax.broadcasted_iota(I32, (rows, tq), 0).astype(F32)
    slot = lax.broadcasted_iota(I32, (kk, tq), 0)
    vals = jnp.zeros((kk, tq), F32)
    idxs = jnp.zeros((kk, tq), F32)
    for it in range(kk):
        m = jnp.max(s, axis=0, keepdims=True)
        am = jnp.min(jnp.where(s == m, rid, float(2 ** 20)), axis=0, keepdims=True)
        vals = jnp.where(slot == it, m, vals)
        idxs = jnp.where(slot == it, am, idxs)
        s = jnp.where(rid == am, -jnp.inf, s)
    return vals, idxs


def _pick_rows(table, sel, kk):
    out = jnp.zeros_like(table)
    for i in range(kk):
        out = jnp.where(sel == float(i), table[i:i + 1, :], out)
    return out


def _pair_candidates(v1, v2):
    kk = PEER_TOPK
    tq = v1.shape[1]
    sub = lax.broadcasted_iota(I32, (SUBLANES, tq), 0)
    subf = sub.astype(F32)
    upper = sub >= 4
    vals = [v1[0:1] + v2[0:8], v1[0:1] + v2[8:16], v1[1:2] + v2[0:8], v1[2:3] + v2[0:8], v1[3:4] + v2[0:8]]
    ids = [subf, subf + 8.0, subf + float(kk), subf + float(2 * kk), subf + float(3 * kk)]
    for j in range(3):
        vals.append(jnp.where(upper, v1[0:8] + v2[j:j + 1], -jnp.inf))
        ids.append(subf * float(kk) + float(j))
    vals.append(v1[8:16] + v2[0:1])
    ids.append((subf + 8.0) * float(kk))
    return jnp.concatenate(vals, axis=0), jnp.concatenate(ids, axis=0)


def _topk_kernel(qp_ref, k1_ref, k2_ref, off_ref, wts_ref):
    kk = PEER_TOPK
    half = PEER_D_KEY // 2
    nt = (((1,), (1,)), ((), ()))
    k1 = k1_ref[...]
    k2 = k2_ref[...]
    ids_all, w_all = [], []
    for hh in range(PEER_HEADS):
        q1 = qp_ref[:, hh * PEER_D_KEY: hh * PEER_D_KEY + half]
        q2 = qp_ref[:, hh * PEER_D_KEY + half: (hh + 1) * PEER_D_KEY]
        s1 = lax.dot_general(k1, q1, nt, preferred_element_type=F32)
        s2 = lax.dot_general(k2, q2, nt, preferred_element_type=F32)
        v1, i1 = _top_rows(s1, kk)
        v2, i2 = _top_rows(s2, kk)
        cand, flat = _pair_candidates(v1, v2)
        tv, ti = _top_rows(cand, kk, flat)
        hi = jnp.floor(ti * (1.0 / kk))
        lo = ti - hi * kk
        ids = _pick_rows(i1, hi, kk) * float(PEER_N_KEYS) + _pick_rows(i2, lo, kk)
        p = jnp.exp(tv - tv[0:1, :])
        w = p / jnp.sum(p, axis=0, keepdims=True)
        ids_all.append(ids)
        w_all.append(w)
    ids_t = jnp.concatenate(ids_all, axis=0).T.astype(I32)
    slot = lax.broadcasted_iota(I32, ids_t.shape, 1)
    off_ref[...] = ids_t * TABLE_ROWS_PER_EXPERT + jnp.where((slot % SUBLANES) < 4, TABLE_FRONT_PAD, 0)
    wts_ref[...] = jnp.concatenate(w_all, axis=0).T


def _topk(qp, k1, k2):
    t, qw = qp.shape
    tq = LANES
    out = lambda: pl.BlockSpec((tq, PEER_SLOTS), lambda i: (i, 0))
    key = lambda: pl.BlockSpec((PEER_N_KEYS, PEER_D_KEY // 2), lambda i: (0, 0))
    return pl.pallas_call(
        _topk_kernel,
        out_shape=(jax.ShapeDtypeStruct((t, PEER_SLOTS), I32),
                   jax.ShapeDtypeStruct((t, PEER_SLOTS), F32)),
        grid=(t // tq,),
        in_specs=[pl.BlockSpec((tq, qw), lambda i: (i, 0)), key(), key()],
        out_specs=(out(), out()),
        compiler_params=_cparams(("arbitrary",)),
        name="peer_topk",
    )(qp, k1, k2)


PEER_SLOTS = PEER_HEADS * PEER_TOPK
PEER_TOKENS_PER_STEP = 32
PEER_TOKENS_PER_HALF = PEER_TOKENS_PER_STEP // 2
TABLE_ROWS_PER_EXPERT = 4
TABLE_FRONT_PAD = 4
TABLE_BACK_PAD = 8
HI_MASK = np.uint32(0xFFFF0000)


def _pack_table(w):
    e = w.shape[0]
    bits = lax.bitcast_convert_type(w.astype(BF16), jnp.uint16).astype(U32)
    bits = bits.reshape(e, 2, TABLE_ROWS_PER_EXPERT, LANES)
    words = (bits[:, 0] | (bits[:, 1] << 16)).reshape(e * TABLE_ROWS_PER_EXPERT, LANES)
    return jnp.pad(words, ((TABLE_FRONT_PAD, TABLE_BACK_PAD), (0, 0)))


def _expert_pair(tbl_ref, off_s, j, g, i, low4):
    wa = tbl_ref[pl.ds(off_s[j, g * SUBLANES + i], SUBLANES), :]
    wb = tbl_ref[pl.ds(off_s[j, g * SUBLANES + i + 4], SUBLANES), :]
    return jnp.where(low4, wa, wb)


def _for_token_halves(off_hbm, bufs, sem, half_body):
    step = pl.program_id(0)
    last = pl.num_programs(0) - 1

    def fill(at_step, half):
        start = at_step * PEER_TOKENS_PER_STEP + half * PEER_TOKENS_PER_HALF
        return pltpu.make_async_copy(off_hbm.at[pl.ds(start, PEER_TOKENS_PER_HALF)], bufs[half], sem.at[half])

    @pl.when(step == 0)
    def _():
        for half in range(2):
            fill(0, half).start()

    for half in range(2):
        fill(step, half).wait()
        half_body(half, bufs[half])

        @pl.when(step < last)
        def _():
            fill(step + 1, half).start()


def _offset_scratch():
    return [pltpu.SMEM((PEER_TOKENS_PER_HALF, PEER_SLOTS), I32),
            pltpu.SMEM((PEER_TOKENS_PER_HALF, PEER_SLOTS), I32),
            pltpu.SemaphoreType.DMA((2,))]


def _gelu_exact(x):
    return 0.5 * x * (1.0 + lax.erf(x * (2.0 ** -0.5)))


def _peer_u_kernel(off_hbm, h_ref, wts_ref, tbl_ref, coef_ref, off_a, off_b, sem):
    nt = (((1,), (1,)), ((), ()))
    ones = jnp.ones((SUBLANES, LANES), BF16)
    sub = lax.broadcasted_iota(I32, (SUBLANES, LANES), 0)
    low4 = sub < 4
    m2 = (sub % 4) < 2
    m1 = (sub % 2) == 0
    r4 = sub % TABLE_ROWS_PER_EXPERT

    def token_act(off_s, jj, j):
        def chunk(c):
            return jnp.broadcast_to(h_ref[j:j + 1, c * LANES:(c + 1) * LANES], (SUBLANES, LANES))

        def by_row(c0):
            return jnp.where(r4 == 0, chunk(c0), jnp.where(r4 == 1, chunk(c0 + 1),
                             jnp.where(r4 == 2, chunk(c0 + 2), chunk(c0 + 3))))

        hlo = by_row(0)
        hhi = by_row(TABLE_ROWS_PER_EXPERT)

        def prod(g, i):
            w = _expert_pair(tbl_ref, off_s, jj, g, i, low4)
            lo = lax.bitcast_convert_type(w << 16, F32)
            hi = lax.bitcast_convert_type(w & HI_MASK, F32)
            return lo * hlo + hi * hhi

        groups = []
        for g in range(PEER_SLOTS // SUBLANES):
            q1, q2, q3, q4 = prod(g, 0), prod(g, 2), prod(g, 1), prod(g, 3)
            t1 = jnp.where(m2, q1 + pltpu.roll(q1, 6, 0), q2 + pltpu.roll(q2, 2, 0))
            t2 = jnp.where(m2, q3 + pltpu.roll(q3, 6, 0), q4 + pltpu.roll(q4, 2, 0))
            groups.append(jnp.where(m1, t1 + pltpu.roll(t1, 7, 0), t2 + pltpu.roll(t2, 1, 0)))
        part = jnp.concatenate(groups, axis=0)
        hi = part.astype(BF16)
        lo = (part - hi.astype(F32)).astype(BF16)
        act = (lax.dot_general(ones, hi, nt, preferred_element_type=F32)
               + lax.dot_general(ones, lo, nt, preferred_element_type=F32))
        return act[0:1, :]

    def half_body(half, off_s):
        base = half * PEER_TOKENS_PER_HALF
        acts = jnp.concatenate([token_act(off_s, jj, base + jj) for jj in range(PEER_TOKENS_PER_HALF)], axis=0)
        rows = pl.ds(base, PEER_TOKENS_PER_HALF)
        coef_ref[rows, :] = wts_ref[rows, :] * _gelu_exact(acts)

    _for_token_halves(off_hbm, (off_a, off_b), sem, half_body)


def _peer_u(off, h, wts, tbl):
    t = h.shape[0]
    tb = PEER_TOKENS_PER_STEP
    return pl.pallas_call(
        _peer_u_kernel,
        out_shape=jax.ShapeDtypeStruct((t, PEER_SLOTS), F32),
        grid=(t // tb,),
        in_specs=[pl.BlockSpec(memory_space=pl.ANY),
                  pl.BlockSpec((tb, D_MODEL), lambda i: (i, 0)),
                  pl.BlockSpec((tb, PEER_SLOTS), lambda i: (i, 0)),
                  pl.BlockSpec(memory_space=pltpu.VMEM)],
        out_specs=pl.BlockSpec((tb, PEER_SLOTS), lambda i: (i, 0)),
        scratch_shapes=_offset_scratch(),
        compiler_params=_cparams(("arbitrary",)),
        name="peer_u",
    )(off, h, wts, tbl)


def _peer_v_tables():
    col = np.arange(PEER_SLOTS * SUBLANES)
    q, r = col // 16, col % 16
    slot = SUBLANES * (q // 4) + (q % 4) + np.where(r < 8, 0, 4)
    expand = (slot[None, :] == np.arange(PEER_SLOTS)[:, None]).astype(np.float32)
    out_row = (r // 2) % 4 + 4 * (r % 2)
    rowsel = (out_row[None, :] == np.arange(SUBLANES)[:, None]).astype(np.float32)
    return jnp.asarray(expand, BF16), jnp.asarray(rowsel, F32)


def _peer_v_kernel(off_hbm, coef_ref, x1_ref, g2_ref, gain_ref, expand_ref, rowsel_ref, tbl_ref, o_ref,
                   out_scr, off_a, off_b, sem):
    sub = lax.broadcasted_iota(I32, (SUBLANES, LANES), 0)
    low4 = sub < 4
    cexp = jnp.dot(coef_ref[...].astype(BF16), expand_ref[...], preferred_element_type=F32)
    rowsel = rowsel_ref[...]

    def half_body(half, off_s):
        for jj in range(PEER_TOKENS_PER_HALF):
            j = half * PEER_TOKENS_PER_HALF + jj
            tiles = []
            for q in range(PEER_SLOTS // 2):
                w = _expert_pair(tbl_ref, off_s, jj, q // 4, q % 4, low4)
                tiles.append(pltpu.bitcast(w, BF16))
            wmat = jnp.concatenate(tiles, axis=0)
            c = (cexp[j:j + 1, :] * rowsel).astype(BF16)
            out = jnp.dot(c, wmat, preferred_element_type=F32)
            for s in range(SUBLANES):
                out_scr[j:j + 1, s * LANES:(s + 1) * LANES] = out[s:s + 1, :]

    _for_token_halves(off_hbm, (off_a, off_b), sem, half_body)
    y = _rms_rows(out_scr[...], gain_ref[...])
    o_ref[...] = x1_ref[...] + g2_ref[0] * y


def _peer_v(off, coef, x1, g2, gain, tbl, tokens_per_sample):
    t = x1.shape[0]
    tb = PEER_TOKENS_PER_STEP
    steps_per_sample = tokens_per_sample // tb
    expand, rowsel = _peer_v_tables()
    tok = lambda: pl.BlockSpec((tb, D_MODEL), lambda i: (i, 0))
    return pl.pallas_call(
        _peer_v_kernel,
        out_shape=jax.ShapeDtypeStruct((t, D_MODEL), F32),
        grid=(t // tb,),
        in_specs=[pl.BlockSpec(memory_space=pl.ANY),
                  pl.BlockSpec((tb, PEER_SLOTS), lambda i: (i, 0)),
                  tok(),
                  pl.BlockSpec((1, 1, D_MODEL), lambda i: (i // steps_per_sample, 0, 0)),
                  pl.BlockSpec((1, D_MODEL), lambda i: (0, 0)),
                  pl.BlockSpec(expand.shape, lambda i: (0, 0)),
                  pl.BlockSpec(rowsel.shape, lambda i: (0, 0)),
                  pl.BlockSpec(memory_space=pltpu.VMEM)],
        out_specs=tok(),
        scratch_shapes=[pltpu.VMEM((tb, D_MODEL), F32)] + _offset_scratch(),
        compiler_params=_cparams(("arbitrary",)),
        name="peer_v",
    )(off, coef, x1, g2, gain, expand, rowsel, tbl)


@functools.lru_cache(maxsize=None)
def _rope_tables_np(n):
    rows = n // GRID_W
    row_ids = np.repeat(np.arange(rows, dtype=np.float64), GRID_W)
    col_ids = np.tile(np.arange(GRID_W, dtype=np.float64), rows)
    freqs = ROPE_BASE ** (-np.arange(ROPE_PAIRS, dtype=np.float64) / ROPE_PAIRS)
    ang = np.concatenate([row_ids[:, None] * freqs[None, :], col_ids[:, None] * freqs[None, :]], axis=-1)
    cos, sin = np.cos(ang), np.sin(ang)
    cos_h = np.concatenate([cos, cos], axis=-1)
    sin_h = np.concatenate([-sin, sin], axis=-1)
    return (np.tile(cos_h, (1, RET_HEADS)).astype(np.float32),
            np.tile(sin_h, (1, RET_HEADS)).astype(np.float32))


@functools.lru_cache(maxsize=None)
def _dft_tables_np(n):
    scale = (n * FNET_GROUP_DIM) ** -0.5
    jn = np.arange(n, dtype=np.int64)
    ang_n = ((jn[:, None] * jn[None, :]) % n).astype(np.float64) * (2.0 * np.pi / n)
    cs = np.concatenate([np.cos(ang_n), -np.sin(ang_n)], axis=1).astype(np.float32)
    jc = np.arange(FNET_GROUP_DIM, dtype=np.int64)
    ang_c = ((jc[:, None] * jc[None, :]) % FNET_GROUP_DIM).astype(np.float64) * (2.0 * np.pi / FNET_GROUP_DIM)
    eye = np.eye(FNET_GROUPS)
    bc = np.kron(eye, np.cos(ang_c) * scale).astype(np.float32)
    bs = np.kron(eye, np.sin(ang_c) * scale).astype(np.float32)
    return bc, bs, cs


def kernel(x, c, ctx, c_ctx, w_ada, b_ada, norm_pre_mix, norm_post_mix, w_in, ret_decay_fwd, ret_decay_bwd, ret_norm_gain, w_out, norm_pre_ffn, norm_post_ffn, peer_w_query, peer_sub_keys_1, peer_sub_keys_2, peer_u, peer_v):
    b, n, d = x.shape
    depth = w_ada.shape[0]
    assert depth == 1 and d == D_MODEL and n % RET_CHUNK == 0 and n % GRID_W == 0
    t = b * n
    tm = min(256, n)
    row = lambda a: a.reshape(1, -1)

    cos, sin = [jnp.asarray(a) for a in _rope_tables_np(n)]
    bc, bs, cs = [jnp.asarray(a).astype(BF16) for a in _dft_tables_np(n)]

    pad = (-(b + 1)) % SUBLANES
    c_all = jnp.concatenate([c, c_ctx[None, :], jnp.zeros((pad, d), F32)], axis=0)
    mod = _ada(c_all, w_ada[0], row(b_ada[0]))
    sh1, sc1, g1, sh2, sc2, g2 = [mod[:b, i * d:(i + 1) * d].reshape(b, 1, d) for i in range(N_MOD)]
    csh1 = mod[b:b + 1, 0:d]
    csc1 = mod[b:b + 1, d:2 * d]

    a_f, a_b = ret_decay_fwd[0], ret_decay_bwd[0]
    a_f_lane = row(jnp.repeat(a_f, RET_DK))
    a_b_lane = row(jnp.repeat(a_b, RET_DK))

    w_in_b = w_in[0].astype(BF16)
    stf, stb = _ctx_states(ctx, row(norm_pre_mix[0]), csh1, csc1,
                           w_in_b[:, QK_WIDTH:QK_WIDTH + 2 * QK_WIDTH], a_f_lane, a_b_lane)

    q, k, v, gate, f = _inproj(x, sh1, sc1, row(norm_pre_mix[0]), w_in_b, cos, sin, tm)
    ret = _retention(q, k, v, gate, stf, stb, a_f, a_b, a_f_lane, a_b_lane, row(ret_norm_gain[0]))
    four = _fourier(f, bc, bs, cs, tm)

    x1, h2, qp = _outproj(ret, four, x, g1, sh2, sc2, row(norm_post_mix[0]), row(norm_pre_ffn[0]),
                          w_out[0].astype(BF16), peer_w_query[0].astype(BF16), tm)

    off, wts = _topk(qp.reshape(t, -1), peer_sub_keys_1[0].astype(BF16), peer_sub_keys_2[0].astype(BF16))

    coef = _peer_u(off, h2.reshape(t, d), wts, _pack_table(peer_u[0]))
    out = _peer_v(off, coef, x1.reshape(t, d), g2, row(norm_post_ffn[0]), _pack_table(peer_v[0]), n)
    return out.reshape(b, n, d)
```

```python
import functools

import numpy as np
import jax
import jax.numpy as jnp
from jax import lax
from jax.experimental import pallas as pl
from jax.experimental.pallas import tpu as pltpu

F32 = jnp.float32
BF16 = jnp.bfloat16
I32 = jnp.int32
U32 = jnp.uint32

D_MODEL = 1024
GRID_W = 64
RET_HEADS = 8
RET_DK = 64
RET_WIDTH = 512
QK_WIDTH = 512
RET_CHUNK = 128
ROPE_PAIRS = RET_DK // 4
ROPE_BASE = 10000.0
FNET_GROUPS = 4
FNET_GROUP_DIM = 128
FNET_WIDTH = 512
IN_WIDTH = 2560
PEER_HEADS = 8
PEER_D_KEY = 256
PEER_N_KEYS = 128
PEER_TOPK = 16
N_MOD = 6
EPS = 1e-6

LANES = 128
SUBLANES = 8
VMEM_LIMIT = 56 * 1024 * 1024
N_PAIRS = RET_HEADS // 2


def _cparams(sem):
    return pltpu.CompilerParams(dimension_semantics=sem, vmem_limit_bytes=VMEM_LIMIT)


def _rms_rows(x, gain):
    ms = jnp.mean(x * x, axis=-1, keepdims=True)
    return x * lax.rsqrt(ms + EPS) * gain


def _silu(x):
    return x * jax.nn.sigmoid(x)


def _log_gamma(a):
    return jnp.log1p(-jnp.exp(a))


def _ada_kernel(c_ref, w_ref, b_ref, o_ref):
    s = _silu(c_ref[...])
    o_ref[...] = jnp.dot(s, w_ref[...], preferred_element_type=F32) + b_ref[...]


def _ada(c_all, w_ada, b_ada):
    rows = c_all.shape[0]
    width = w_ada.shape[1]
    tn = 1536
    return pl.pallas_call(
        _ada_kernel,
        out_shape=jax.ShapeDtypeStruct((rows, width), F32),
        grid=(width // tn,),
        in_specs=[
            pl.BlockSpec((rows, D_MODEL), lambda j: (0, 0)),
            pl.BlockSpec((D_MODEL, tn), lambda j: (0, j)),
            pl.BlockSpec((1, tn), lambda j: (0, j)),
        ],
        out_specs=pl.BlockSpec((rows, tn), lambda j: (0, j)),
        compiler_params=_cparams(("arbitrary",)),
        name="ada",
    )(c_all, w_ada, b_ada)


def _ctx_kernel(ctx_ref, g_ref, sh_ref, sc_ref, w_ref, af_ref, ab_ref, stf_ref, stb_ref):
    x = ctx_ref[0]
    n = x.shape[0]
    h = (_rms_rows(x, g_ref[...]) * (1.0 + sc_ref[...]) + sh_ref[...]).astype(BF16)
    kv = jnp.dot(h, w_ref[...], preferred_element_type=F32)
    k = kv[:, :QK_WIDTH] * (RET_DK ** -0.5)
    v = kv[:, QK_WIDTH:].astype(BF16)
    pos = lax.broadcasted_iota(I32, (n, QK_WIDTH), 0).astype(F32)
    lgf = _log_gamma(af_ref[...])
    lgb = _log_gamma(ab_ref[...])
    kf = (k * jnp.exp(lgf * (n - 1.0 - pos))).astype(BF16)
    kb = (k * jnp.exp(lgb * pos)).astype(BF16)
    tn = (((0,), (0,)), ((), ()))
    gf = lax.dot_general(kf, v, tn, preferred_element_type=F32)
    gb = lax.dot_general(kb, v, tn, preferred_element_type=F32)
    ri = lax.broadcasted_iota(I32, (LANES, LANES), 0) // RET_DK
    ci = lax.broadcasted_iota(I32, (LANES, LANES), 1) // RET_DK
    bd = ri == ci
    for p in range(N_PAIRS):
        sl = slice(p * LANES, (p + 1) * LANES)
        stf_ref[0, p] = jnp.where(bd, gf[sl, sl], 0.0)
        stb_ref[0, p] = jnp.where(bd, gb[sl, sl], 0.0)


def _ctx_states(ctx, gain, csh, csc, w_kv, a_f, a_b):
    b, n, _ = ctx.shape
    vec = lambda: pl.BlockSpec((1, D_MODEL), lambda i: (0, 0))
    lane = lambda: pl.BlockSpec((1, QK_WIDTH), lambda i: (0, 0))
    st = jax.ShapeDtypeStruct((b, N_PAIRS, LANES, LANES), F32)
    st_spec = lambda: pl.BlockSpec((1, N_PAIRS, LANES, LANES), lambda i: (i, 0, 0, 0))
    return pl.pallas_call(
        _ctx_kernel,
        out_shape=(st, st),
        grid=(b,),
        in_specs=[
            pl.BlockSpec((1, n, D_MODEL), lambda i: (i, 0, 0)),
            vec(), vec(), vec(),
            pl.BlockSpec((D_MODEL, 2 * QK_WIDTH), lambda i: (0, 0)),
            lane(), lane(),
        ],
        out_specs=(st_spec(), st_spec()),
        compiler_params=_cparams(("arbitrary",)),
        name="ctx_states",
    )(ctx, gain, csh, csc, w_kv, a_f, a_b)


def _inproj_kernel(x_ref, sh_ref, sc_ref, g_ref, w_ref, cos_ref, sin_ref,
                   q_ref, k_ref, v_ref, gate_ref, f_ref):
    x = x_ref[0]
    tm = x.shape[0]
    h = (_rms_rows(x, g_ref[...]) * (1.0 + sc_ref[0]) + sh_ref[0]).astype(BF16)

    def proj(lo, hi):
        return jnp.dot(h, w_ref[:, lo:hi], preferred_element_type=F32)

    cos = cos_ref[...]
    sin = sin_ref[...]
    lane = lax.broadcasted_iota(I32, (tm, QK_WIDTH), 1)
    first = (lane % RET_DK) < (RET_DK // 2)

    def rope(t):
        swapped = jnp.where(first,
                            pltpu.roll(t, QK_WIDTH - RET_DK // 2, 1),
                            pltpu.roll(t, RET_DK // 2, 1))
        return t * cos + swapped * sin

    q_ref[0] = rope(proj(0, 512)).astype(BF16)
    k_ref[0] = rope(proj(512, 1024) * (RET_DK ** -0.5)).astype(BF16)
    v_ref[0] = proj(1024, 1536).astype(BF16)
    gate_ref[0] = proj(1536, 2048)
    f_ref[0] = proj(2048, 2560).astype(BF16)


def _inproj(x, sh, sc, gain, w_in, cos, sin, tm):
    b, n, _ = x.shape
    mod = lambda: pl.BlockSpec((1, 1, D_MODEL), lambda i, j: (i, 0, 0))
    tab = lambda: pl.BlockSpec((tm, QK_WIDTH), lambda i, j: (j, 0))
    out = lambda: pl.BlockSpec((1, tm, 512), lambda i, j: (i, j, 0))
    sd = lambda dt: jax.ShapeDtypeStruct((b, n, 512), dt)
    return pl.pallas_call(
        _inproj_kernel,
        out_shape=(sd(BF16), sd(BF16), sd(BF16), sd(F32), sd(BF16)),
        grid=(b, n // tm),
        in_specs=[
            pl.BlockSpec((1, tm, D_MODEL), lambda i, j: (i, j, 0)),
            mod(), mod(),
            pl.BlockSpec((1, D_MODEL), lambda i, j: (0, 0)),
            pl.BlockSpec((D_MODEL, IN_WIDTH), lambda i, j: (0, 0)),
            tab(), tab(),
        ],
        out_specs=(out(), out(), out(), out(), out()),
        compiler_params=_cparams(("arbitrary", "arbitrary")),
        name="inproj",
    )(x, sh, sc, gain, w_in, cos, sin)


def _ret_kernel(af_s, ab_s, q_ref, k_ref, v_ref, gate_ref, stf_ref, stb_ref, af_ref, ab_ref,
                gain_ref, o_ref, mask_scr, qdf_scr, qdb_scr, kdf_scr, kdb_scr, sf_scr, sb_scr,
                sb_store):
    c = RET_CHUNK
    n = q_ref.shape[1]
    nc = n // c
    lgf = _log_gamma(af_ref[...])
    lgb = _log_gamma(ab_ref[...])
    cdf = jnp.exp(lgf * float(c))
    cdb = jnp.exp(lgb * float(c))

    @pl.when(pl.program_id(0) == 0)
    def _():
        r = lax.broadcasted_iota(I32, (c, RET_WIDTH), 0).astype(F32)
        qdf_scr[...] = jnp.exp(lgf * (r + 1.0))
        kdf_scr[...] = jnp.exp(lgf * (c - 1.0 - r))
        qdb_scr[...] = jnp.exp(lgb * (c - r))
        kdb_scr[...] = jnp.exp(lgb * r)
        ri = lax.broadcasted_iota(I32, (c, c), 0)
        ci = lax.broadcasted_iota(I32, (c, c), 1)
        d = (ri - ci).astype(F32)
        for hh in range(RET_HEADS):
            gf = _log_gamma(jnp.full((c, c), af_s[hh], F32))
            gb = _log_gamma(jnp.full((c, c), ab_s[hh], F32))
            mask_scr[hh] = jnp.where(d >= 0.0, jnp.exp(gf * jnp.maximum(d, 0.0)),
                                     jnp.exp(gb * jnp.maximum(-d, 0.0)))

    ri = lax.broadcasted_iota(I32, (LANES, LANES), 0) // RET_DK
    ci = lax.broadcasted_iota(I32, (LANES, LANES), 1) // RET_DK
    bd = ri == ci
    bd_ones = jnp.where(bd, 1.0, 0.0).astype(BF16)
    lane_head = lax.broadcasted_iota(I32, (c, LANES), 1) // RET_DK
    tn = (((0,), (0,)), ((), ()))
    nt = (((1,), (1,)), ((), ()))

    for p in range(N_PAIRS):
        sf_scr[p] = stf_ref[0, p]
        sb_scr[p] = stb_ref[0, p]

    def bwd_states(i, carry):
        cc = nc - 1 - i
        rows = pl.ds(pl.multiple_of(cc * c, c), c)
        for p in range(N_PAIRS):
            sl = slice(p * LANES, (p + 1) * LANES)
            sb = sb_scr[p]
            sb_store[cc, p] = sb.astype(BF16)
            kw = (k_ref[0, rows, sl].astype(F32) * kdb_scr[:, sl]).astype(BF16)
            g = lax.dot_general(kw, v_ref[0, rows, sl], tn, preferred_element_type=F32)
            sb_scr[p] = sb * cdb[:, sl] + jnp.where(bd, g, 0.0)
        return carry

    lax.fori_loop(0, nc, bwd_states, 0)

    def fwd(cc, carry):
        rows = pl.ds(pl.multiple_of(cc * c, c), c)
        for p in range(N_PAIRS):
            sl = slice(p * LANES, (p + 1) * LANES)
            qp = q_ref[0, rows, sl]
            kp = k_ref[0, rows, sl]
            vp = v_ref[0, rows, sl]
            qf = qp.astype(F32)
            sf = sf_scr[p]
            acc = jnp.dot((qf * qdf_scr[:, sl]).astype(BF16), sf.astype(BF16),
                          preferred_element_type=F32)
            acc += jnp.dot((qf * qdb_scr[:, sl]).astype(BF16), sb_store[cc, p],
                           preferred_element_type=F32)
            for j in range(2):
                own = lane_head == j
                qm = jnp.where(own, qp, jnp.zeros_like(qp))
                s = lax.dot_general(qm, kp, nt, preferred_element_type=F32)
                pm = (s * mask_scr[2 * p + j]).astype(BF16)
                vm = jnp.where(own, vp, jnp.zeros_like(vp))
                acc += jnp.dot(pm, vm, preferred_element_type=F32)
            sq = acc * acc
            sq_hi = sq.astype(BF16)
            sq_lo = (sq - sq_hi.astype(F32)).astype(BF16)
            ms = (jnp.dot(sq_hi, bd_ones, preferred_element_type=F32)
                  + jnp.dot(sq_lo, bd_ones, preferred_element_type=F32)) * (1.0 / RET_DK)
            y = acc * lax.rsqrt(ms + EPS) * gain_ref[:, sl] * _silu(gate_ref[0, rows, sl])
            o_ref[0, rows, sl] = y.astype(BF16)
            kw = (kp.astype(F32) * kdf_scr[:, sl]).astype(BF16)
            g = lax.dot_general(kw, vp, tn, preferred_element_type=F32)
            sf_scr[p] = sf * cdf[:, sl] + jnp.where(bd, g, 0.0)
        return carry

    lax.fori_loop(0, nc, fwd, 0)


def _retention(q, k, v, gate, stf, stb, a_f, a_b, a_f_lane, a_b_lane, gain):
    b, n, _ = q.shape
    nc = n // RET_CHUNK
    seq = lambda: pl.BlockSpec((1, n, RET_WIDTH), lambda i, *_: (i, 0, 0))
    st = lambda: pl.BlockSpec((1, N_PAIRS, LANES, LANES), lambda i, *_: (i, 0, 0, 0))
    lane = lambda: pl.BlockSpec((1, RET_WIDTH), lambda i, *_: (0, 0))
    grid_spec = pltpu.PrefetchScalarGridSpec(
        num_scalar_prefetch=2,
        grid=(b,),
        in_specs=[seq(), seq(), seq(), seq(), st(), st(), lane(), lane(), lane()],
        out_specs=seq(),
        scratch_shapes=[
            pltpu.VMEM((RET_HEADS, RET_CHUNK, RET_CHUNK), F32),
            pltpu.VMEM((RET_CHUNK, RET_WIDTH), F32),
            pltpu.VMEM((RET_CHUNK, RET_WIDTH), F32),
            pltpu.VMEM((RET_CHUNK, RET_WIDTH), F32),
            pltpu.VMEM((RET_CHUNK, RET_WIDTH), F32),
            pltpu.VMEM((N_PAIRS, LANES, LANES), F32),
            pltpu.VMEM((N_PAIRS, LANES, LANES), F32),
            pltpu.VMEM((nc, N_PAIRS, LANES, LANES), BF16),
        ],
    )
    return pl.pallas_call(
        _ret_kernel,
        out_shape=jax.ShapeDtypeStruct((b, n, RET_WIDTH), BF16),
        grid_spec=grid_spec,
        compiler_params=_cparams(("arbitrary",)),
        name="retention",
    )(a_f, a_b, q, k, v, gate, stf, stb, a_f_lane, a_b_lane, gain)


def _four_kernel(f_ref, bc_ref, bs_ref, cs_ref, o_ref, z_scr):
    n = f_ref.shape[1]

    @pl.when(pl.program_id(1) == 0)
    def _():
        f = f_ref[0]
        z_scr[0:n, :] = jnp.dot(f, bc_ref[...], preferred_element_type=F32).astype(BF16)
        z_scr[n:2 * n, :] = jnp.dot(f, bs_ref[...], preferred_element_type=F32).astype(BF16)

    o_ref[0] = jnp.dot(cs_ref[...], z_scr[...], preferred_element_type=F32).astype(BF16)


def _fourier(f, bc, bs, cs, tn):
    b, n, _ = f.shape
    return pl.pallas_call(
        _four_kernel,
        out_shape=jax.ShapeDtypeStruct((b, n, FNET_WIDTH), BF16),
        grid=(b, n // tn),
        in_specs=[
            pl.BlockSpec((1, n, FNET_WIDTH), lambda i, j: (i, 0, 0)),
            pl.BlockSpec((FNET_WIDTH, FNET_WIDTH), lambda i, j: (0, 0)),
            pl.BlockSpec((FNET_WIDTH, FNET_WIDTH), lambda i, j: (0, 0)),
            pl.BlockSpec((tn, 2 * n), lambda i, j: (j, 0)),
        ],
        out_specs=pl.BlockSpec((1, tn, FNET_WIDTH), lambda i, j: (i, j, 0)),
        scratch_shapes=[pltpu.VMEM((2 * n, FNET_WIDTH), BF16)],
        compiler_params=_cparams(("arbitrary", "arbitrary")),
        name="fourier",
    )(f, bc, bs, cs)


def _outp_kernel(ret_ref, four_ref, x_ref, g1_ref, sh2_ref, sc2_ref, gpost_ref, gpre_ref,
                 wo_ref, wq_ref, x1_ref, h2_ref, qp_ref):
    mix = jnp.dot(ret_ref[0], wo_ref[0:RET_WIDTH, :], preferred_element_type=F32)
    mix += jnp.dot(four_ref[0], wo_ref[RET_WIDTH:, :], preferred_element_type=F32)
    x1 = x_ref[0] + g1_ref[0] * _rms_rows(mix, gpost_ref[...])
    x1_ref[0] = x1
    h2 = _rms_rows(x1, gpre_ref[...]) * (1.0 + sc2_ref[0]) + sh2_ref[0]
    h2_ref[0] = h2
    qp_ref[0] = jnp.dot(h2.astype(BF16), wq_ref[...], preferred_element_type=F32).astype(BF16)


def _outproj(ret, four, x, g1, sh2, sc2, gpost, gpre, w_out, w_q, tm):
    b, n, _ = x.shape
    qw = w_q.shape[1]
    mod = lambda: pl.BlockSpec((1, 1, D_MODEL), lambda i, j: (i, 0, 0))
    vec = lambda: pl.BlockSpec((1, D_MODEL), lambda i, j: (0, 0))
    half = lambda: pl.BlockSpec((1, tm, 512), lambda i, j: (i, j, 0))
    full = lambda: pl.BlockSpec((1, tm, D_MODEL), lambda i, j: (i, j, 0))
    return pl.pallas_call(
        _outp_kernel,
        out_shape=(jax.ShapeDtypeStruct((b, n, D_MODEL), F32),
                   jax.ShapeDtypeStruct((b, n, D_MODEL), F32),
                   jax.ShapeDtypeStruct((b, n, qw), BF16)),
        grid=(b, n // tm),
        in_specs=[half(), half(), full(), mod(), mod(), mod(), vec(), vec(),
                  pl.BlockSpec(w_out.shape, lambda i, j: (0, 0)),
                  pl.BlockSpec(w_q.shape, lambda i, j: (0, 0))],
        out_specs=(full(), full(), pl.BlockSpec((1, tm, qw), lambda i, j: (i, j, 0))),
        compiler_params=_cparams(("arbitrary", "arbitrary")),
        name="outproj",
    )(ret, four, x, g1, sh2, sc2, gpost, gpre, w_out, w_q)


def _top_rows(s, kk, rid=None):
    rows, tq = s.shape
    if rid is None:
        rid = lax.broadcasted_iota(I32, (rows, tq), 0).astype(F32)
    slot = lax.broadcasted_iota(I32, (kk, tq), 0)
    vals = jnp.zeros((kk, tq), F32)
    idxs = jnp.zeros((kk, tq), F32)
    for it in range(kk):
        m = jnp.max(s, axis=0, keepdims=True)
        am = jnp.min(jnp.where(s == m, rid, float(2 ** 20)), axis=0, keepdims=True)
        vals = jnp.where(slot == it, m, vals)
        idxs = jnp.where(slot == it, am, idxs)
        s = jnp.where(rid == am, -jnp.inf, s)
    return vals, idxs


def _pick_rows(table, sel, kk):
    out = jnp.zeros_like(table)
    for i in range(kk):
        out = jnp.where(sel == float(i), table[i:i + 1, :], out)
    return out


def _pair_candidates(v1, v2):
    kk = PEER_TOPK
    tq = v1.shape[1]
    sub = lax.broadcasted_iota(I32, (SUBLANES, tq), 0)
    subf = sub.astype(F32)
    upper = sub >= 4
    vals = [v1[0:1] + v2[0:8], v1[0:1] + v2[8:16], v1[1:2] + v2[0:8], v1[2:3] + v2[0:8], v1[3:4] + v2[0:8]]
    ids = [subf, subf + 8.0, subf + float(kk), subf + float(2 * kk), subf + float(3 * kk)]
    for j in range(3):
        vals.append(jnp.where(upper, v1[0:8] + v2[j:j + 1], -jnp.inf))
        ids.append(subf * float(kk) + float(j))
    vals.append(v1[8:16] + v2[0:1])
    ids.append((subf + 8.0) * float(kk))
    return jnp.concatenate(vals, axis=0), jnp.concatenate(ids, axis=0)


def _topk_kernel(qp_ref, k1_ref, k2_ref, off_ref, wts_ref):
    kk = PEER_TOPK
    half = PEER_D_KEY // 2
    nt = (((1,), (1,)), ((), ()))
    k1 = k1_ref[...]
    k2 = k2_ref[...]
    ids_all, w_all = [], []
    for hh in range(PEER_HEADS):
        q1 = qp_ref[:, hh * PEER_D_KEY: hh * PEER_D_KEY + half]
        q2 = qp_ref[:, hh * PEER_D_KEY + half: (hh + 1) * PEER_D_KEY]
        s1 = lax.dot_general(k1, q1, nt, preferred_element_type=F32)
        s2 = lax.dot_general(k2, q2, nt, preferred_element_type=F32)
        v1, i1 = _top_rows(s1, kk)
        v2, i2 = _top_rows(s2, kk)
        cand, flat = _pair_candidates(v1, v2)
        tv, ti = _top_rows(cand, kk, flat)
        hi = jnp.floor(ti * (1.0 / kk))
        lo = ti - hi * kk
        ids = _pick_rows(i1, hi, kk) * float(PEER_N_KEYS) + _pick_rows(i2, lo, kk)
        p = jnp.exp(tv - tv[0:1, :])
        w = p / jnp.sum(p, axis=0, keepdims=True)
        ids_all.append(ids)
        w_all.append(w)
    ids_t = jnp.concatenate(ids_all, axis=0).T.astype(I32)
    slot = lax.broadcasted_iota(I32, ids_t.shape, 1)
    off_ref[...] = ids_t * TABLE_ROWS_PER_EXPERT + jnp.where((slot % SUBLANES) < 4, TABLE_FRONT_PAD, 0)
    wts_ref[...] = jnp.concatenate(w_all, axis=0).T


def _topk(qp, k1, k2):
    t, qw = qp.shape
    tq = LANES
    out = lambda: pl.BlockSpec((tq, PEER_SLOTS), lambda i: (i, 0))
    key = lambda: pl.BlockSpec((PEER_N_KEYS, PEER_D_KEY // 2), lambda i: (0, 0))
    return pl.pallas_call(
        _topk_kernel,
        out_shape=(jax.ShapeDtypeStruct((t, PEER_SLOTS), I32),
                   jax.ShapeDtypeStruct((t, PEER_SLOTS), F32)),
        grid=(t // tq,),
        in_specs=[pl.BlockSpec((tq, qw), lambda i: (i, 0)), key(), key()],
        out_specs=(out(), out()),
        compiler_params=_cparams(("arbitrary",)),
        name="peer_topk",
    )(qp, k1, k2)


PEER_SLOTS = PEER_HEADS * PEER_TOPK
PEER_TOKENS_PER_STEP = 64
PEER_TOKENS_PER_HALF = PEER_TOKENS_PER_STEP // 2
TABLE_ROWS_PER_EXPERT = 4
TABLE_FRONT_PAD = 4
TABLE_BACK_PAD = 8
HI_MASK = np.uint32(0xFFFF0000)


def _pack_table(w):
    e = w.shape[0]
    bits = lax.bitcast_convert_type(w.astype(BF16), jnp.uint16).astype(U32)
    bits = bits.reshape(e, 2, TABLE_ROWS_PER_EXPERT, LANES)
    words = (bits[:, 0] | (bits[:, 1] << 16)).reshape(e * TABLE_ROWS_PER_EXPERT, LANES)
    return jnp.pad(words, ((TABLE_FRONT_PAD, TABLE_BACK_PAD), (0, 0)))


def _expert_pair(tbl_ref, off_s, j, g, i, low4):
    wa = tbl_ref[pl.ds(off_s[j, g * SUBLANES + i], SUBLANES), :]
    wb = tbl_ref[pl.ds(off_s[j, g * SUBLANES + i + 4], SUBLANES), :]
    return jnp.where(low4, wa, wb)


def _for_token_halves(off_hbm, bufs, sem, half_body):
    step = pl.program_id(0)
    last = pl.num_programs(0) - 1

    def fill(at_step, half):
        start = at_step * PEER_TOKENS_PER_STEP + half * PEER_TOKENS_PER_HALF
        return pltpu.make_async_copy(off_hbm.at[pl.ds(start, PEER_TOKENS_PER_HALF)], bufs[half], sem.at[half])

    @pl.when(step == 0)
    def _():
        for half in range(2):
            fill(0, half).start()

    for half in range(2):
        fill(step, half).wait()
        half_body(half, bufs[half])

        @pl.when(step < last)
        def _():
            fill(step + 1, half).start()


def _offset_scratch():
    return [pltpu.SMEM((PEER_TOKENS_PER_HALF, PEER_SLOTS), I32),
            pltpu.SMEM((PEER_TOKENS_PER_HALF, PEER_SLOTS), I32),
            pltpu.SemaphoreType.DMA((2,))]


def _gelu_exact(x):
    return 0.5 * x * (1.0 + lax.erf(x * (2.0 ** -0.5)))


def _peer_u_slot_sum():
    col = np.arange(PEER_SLOTS * TABLE_ROWS_PER_EXPERT)
    q, r = col // SUBLANES, col % SUBLANES
    slot = SUBLANES * (q // 4) + (q % 4) + np.where(r < 4, 0, 4)
    return jnp.asarray((slot[:, None] == np.arange(PEER_SLOTS)[None, :]).astype(np.float32), BF16)


def _split_bf16(x):
    hi = x.astype(BF16)
    return hi, (x - hi.astype(F32)).astype(BF16)


def _peer_u_kernel(off_hbm, h_ref, wts_ref, slotsum_ref, tbl_ref, coef_ref, off_a, off_b, sem):
    nt = (((1,), (1,)), ((), ()))
    ones = jnp.ones((SUBLANES, LANES), BF16)
    sub = lax.broadcasted_iota(I32, (SUBLANES, LANES), 0)
    low4 = sub < 4
    r4 = sub % TABLE_ROWS_PER_EXPERT

    def token_row_sums(off_s, jj, j):
        def chunk(c):
            return jnp.broadcast_to(h_ref[j:j + 1, c * LANES:(c + 1) * LANES], (SUBLANES, LANES))

        def by_row(c0):
            return jnp.where(r4 == 0, chunk(c0), jnp.where(r4 == 1, chunk(c0 + 1),
                             jnp.where(r4 == 2, chunk(c0 + 2), chunk(c0 + 3))))

        hlo = by_row(0)
        hhi = by_row(TABLE_ROWS_PER_EXPERT)
        prods = []
        for q in range(PEER_SLOTS // 2):
            w = _expert_pair(tbl_ref, off_s, jj, q // 4, q % 4, low4)
            lo = lax.bitcast_convert_type(w << 16, F32)
            hi = lax.bitcast_convert_type(w & HI_MASK, F32)
            prods.append(lo * hlo + hi * hhi)
        stacked = jnp.concatenate(prods, axis=0).astype(BF16)
        return lax.dot_general(ones, stacked, nt, preferred_element_type=F32)[0:1, :]

    def finish(half, sums):
        hi, lo = _split_bf16(sums)
        acts = (jnp.dot(hi, slotsum_ref[...], preferred_element_type=F32)
                + jnp.dot(lo, slotsum_ref[...], preferred_element_type=F32))
        rows = pl.ds(half * PEER_TOKENS_PER_HALF, PEER_TOKENS_PER_HALF)
        coef_ref[rows, :] = wts_ref[rows, :] * _gelu_exact(acts)

    pending = []

    def half_body(half, off_s):
        while pending:
            finish(*pending.pop())
        base = half * PEER_TOKENS_PER_HALF
        sums = [token_row_sums(off_s, jj, base + jj) for jj in range(PEER_TOKENS_PER_HALF)]
        pending.append((half, jnp.concatenate(sums, axis=0)))

    _for_token_halves(off_hbm, (off_a, off_b), sem, half_body)
    finish(*pending.pop())


def _peer_u(off, h, wts, tbl):
    t = h.shape[0]
    tb = PEER_TOKENS_PER_STEP
    slotsum = _peer_u_slot_sum()
    return pl.pallas_call(
        _peer_u_kernel,
        out_shape=jax.ShapeDtypeStruct((t, PEER_SLOTS), F32),
        grid=(t // tb,),
        in_specs=[pl.BlockSpec(memory_space=pl.ANY),
                  pl.BlockSpec((tb, D_MODEL), lambda i: (i, 0)),
                  pl.BlockSpec((tb, PEER_SLOTS), lambda i: (i, 0)),
                  pl.BlockSpec(slotsum.shape, lambda i: (0, 0)),
                  pl.BlockSpec(memory_space=pltpu.VMEM)],
        out_specs=pl.BlockSpec((tb, PEER_SLOTS), lambda i: (i, 0)),
        scratch_shapes=_offset_scratch(),
        compiler_params=_cparams(("arbitrary",)),
        name="peer_u",
    )(off, h, wts, slotsum, tbl)


def _peer_v_tables():
    col = np.arange(PEER_SLOTS * SUBLANES)
    q, r = col // 16, col % 16
    slot = SUBLANES * (q // 4) + (q % 4) + np.where(r < 8, 0, 4)
    expand = (slot[None, :] == np.arange(PEER_SLOTS)[:, None]).astype(np.float32)
    out_row = (r // 2) % 4 + 4 * (r % 2)
    rowsel = (out_row[None, :] == np.arange(SUBLANES)[:, None]).astype(np.float32)
    return jnp.asarray(expand, BF16), jnp.asarray(rowsel, F32)


def _peer_v_kernel(off_hbm, coef_ref, x1_ref, g2_ref, gain_ref, expand_ref, rowsel_ref, tbl_ref, o_ref,
                   out_scr, off_a, off_b, sem):
    sub = lax.broadcasted_iota(I32, (SUBLANES, LANES), 0)
    low4 = sub < 4
    cexp = jnp.dot(coef_ref[...].astype(BF16), expand_ref[...], preferred_element_type=F32)
    rowsel = rowsel_ref[...]

    def half_body(half, off_s):
        for jj in range(PEER_TOKENS_PER_HALF):
            j = half * PEER_TOKENS_PER_HALF + jj
            tiles = []
            for q in range(PEER_SLOTS // 2):
                w = _expert_pair(tbl_ref, off_s, jj, q // 4, q % 4, low4)
                tiles.append(pltpu.bitcast(w, BF16))
            wmat = jnp.concatenate(tiles, axis=0)
            c = (cexp[j:j + 1, :] * rowsel).astype(BF16)
            out = jnp.dot(c, wmat, preferred_element_type=F32)
            for s in range(SUBLANES):
                out_scr[j:j + 1, s * LANES:(s + 1) * LANES] = out[s:s + 1, :]

    _for_token_halves(off_hbm, (off_a, off_b), sem, half_body)
    y = _rms_rows(out_scr[...], gain_ref[...])
    o_ref[...] = x1_ref[...] + g2_ref[0] * y


def _peer_v(off, coef, x1, g2, gain, tbl, tokens_per_sample):
    t = x1.shape[0]
    tb = PEER_TOKENS_PER_STEP
    steps_per_sample = tokens_per_sample // tb
    expand, rowsel = _peer_v_tables()
    tok = lambda: pl.BlockSpec((tb, D_MODEL), lambda i: (i, 0))
    return pl.pallas_call(
        _peer_v_kernel,
        out_shape=jax.ShapeDtypeStruct((t, D_MODEL), F32),
        grid=(t // tb,),
        in_specs=[pl.BlockSpec(memory_space=pl.ANY),
                  pl.BlockSpec((tb, PEER_SLOTS), lambda i: (i, 0)),
                  tok(),
                  pl.BlockSpec((1, 1, D_MODEL), lambda i: (i // steps_per_sample, 0, 0)),
                  pl.BlockSpec((1, D_MODEL), lambda i: (0, 0)),
                  pl.BlockSpec(expand.shape, lambda i: (0, 0)),
                  pl.BlockSpec(rowsel.shape, lambda i: (0, 0)),
                  pl.BlockSpec(memory_space=pltpu.VMEM)],
        out_specs=tok(),
        scratch_shapes=[pltpu.VMEM((tb, D_MODEL), F32)] + _offset_scratch(),
        compiler_params=_cparams(("arbitrary",)),
        name="peer_v",
    )(off, coef, x1, g2, gain, expand, rowsel, tbl)


@functools.lru_cache(maxsize=None)
def _rope_tables_np(n):
    rows = n // GRID_W
    row_ids = np.repeat(np.arange(rows, dtype=np.float64), GRID_W)
    col_ids = np.tile(np.arange(GRID_W, dtype=np.float64), rows)
    freqs = ROPE_BASE ** (-np.arange(ROPE_PAIRS, dtype=np.float64) / ROPE_PAIRS)
    ang = np.concatenate([row_ids[:, None] * freqs[None, :], col_ids[:, None] * freqs[None, :]], axis=-1)
    cos, sin = np.cos(ang), np.sin(ang)
    cos_h = np.concatenate([cos, cos], axis=-1)
    sin_h = np.concatenate([-sin, sin], axis=-1)
    return (np.tile(cos_h, (1, RET_HEADS)).astype(np.float32),
            np.tile(sin_h, (1, RET_HEADS)).astype(np.float32))


@functools.lru_cache(maxsize=None)
def _dft_tables_np(n):
    scale = (n * FNET_GROUP_DIM) ** -0.5
    jn = np.arange(n, dtype=np.int64)
    ang_n = ((jn[:, None] * jn[None, :]) % n).astype(np.float64) * (2.0 * np.pi / n)
    cs = np.concatenate([np.cos(ang_n), -np.sin(ang_n)], axis=1).astype(np.float32)
    jc = np.arange(FNET_GROUP_DIM, dtype=np.int64)
    ang_c = ((jc[:, None] * jc[None, :]) % FNET_GROUP_DIM).astype(np.float64) * (2.0 * np.pi / FNET_GROUP_DIM)
    eye = np.eye(FNET_GROUPS)
    bc = np.kron(eye, np.cos(ang_c) * scale).astype(np.float32)
    bs = np.kron(eye, np.sin(ang_c) * scale).astype(np.float32)
    return bc, bs, cs


def kernel(x, c, ctx, c_ctx, w_ada, b_ada, norm_pre_mix, norm_post_mix, w_in, ret_decay_fwd, ret_decay_bwd, ret_norm_gain, w_out, norm_pre_ffn, norm_post_ffn, peer_w_query, peer_sub_keys_1, peer_sub_keys_2, peer_u, peer_v):
    b, n, d = x.shape
    depth = w_ada.shape[0]
    assert depth == 1 and d == D_MODEL and n % RET_CHUNK == 0 and n % GRID_W == 0
    t = b * n
    tm = min(256, n)
    row = lambda a: a.reshape(1, -1)

    cos, sin = [jnp.asarray(a) for a in _rope_tables_np(n)]
    bc, bs, cs = [jnp.asarray(a).astype(BF16) for a in _dft_tables_np(n)]

    pad = (-(b + 1)) % SUBLANES
    c_all = jnp.concatenate([c, c_ctx[None, :], jnp.zeros((pad, d), F32)], axis=0)
    mod = _ada(c_all, w_ada[0], row(b_ada[0]))
    sh1, sc1, g1, sh2, sc2, g2 = [mod[:b, i * d:(i + 1) * d].reshape(b, 1, d) for i in range(N_MOD)]
    csh1 = mod[b:b + 1, 0:d]
    csc1 = mod[b:b + 1, d:2 * d]

    a_f, a_b = ret_decay_fwd[0], ret_decay_bwd[0]
    a_f_lane = row(jnp.repeat(a_f, RET_DK))
    a_b_lane = row(jnp.repeat(a_b, RET_DK))

    w_in_b = w_in[0].astype(BF16)
    stf, stb = _ctx_states(ctx, row(norm_pre_mix[0]), csh1, csc1,
                           w_in_b[:, QK_WIDTH:QK_WIDTH + 2 * QK_WIDTH], a_f_lane, a_b_lane)

    q, k, v, gate, f = _inproj(x, sh1, sc1, row(norm_pre_mix[0]), w_in_b, cos, sin, tm)
    ret = _retention(q, k, v, gate, stf, stb, a_f, a_b, a_f_lane, a_b_lane, row(ret_norm_gain[0]))
    four = _fourier(f, bc, bs, cs, tm)

    x1, h2, qp = _outproj(ret, four, x, g1, sh2, sc2, row(norm_post_mix[0]), row(norm_pre_ffn[0]),
                          w_out[0].astype(BF16), peer_w_query[0].astype(BF16), tm)

    off, wts = _topk(qp.reshape(t, -1), peer_sub_keys_1[0].astype(BF16), peer_sub_keys_2[0].astype(BF16))

    coef = _peer_u(off, h2.reshape(t, d), wts, _pack_table(peer_u[0]))
    out = _peer_v(off, coef, x1.reshape(t, d), g2, row(norm_post_ffn[0]), _pack_table(peer_v[0]), n)
    return out.reshape(b, n, d)
```

```python
import functools

import numpy as np
import jax
import jax.numpy as jnp
from jax import lax
from jax.experimental import pallas as pl
from jax.experimental.pallas import tpu as pltpu

F32 = jnp.float32
BF16 = jnp.bfloat16
I32 = jnp.int32
U32 = jnp.uint32

D_MODEL = 1024
GRID_W = 64
RET_HEADS = 8
RET_DK = 64
RET_WIDTH = 512
QK_WIDTH = 512
RET_CHUNK = 128
ROPE_PAIRS = RET_DK // 4
ROPE_BASE = 10000.0
FNET_GROUPS = 4
FNET_GROUP_DIM = 128
FNET_WIDTH = 512
IN_WIDTH = 2560
PEER_HEADS = 8
PEER_D_KEY = 256
PEER_N_KEYS = 128
PEER_TOPK = 16
N_MOD = 6
EPS = 1e-6

LANES = 128
SUBLANES = 8
VMEM_LIMIT = 56 * 1024 * 1024
N_PAIRS = RET_HEADS // 2


def _cparams(sem):
    return pltpu.CompilerParams(dimension_semantics=sem, vmem_limit_bytes=VMEM_LIMIT)


def _rms_rows(x, gain):
    ms = jnp.mean(x * x, axis=-1, keepdims=True)
    return x * lax.rsqrt(ms + EPS) * gain


def _silu(x):
    return x * jax.nn.sigmoid(x)


def _log_gamma(a):
    return jnp.log1p(-jnp.exp(a))


def _ada_kernel(c_ref, w_ref, b_ref, o_ref):
    s = _silu(c_ref[...])
    o_ref[...] = jnp.dot(s, w_ref[...], preferred_element_type=F32) + b_ref[...]


def _ada(c_all, w_ada, b_ada):
    rows = c_all.shape[0]
    width = w_ada.shape[1]
    tn = 1536
    return pl.pallas_call(
        _ada_kernel,
        out_shape=jax.ShapeDtypeStruct((rows, width), F32),
        grid=(width // tn,),
        in_specs=[
            pl.BlockSpec((rows, D_MODEL), lambda j: (0, 0)),
            pl.BlockSpec((D_MODEL, tn), lambda j: (0, j)),
            pl.BlockSpec((1, tn), lambda j: (0, j)),
        ],
        out_specs=pl.BlockSpec((rows, tn), lambda j: (0, j)),
        compiler_params=_cparams(("arbitrary",)),
        name="ada",
    )(c_all, w_ada, b_ada)


def _ctx_kernel(ctx_ref, g_ref, sh_ref, sc_ref, w_ref, af_ref, ab_ref, stf_ref, stb_ref):
    x = ctx_ref[0]
    n = x.shape[0]
    h = (_rms_rows(x, g_ref[...]) * (1.0 + sc_ref[...]) + sh_ref[...]).astype(BF16)
    kv = jnp.dot(h, w_ref[...], preferred_element_type=F32)
    k = kv[:, :QK_WIDTH] * (RET_DK ** -0.5)
    v = kv[:, QK_WIDTH:].astype(BF16)
    pos = lax.broadcasted_iota(I32, (n, QK_WIDTH), 0).astype(F32)
    lgf = _log_gamma(af_ref[...])
    lgb = _log_gamma(ab_ref[...])
    kf = (k * jnp.exp(lgf * (n - 1.0 - pos))).astype(BF16)
    kb = (k * jnp.exp(lgb * pos)).astype(BF16)
    tn = (((0,), (0,)), ((), ()))
    gf = lax.dot_general(kf, v, tn, preferred_element_type=F32)
    gb = lax.dot_general(kb, v, tn, preferred_element_type=F32)
    ri = lax.broadcasted_iota(I32, (LANES, LANES), 0) // RET_DK
    ci = lax.broadcasted_iota(I32, (LANES, LANES), 1) // RET_DK
    bd = ri == ci
    for p in range(N_PAIRS):
        sl = slice(p * LANES, (p + 1) * LANES)
        stf_ref[0, p] = jnp.where(bd, gf[sl, sl], 0.0)
        stb_ref[0, p] = jnp.where(bd, gb[sl, sl], 0.0)


def _ctx_states(ctx, gain, csh, csc, w_kv, a_f, a_b):
    b, n, _ = ctx.shape
    vec = lambda: pl.BlockSpec((1, D_MODEL), lambda i: (0, 0))
    lane = lambda: pl.BlockSpec((1, QK_WIDTH), lambda i: (0, 0))
    st = jax.ShapeDtypeStruct((b, N_PAIRS, LANES, LANES), F32)
    st_spec = lambda: pl.BlockSpec((1, N_PAIRS, LANES, LANES), lambda i: (i, 0, 0, 0))
    return pl.pallas_call(
        _ctx_kernel,
        out_shape=(st, st),
        grid=(b,),
        in_specs=[
            pl.BlockSpec((1, n, D_MODEL), lambda i: (i, 0, 0)),
            vec(), vec(), vec(),
            pl.BlockSpec((D_MODEL, 2 * QK_WIDTH), lambda i: (0, 0)),
            lane(), lane(),
        ],
        out_specs=(st_spec(), st_spec()),
        compiler_params=_cparams(("arbitrary",)),
        name="ctx_states",
    )(ctx, gain, csh, csc, w_kv, a_f, a_b)


def _inproj_kernel(x_ref, sh_ref, sc_ref, g_ref, w_ref, cos_ref, sin_ref,
                   q_ref, k_ref, v_ref, gate_ref, f_ref):
    x = x_ref[0]
    tm = x.shape[0]
    h = (_rms_rows(x, g_ref[...]) * (1.0 + sc_ref[0]) + sh_ref[0]).astype(BF16)

    def proj(lo, hi):
        return jnp.dot(h, w_ref[:, lo:hi], preferred_element_type=F32)

    cos = cos_ref[...]
    sin = sin_ref[...]
    lane = lax.broadcasted_iota(I32, (tm, QK_WIDTH), 1)
    first = (lane % RET_DK) < (RET_DK // 2)

    def rope(t):
        swapped = jnp.where(first,
                            pltpu.roll(t, QK_WIDTH - RET_DK // 2, 1),
                            pltpu.roll(t, RET_DK // 2, 1))
        return t * cos + swapped * sin

    q_ref[0] = rope(proj(0, 512)).astype(BF16)
    k_ref[0] = rope(proj(512, 1024) * (RET_DK ** -0.5)).astype(BF16)
    v_ref[0] = proj(1024, 1536).astype(BF16)
    gate_ref[0] = proj(1536, 2048)
    f_ref[0] = proj(2048, 2560).astype(BF16)


def _inproj(x, sh, sc, gain, w_in, cos, sin, tm):
    b, n, _ = x.shape
    mod = lambda: pl.BlockSpec((1, 1, D_MODEL), lambda i, j: (i, 0, 0))
    tab = lambda: pl.BlockSpec((tm, QK_WIDTH), lambda i, j: (j, 0))
    out = lambda: pl.BlockSpec((1, tm, 512), lambda i, j: (i, j, 0))
    sd = lambda dt: jax.ShapeDtypeStruct((b, n, 512), dt)
    return pl.pallas_call(
        _inproj_kernel,
        out_shape=(sd(BF16), sd(BF16), sd(BF16), sd(F32), sd(BF16)),
        grid=(b, n // tm),
        in_specs=[
            pl.BlockSpec((1, tm, D_MODEL), lambda i, j: (i, j, 0)),
            mod(), mod(),
            pl.BlockSpec((1, D_MODEL), lambda i, j: (0, 0)),
            pl.BlockSpec((D_MODEL, IN_WIDTH), lambda i, j: (0, 0)),
            tab(), tab(),
        ],
        out_specs=(out(), out(), out(), out(), out()),
        compiler_params=_cparams(("arbitrary", "arbitrary")),
        name="inproj",
    )(x, sh, sc, gain, w_in, cos, sin)


def _ret_kernel(af_s, ab_s, q_ref, k_ref, v_ref, gate_ref, stf_ref, stb_ref, af_ref, ab_ref,
                gain_ref, o_ref, mask_scr, qdf_scr, qdb_scr, kdf_scr, kdb_scr, sf_scr, sb_scr,
                sb_store):
    c = RET_CHUNK
    n = q_ref.shape[1]
    nc = n // c
    lgf = _log_gamma(af_ref[...])
    lgb = _log_gamma(ab_ref[...])
    cdf = jnp.exp(lgf * float(c))
    cdb = jnp.exp(lgb * float(c))

    @pl.when(pl.program_id(0) == 0)
    def _():
        r = lax.broadcasted_iota(I32, (c, RET_WIDTH), 0).astype(F32)
        qdf_scr[...] = jnp.exp(lgf * (r + 1.0))
        kdf_scr[...] = jnp.exp(lgf * (c - 1.0 - r))
        qdb_scr[...] = jnp.exp(lgb * (c - r))
        kdb_scr[...] = jnp.exp(lgb * r)
        ri = lax.broadcasted_iota(I32, (c, c), 0)
        ci = lax.broadcasted_iota(I32, (c, c), 1)
        d = (ri - ci).astype(F32)
        for hh in range(RET_HEADS):
            gf = _log_gamma(jnp.full((c, c), af_s[hh], F32))
            gb = _log_gamma(jnp.full((c, c), ab_s[hh], F32))
            mask_scr[hh] = jnp.where(d >= 0.0, jnp.exp(gf * jnp.maximum(d, 0.0)),
                                     jnp.exp(gb * jnp.maximum(-d, 0.0)))

    ri = lax.broadcasted_iota(I32, (LANES, LANES), 0) // RET_DK
    ci = lax.broadcasted_iota(I32, (LANES, LANES), 1) // RET_DK
    bd = ri == ci
    bd_ones = jnp.where(bd, 1.0, 0.0).astype(BF16)
    lane_head = lax.broadcasted_iota(I32, (c, LANES), 1) // RET_DK
    tn = (((0,), (0,)), ((), ()))
    nt = (((1,), (1,)), ((), ()))

    for p in range(N_PAIRS):
        sf_scr[p] = stf_ref[0, p]
        sb_scr[p] = stb_ref[0, p]

    def bwd_states(i, carry):
        cc = nc - 1 - i
        rows = pl.ds(pl.multiple_of(cc * c, c), c)
        for p in range(N_PAIRS):
            sl = slice(p * LANES, (p + 1) * LANES)
            sb = sb_scr[p]
            sb_store[cc, p] = sb.astype(BF16)
            kw = (k_ref[0, rows, sl].astype(F32) * kdb_scr[:, sl]).astype(BF16)
            g = lax.dot_general(kw, v_ref[0, rows, sl], tn, preferred_element_type=F32)
            sb_scr[p] = sb * cdb[:, sl] + jnp.where(bd, g, 0.0)
        return carry

    lax.fori_loop(0, nc, bwd_states, 0)

    def fwd(cc, carry):
        rows = pl.ds(pl.multiple_of(cc * c, c), c)
        outs, states = [], []
        for p in range(N_PAIRS):
            sl = slice(p * LANES, (p + 1) * LANES)
            qp = q_ref[0, rows, sl]
            kp = k_ref[0, rows, sl]
            vp = v_ref[0, rows, sl]
            qf = qp.astype(F32)
            sf = sf_scr[p]
            q_dec = jnp.concatenate([(qf * qdf_scr[:, sl]).astype(BF16), (qf * qdb_scr[:, sl]).astype(BF16)], axis=1)
            s_both = jnp.concatenate([sf.astype(BF16), sb_store[cc, p]], axis=0)
            acc = jnp.dot(q_dec, s_both, preferred_element_type=F32)
            zero = jnp.zeros_like(qp)
            q_heads = jnp.concatenate([jnp.where(lane_head == j, qp, zero) for j in range(2)], axis=0)
            s = lax.dot_general(q_heads, kp, nt, preferred_element_type=F32)
            pm = jnp.concatenate([(s[j * c:(j + 1) * c] * mask_scr[2 * p + j]).astype(BF16) for j in range(2)], axis=1)
            v_heads = jnp.concatenate([jnp.where(lane_head == j, vp, zero) for j in range(2)], axis=0)
            acc += jnp.dot(pm, v_heads, preferred_element_type=F32)
            sq_hi, sq_lo = _split_bf16(acc * acc)
            ms = jnp.dot(jnp.concatenate([sq_hi, sq_lo], axis=1), jnp.concatenate([bd_ones, bd_ones], axis=0),
                         preferred_element_type=F32) * (1.0 / RET_DK)
            y = acc * lax.rsqrt(ms + EPS) * gain_ref[:, sl] * _silu(gate_ref[0, rows, sl])
            outs.append(y.astype(BF16))
            kw = (kp.astype(F32) * kdf_scr[:, sl]).astype(BF16)
            g = lax.dot_general(kw, vp, tn, preferred_element_type=F32)
            states.append(sf * cdf[:, sl] + jnp.where(bd, g, 0.0))
        for p in range(N_PAIRS):
            o_ref[0, rows, p * LANES:(p + 1) * LANES] = outs[p]
            sf_scr[p] = states[p]
        return carry

    lax.fori_loop(0, nc, fwd, 0)


def _retention(q, k, v, gate, stf, stb, a_f, a_b, a_f_lane, a_b_lane, gain):
    b, n, _ = q.shape
    nc = n // RET_CHUNK
    seq = lambda: pl.BlockSpec((1, n, RET_WIDTH), lambda i, *_: (i, 0, 0))
    st = lambda: pl.BlockSpec((1, N_PAIRS, LANES, LANES), lambda i, *_: (i, 0, 0, 0))
    lane = lambda: pl.BlockSpec((1, RET_WIDTH), lambda i, *_: (0, 0))
    grid_spec = pltpu.PrefetchScalarGridSpec(
        num_scalar_prefetch=2,
        grid=(b,),
        in_specs=[seq(), seq(), seq(), seq(), st(), st(), lane(), lane(), lane()],
        out_specs=seq(),
        scratch_shapes=[
            pltpu.VMEM((RET_HEADS, RET_CHUNK, RET_CHUNK), F32),
            pltpu.VMEM((RET_CHUNK, RET_WIDTH), F32),
            pltpu.VMEM((RET_CHUNK, RET_WIDTH), F32),
            pltpu.VMEM((RET_CHUNK, RET_WIDTH), F32),
            pltpu.VMEM((RET_CHUNK, RET_WIDTH), F32),
            pltpu.VMEM((N_PAIRS, LANES, LANES), F32),
            pltpu.VMEM((N_PAIRS, LANES, LANES), F32),
            pltpu.VMEM((nc, N_PAIRS, LANES, LANES), BF16),
        ],
    )
    return pl.pallas_call(
        _ret_kernel,
        out_shape=jax.ShapeDtypeStruct((b, n, RET_WIDTH), BF16),
        grid_spec=grid_spec,
        compiler_params=_cparams(("arbitrary",)),
        name="retention",
    )(a_f, a_b, q, k, v, gate, stf, stb, a_f_lane, a_b_lane, gain)


def _four_kernel(f_ref, bc_ref, bs_ref, cs_ref, o_ref, z_scr):
    n = f_ref.shape[1]

    @pl.when(pl.program_id(1) == 0)
    def _():
        f = f_ref[0]
        z_scr[0:n, :] = jnp.dot(f, bc_ref[...], preferred_element_type=F32).astype(BF16)
        z_scr[n:2 * n, :] = jnp.dot(f, bs_ref[...], preferred_element_type=F32).astype(BF16)

    o_ref[0] = jnp.dot(cs_ref[...], z_scr[...], preferred_element_type=F32).astype(BF16)


def _fourier(f, bc, bs, cs, tn):
    b, n, _ = f.shape
    return pl.pallas_call(
        _four_kernel,
        out_shape=jax.ShapeDtypeStruct((b, n, FNET_WIDTH), BF16),
        grid=(b, n // tn),
        in_specs=[
            pl.BlockSpec((1, n, FNET_WIDTH), lambda i, j: (i, 0, 0)),
            pl.BlockSpec((FNET_WIDTH, FNET_WIDTH), lambda i, j: (0, 0)),
            pl.BlockSpec((FNET_WIDTH, FNET_WIDTH), lambda i, j: (0, 0)),
            pl.BlockSpec((tn, 2 * n), lambda i, j: (j, 0)),
        ],
        out_specs=pl.BlockSpec((1, tn, FNET_WIDTH), lambda i, j: (i, j, 0)),
        scratch_shapes=[pltpu.VMEM((2 * n, FNET_WIDTH), BF16)],
        compiler_params=_cparams(("arbitrary", "arbitrary")),
        name="fourier",
    )(f, bc, bs, cs)


def _outp_kernel(ret_ref, four_ref, x_ref, g1_ref, sh2_ref, sc2_ref, gpost_ref, gpre_ref,
                 wo_ref, wq_ref, x1_ref, h2_ref, qp_ref):
    mix = jnp.dot(ret_ref[0], wo_ref[0:RET_WIDTH, :], preferred_element_type=F32)
    mix += jnp.dot(four_ref[0], wo_ref[RET_WIDTH:, :], preferred_element_type=F32)
    x1 = x_ref[0] + g1_ref[0] * _rms_rows(mix, gpost_ref[...])
    x1_ref[0] = x1
    h2 = _rms_rows(x1, gpre_ref[...]) * (1.0 + sc2_ref[0]) + sh2_ref[0]
    h2_ref[0] = h2
    qp_ref[0] = jnp.dot(h2.astype(BF16), wq_ref[...], preferred_element_type=F32).astype(BF16)


def _outproj(ret, four, x, g1, sh2, sc2, gpost, gpre, w_out, w_q, tm):
    b, n, _ = x.shape
    qw = w_q.shape[1]
    mod = lambda: pl.BlockSpec((1, 1, D_MODEL), lambda i, j: (i, 0, 0))
    vec = lambda: pl.BlockSpec((1, D_MODEL), lambda i, j: (0, 0))
    half = lambda: pl.BlockSpec((1, tm, 512), lambda i, j: (i, j, 0))
    full = lambda: pl.BlockSpec((1, tm, D_MODEL), lambda i, j: (i, j, 0))
    return pl.pallas_call(
        _outp_kernel,
        out_shape=(jax.ShapeDtypeStruct((b, n, D_MODEL), F32),
                   jax.ShapeDtypeStruct((b, n, D_MODEL), F32),
                   jax.ShapeDtypeStruct((b, n, qw), BF16)),
        grid=(b, n // tm),
        in_specs=[half(), half(), full(), mod(), mod(), mod(), vec(), vec(),
                  pl.BlockSpec(w_out.shape, lambda i, j: (0, 0)),
                  pl.BlockSpec(w_q.shape, lambda i, j: (0, 0))],
        out_specs=(full(), full(), pl.BlockSpec((1, tm, qw), lambda i, j: (i, j, 0))),
        compiler_params=_cparams(("arbitrary", "arbitrary")),
        name="outproj",
    )(ret, four, x, g1, sh2, sc2, gpost, gpre, w_out, w_q)


def _top_rows(s, kk, rid=None):
    rows, tq = s.shape
    if rid is None:
        rid = lax.broadcasted_iota(I32, (rows, tq), 0).astype(F32)
    slot = lax.broadcasted_iota(I32, (kk, tq), 0)
    vals = jnp.zeros((kk, tq), F32)
    idxs = jnp.zeros((kk, tq), F32)
    for it in range(kk):
        m = jnp.max(s, axis=0, keepdims=True)
        am = jnp.min(jnp.where(s == m, rid, float(2 ** 20)), axis=0, keepdims=True)
        vals = jnp.where(slot == it, m, vals)
        idxs = jnp.where(slot == it, am, idxs)
        s = jnp.where(rid == am, -jnp.inf, s)
    return vals, idxs


def _sorting_network(n):
    pairs = []
    p = 1
    while p < n:
        k = p
        while k >= 1:
            for j in range(k % p, n - k, 2 * k):
                for i in range(min(k, n - j - k)):
                    if (i + j) // (2 * p) == (i + j + k) // (2 * p):
                        pairs.append((i + j, i + j + k))
            k //= 2
        p *= 2
    return pairs


def _top_rows_by_columns(s, kk):
    rows, tq = s.shape
    nv = rows // SUBLANES
    assert nv == kk
    sub = lax.broadcasted_iota(I32, (SUBLANES, tq), 0).astype(F32)
    v = [s[k * SUBLANES:(k + 1) * SUBLANES] for k in range(nv)]
    ids = [sub + float(k * SUBLANES) for k in range(nv)]
    for a, b in _sorting_network(nv):
        swap = (v[b] > v[a]) | ((v[b] == v[a]) & (ids[b] < ids[a]))
        v[a], v[b] = jnp.where(swap, v[b], v[a]), jnp.where(swap, v[a], v[b])
        ids[a], ids[b] = jnp.where(swap, ids[b], ids[a]), jnp.where(swap, ids[a], ids[b])
    slot = lax.broadcasted_iota(I32, (kk, tq), 0)
    vals = jnp.zeros((kk, tq), F32)
    idxs = jnp.zeros((kk, tq), F32)
    for it in range(kk):
        m = jnp.max(v[0], axis=0, keepdims=True)
        am = jnp.min(jnp.where(v[0] == m, ids[0], float(rows)), axis=0, keepdims=True)
        vals = jnp.where(slot == it, m, vals)
        idxs = jnp.where(slot == it, am, idxs)
        hit = ids[0] == am
        for k in range(kk - 1 - it):
            v[k] = jnp.where(hit, v[k + 1], v[k])
            ids[k] = jnp.where(hit, ids[k + 1], ids[k])
    return vals, idxs


def _pick_rows(table, sel, kk):
    out = jnp.zeros_like(table)
    for i in range(kk):
        out = jnp.where(sel == float(i), table[i:i + 1, :], out)
    return out


def _pair_candidates(v1, v2):
    kk = PEER_TOPK
    tq = v1.shape[1]
    sub = lax.broadcasted_iota(I32, (SUBLANES, tq), 0)
    subf = sub.astype(F32)
    upper = sub >= 4
    vals = [v1[0:1] + v2[0:8], v1[0:1] + v2[8:16], v1[1:2] + v2[0:8], v1[2:3] + v2[0:8], v1[3:4] + v2[0:8]]
    ids = [subf, subf + 8.0, subf + float(kk), subf + float(2 * kk), subf + float(3 * kk)]
    for j in range(3):
        vals.append(jnp.where(upper, v1[0:8] + v2[j:j + 1], -jnp.inf))
        ids.append(subf * float(kk) + float(j))
    vals.append(v1[8:16] + v2[0:1])
    ids.append((subf + 8.0) * float(kk))
    return jnp.concatenate(vals, axis=0), jnp.concatenate(ids, axis=0)


def _topk_kernel(qp_ref, k1_ref, k2_ref, off_ref, wts_ref):
    kk = PEER_TOPK
    half = PEER_D_KEY // 2
    nt = (((1,), (1,)), ((), ()))
    k1 = k1_ref[...]
    k2 = k2_ref[...]
    ids_all, w_all = [], []
    for hh in range(PEER_HEADS):
        q1 = qp_ref[:, hh * PEER_D_KEY: hh * PEER_D_KEY + half]
        q2 = qp_ref[:, hh * PEER_D_KEY + half: (hh + 1) * PEER_D_KEY]
        s1 = lax.dot_general(k1, q1, nt, preferred_element_type=F32)
        s2 = lax.dot_general(k2, q2, nt, preferred_element_type=F32)
        v1, i1 = _top_rows_by_columns(s1, kk)
        v2, i2 = _top_rows_by_columns(s2, kk)
        cand, flat = _pair_candidates(v1, v2)
        tv, ti = _top_rows(cand, kk, flat)
        hi = jnp.floor(ti * (1.0 / kk))
        lo = ti - hi * kk
        ids = _pick_rows(i1, hi, kk) * float(PEER_N_KEYS) + _pick_rows(i2, lo, kk)
        p = jnp.exp(tv - tv[0:1, :])
        w = p / jnp.sum(p, axis=0, keepdims=True)
        ids_all.append(ids)
        w_all.append(w)
    ids_t = jnp.concatenate(ids_all, axis=0).T.astype(I32)
    slot = lax.broadcasted_iota(I32, ids_t.shape, 1)
    off_ref[...] = ids_t * TABLE_ROWS_PER_EXPERT + jnp.where((slot % SUBLANES) < 4, TABLE_FRONT_PAD, 0)
    wts_ref[...] = jnp.concatenate(w_all, axis=0).T


def _topk(qp, k1, k2):
    t, qw = qp.shape
    tq = LANES
    out = lambda: pl.BlockSpec((tq, PEER_SLOTS), lambda i: (i, 0))
    key = lambda: pl.BlockSpec((PEER_N_KEYS, PEER_D_KEY // 2), lambda i: (0, 0))
    return pl.pallas_call(
        _topk_kernel,
        out_shape=(jax.ShapeDtypeStruct((t, PEER_SLOTS), I32),
                   jax.ShapeDtypeStruct((t, PEER_SLOTS), F32)),
        grid=(t // tq,),
        in_specs=[pl.BlockSpec((tq, qw), lambda i: (i, 0)), key(), key()],
        out_specs=(out(), out()),
        compiler_params=_cparams(("arbitrary",)),
        name="peer_topk",
    )(qp, k1, k2)


PEER_SLOTS = PEER_HEADS * PEER_TOPK
PEER_TOKENS_PER_STEP = 128
PEER_TOKENS_PER_HALF = PEER_TOKENS_PER_STEP // 2
TABLE_ROWS_PER_EXPERT = 4
TABLE_FRONT_PAD = 4
TABLE_BACK_PAD = 8
HI_MASK = np.uint32(0xFFFF0000)


def _pack_table(w):
    e = w.shape[0]
    bits = lax.bitcast_convert_type(w.astype(BF16), jnp.uint16).astype(U32)
    bits = bits.reshape(e, 2, TABLE_ROWS_PER_EXPERT, LANES)
    words = (bits[:, 0] | (bits[:, 1] << 16)).reshape(e * TABLE_ROWS_PER_EXPERT, LANES)
    return jnp.pad(words, ((TABLE_FRONT_PAD, TABLE_BACK_PAD), (0, 0)))


def _expert_pair(tbl_ref, off_s, j, g, i, low4):
    wa = tbl_ref[pl.ds(off_s[j, g * SUBLANES + i], SUBLANES), :]
    wb = tbl_ref[pl.ds(off_s[j, g * SUBLANES + i + 4], SUBLANES), :]
    return jnp.where(low4, wa, wb)


def _for_token_halves(off_hbm, bufs, sem, half_body):
    step = pl.program_id(0)
    last = pl.num_programs(0) - 1

    def fill(at_step, half):
        start = at_step * PEER_TOKENS_PER_STEP + half * PEER_TOKENS_PER_HALF
        return pltpu.make_async_copy(off_hbm.at[pl.ds(start, PEER_TOKENS_PER_HALF)], bufs[half], sem.at[half])

    @pl.when(step == 0)
    def _():
        for half in range(2):
            fill(0, half).start()

    for half in range(2):
        fill(step, half).wait()
        half_body(half, bufs[half])

        @pl.when(step < last)
        def _():
            fill(step + 1, half).start()


def _offset_scratch():
    return [pltpu.SMEM((PEER_TOKENS_PER_HALF, PEER_SLOTS), I32),
            pltpu.SMEM((PEER_TOKENS_PER_HALF, PEER_SLOTS), I32),
            pltpu.SemaphoreType.DMA((2,))]


def _gelu_exact(x):
    return 0.5 * x * (1.0 + lax.erf(x * (2.0 ** -0.5)))


def _peer_u_slot_sum():
    col = np.arange(PEER_SLOTS * TABLE_ROWS_PER_EXPERT)
    q, r = col // SUBLANES, col % SUBLANES
    slot = SUBLANES * (q // 4) + (q % 4) + np.where(r < 4, 0, 4)
    return jnp.asarray((slot[:, None] == np.arange(PEER_SLOTS)[None, :]).astype(np.float32), BF16)


def _split_bf16(x):
    hi = x.astype(BF16)
    return hi, (x - hi.astype(F32)).astype(BF16)


def _peer_u_kernel(off_hbm, h_ref, wts_ref, slotsum_ref, tbl_ref, coef_ref, off_a, off_b, sem):
    nt = (((1,), (1,)), ((), ()))
    ones = jnp.ones((SUBLANES, LANES), BF16)
    sub = lax.broadcasted_iota(I32, (SUBLANES, LANES), 0)
    low4 = sub < 4
    r4 = sub % TABLE_ROWS_PER_EXPERT

    def token_row_sums(off_s, jj, j):
        def chunk(c):
            return jnp.broadcast_to(h_ref[j:j + 1, c * LANES:(c + 1) * LANES], (SUBLANES, LANES))

        def by_row(c0):
            return jnp.where(r4 == 0, chunk(c0), jnp.where(r4 == 1, chunk(c0 + 1),
                             jnp.where(r4 == 2, chunk(c0 + 2), chunk(c0 + 3))))

        hlo = by_row(0)
        hhi = by_row(TABLE_ROWS_PER_EXPERT)
        prods = []
        for q in range(PEER_SLOTS // 2):
            w = _expert_pair(tbl_ref, off_s, jj, q // 4, q % 4, low4)
            lo = lax.bitcast_convert_type(w << 16, F32)
            hi = lax.bitcast_convert_type(w & HI_MASK, F32)
            prods.append(lo * hlo + hi * hhi)
        stacked = jnp.concatenate(prods, axis=0).astype(BF16)
        return lax.dot_general(ones, stacked, nt, preferred_element_type=F32)[0:1, :]

    def finish(half, sums):
        hi, lo = _split_bf16(sums)
        acts = (jnp.dot(hi, slotsum_ref[...], preferred_element_type=F32)
                + jnp.dot(lo, slotsum_ref[...], preferred_element_type=F32))
        rows = pl.ds(half * PEER_TOKENS_PER_HALF, PEER_TOKENS_PER_HALF)
        coef_ref[rows, :] = wts_ref[rows, :] * _gelu_exact(acts)

    pending = []

    def half_body(half, off_s):
        while pending:
            finish(*pending.pop())
        base = half * PEER_TOKENS_PER_HALF
        sums = [token_row_sums(off_s, jj, base + jj) for jj in range(PEER_TOKENS_PER_HALF)]
        pending.append((half, jnp.concatenate(sums, axis=0)))

    _for_token_halves(off_hbm, (off_a, off_b), sem, half_body)
    finish(*pending.pop())


def _peer_u(off, h, wts, tbl):
    t = h.shape[0]
    tb = PEER_TOKENS_PER_STEP
    slotsum = _peer_u_slot_sum()
    return pl.pallas_call(
        _peer_u_kernel,
        out_shape=jax.ShapeDtypeStruct((t, PEER_SLOTS), F32),
        grid=(t // tb,),
        in_specs=[pl.BlockSpec(memory_space=pl.ANY),
                  pl.BlockSpec((tb, D_MODEL), lambda i: (i, 0)),
                  pl.BlockSpec((tb, PEER_SLOTS), lambda i: (i, 0)),
                  pl.BlockSpec(slotsum.shape, lambda i: (0, 0)),
                  pl.BlockSpec(memory_space=pltpu.VMEM)],
        out_specs=pl.BlockSpec((tb, PEER_SLOTS), lambda i: (i, 0)),
        scratch_shapes=_offset_scratch(),
        compiler_params=_cparams(("arbitrary",)),
        name="peer_u",
    )(off, h, wts, slotsum, tbl)


def _peer_v_tables():
    col = np.arange(PEER_SLOTS * SUBLANES)
    q, r = col // 16, col % 16
    slot = SUBLANES * (q // 4) + (q % 4) + np.where(r < 8, 0, 4)
    expand = (slot[None, :] == np.arange(PEER_SLOTS)[:, None]).astype(np.float32)
    out_row = (r // 2) % 4 + 4 * (r % 2)
    rowsel = (out_row[None, :] == np.arange(SUBLANES)[:, None]).astype(np.float32)
    return jnp.asarray(expand, BF16), jnp.asarray(rowsel, F32)


def _peer_v_kernel(off_hbm, coef_ref, x1_ref, g2_ref, gain_ref, expand_ref, rowsel_ref, tbl_ref, o_ref,
                   out_scr, off_a, off_b, sem):
    sub = lax.broadcasted_iota(I32, (SUBLANES, LANES), 0)
    low4 = sub < 4
    cexp = jnp.dot(coef_ref[...].astype(BF16), expand_ref[...], preferred_element_type=F32)
    rowsel = rowsel_ref[...]

    def half_body(half, off_s):
        for jj in range(PEER_TOKENS_PER_HALF):
            j = half * PEER_TOKENS_PER_HALF + jj
            tiles = []
            for q in range(PEER_SLOTS // 2):
                w = _expert_pair(tbl_ref, off_s, jj, q // 4, q % 4, low4)
                tiles.append(pltpu.bitcast(w, BF16))
            wmat = jnp.concatenate(tiles, axis=0)
            c = (cexp[j:j + 1, :] * rowsel).astype(BF16)
            out = jnp.dot(c, wmat, preferred_element_type=F32)
            for s in range(SUBLANES):
                out_scr[j:j + 1, s * LANES:(s + 1) * LANES] = out[s:s + 1, :]

    _for_token_halves(off_hbm, (off_a, off_b), sem, half_body)
    y = _rms_rows(out_scr[...], gain_ref[...])
    o_ref[...] = x1_ref[...] + g2_ref[0] * y


def _peer_v(off, coef, x1, g2, gain, tbl, tokens_per_sample):
    t = x1.shape[0]
    tb = PEER_TOKENS_PER_STEP
    steps_per_sample = tokens_per_sample // tb
    expand, rowsel = _peer_v_tables()
    tok = lambda: pl.BlockSpec((tb, D_MODEL), lambda i: (i, 0))
    return pl.pallas_call(
        _peer_v_kernel,
        out_shape=jax.ShapeDtypeStruct((t, D_MODEL), F32),
        grid=(t // tb,),
        in_specs=[pl.BlockSpec(memory_space=pl.ANY),
                  pl.BlockSpec((tb, PEER_SLOTS), lambda i: (i, 0)),
                  tok(),
                  pl.BlockSpec((1, 1, D_MODEL), lambda i: (i // steps_per_sample, 0, 0)),
                  pl.BlockSpec((1, D_MODEL), lambda i: (0, 0)),
                  pl.BlockSpec(expand.shape, lambda i: (0, 0)),
                  pl.BlockSpec(rowsel.shape, lambda i: (0, 0)),
                  pl.BlockSpec(memory_space=pltpu.VMEM)],
        out_specs=tok(),
        scratch_shapes=[pltpu.VMEM((tb, D_MODEL), F32)] + _offset_scratch(),
        compiler_params=_cparams(("arbitrary",)),
        name="peer_v",
    )(off, coef, x1, g2, gain, expand, rowsel, tbl)


@functools.lru_cache(maxsize=None)
def _rope_tables_np(n):
    rows = n // GRID_W
    row_ids = np.repeat(np.arange(rows, dtype=np.float64), GRID_W)
    col_ids = np.tile(np.arange(GRID_W, dtype=np.float64), rows)
    freqs = ROPE_BASE ** (-np.arange(ROPE_PAIRS, dtype=np.float64) / ROPE_PAIRS)
    ang = np.concatenate([row_ids[:, None] * freqs[None, :], col_ids[:, None] * freqs[None, :]], axis=-1)
    cos, sin = np.cos(ang), np.sin(ang)
    cos_h = np.concatenate([cos, cos], axis=-1)
    sin_h = np.concatenate([-sin, sin], axis=-1)
    return (np.tile(cos_h, (1, RET_HEADS)).astype(np.float32),
            np.tile(sin_h, (1, RET_HEADS)).astype(np.float32))


@functools.lru_cache(maxsize=None)
def _dft_tables_np(n):
    scale = (n * FNET_GROUP_DIM) ** -0.5
    jn = np.arange(n, dtype=np.int64)
    ang_n = ((jn[:, None] * jn[None, :]) % n).astype(np.float64) * (2.0 * np.pi / n)
    cs = np.concatenate([np.cos(ang_n), -np.sin(ang_n)], axis=1).astype(np.float32)
    jc = np.arange(FNET_GROUP_DIM, dtype=np.int64)
    ang_c = ((jc[:, None] * jc[None, :]) % FNET_GROUP_DIM).astype(np.float64) * (2.0 * np.pi / FNET_GROUP_DIM)
    eye = np.eye(FNET_GROUPS)
    bc = np.kron(eye, np.cos(ang_c) * scale).astype(np.float32)
    bs = np.kron(eye, np.sin(ang_c) * scale).astype(np.float32)
    return bc, bs, cs


def kernel(x, c, ctx, c_ctx, w_ada, b_ada, norm_pre_mix, norm_post_mix, w_in, ret_decay_fwd, ret_decay_bwd, ret_norm_gain, w_out, norm_pre_ffn, norm_post_ffn, peer_w_query, peer_sub_keys_1, peer_sub_keys_2, peer_u, peer_v):
    b, n, d = x.shape
    depth = w_ada.shape[0]
    assert depth == 1 and d == D_MODEL and n % RET_CHUNK == 0 and n % GRID_W == 0
    t = b * n
    tm = min(256, n)
    row = lambda a: a.reshape(1, -1)

    cos, sin = [jnp.asarray(a) for a in _rope_tables_np(n)]
    bc, bs, cs = [jnp.asarray(a).astype(BF16) for a in _dft_tables_np(n)]

    pad = (-(b + 1)) % SUBLANES
    c_all = jnp.concatenate([c, c_ctx[None, :], jnp.zeros((pad, d), F32)], axis=0)
    mod = _ada(c_all, w_ada[0], row(b_ada[0]))
    sh1, sc1, g1, sh2, sc2, g2 = [mod[:b, i * d:(i + 1) * d].reshape(b, 1, d) for i in range(N_MOD)]
    csh1 = mod[b:b + 1, 0:d]
    csc1 = mod[b:b + 1, d:2 * d]

    a_f, a_b = ret_decay_fwd[0], ret_decay_bwd[0]
    a_f_lane = row(jnp.repeat(a_f, RET_DK))
    a_b_lane = row(jnp.repeat(a_b, RET_DK))

    w_in_b = w_in[0].astype(BF16)
    stf, stb = _ctx_states(ctx, row(norm_pre_mix[0]), csh1, csc1,
                           w_in_b[:, QK_WIDTH:QK_WIDTH + 2 * QK_WIDTH], a_f_lane, a_b_lane)

    q, k, v, gate, f = _inproj(x, sh1, sc1, row(norm_pre_mix[0]), w_in_b, cos, sin, tm)
    ret = _retention(q, k, v, gate, stf, stb, a_f, a_b, a_f_lane, a_b_lane, row(ret_norm_gain[0]))
    four = _fourier(f, bc, bs, cs, tm)

    x1, h2, qp = _outproj(ret, four, x, g1, sh2, sc2, row(norm_post_mix[0]), row(norm_pre_ffn[0]),
                          w_out[0].astype(BF16), peer_w_query[0].astype(BF16), tm)

    off, wts = _topk(qp.reshape(t, -1), peer_sub_keys_1[0].astype(BF16), peer_sub_keys_2[0].astype(BF16))

    coef = _peer_u(off, h2.reshape(t, d), wts, _pack_table(peer_u[0]))
    out = _peer_v(off, coef, x1.reshape(t, d), g2, row(norm_post_ffn[0]), _pack_table(peer_v[0]), n)
    return out.reshape(b, n, d)
```

```python
import functools

import numpy as np
import jax
import jax.numpy as jnp
from jax import lax
from jax.experimental import pallas as pl
from jax.experimental.pallas import tpu as pltpu

F32 = jnp.float32
BF16 = jnp.bfloat16
I32 = jnp.int32
U32 = jnp.uint32

D_MODEL = 1024
GRID_W = 64
RET_HEADS = 8
RET_DK = 64
RET_WIDTH = 512
QK_WIDTH = 512
RET_CHUNK = 128
ROPE_PAIRS = RET_DK // 4
ROPE_BASE = 10000.0
FNET_GROUPS = 4
FNET_GROUP_DIM = 128
FNET_WIDTH = 512
IN_WIDTH = 2560
PEER_HEADS = 8
PEER_D_KEY = 256
PEER_N_KEYS = 128
PEER_TOPK = 16
N_MOD = 6
EPS = 1e-6

LANES = 128
SUBLANES = 8
VMEM_LIMIT = 56 * 1024 * 1024
N_PAIRS = RET_HEADS // 2


def _cparams(sem):
    return pltpu.CompilerParams(dimension_semantics=sem, vmem_limit_bytes=VMEM_LIMIT)


def _rms_rows(x, gain):
    ms = jnp.mean(x * x, axis=-1, keepdims=True)
    return x * lax.rsqrt(ms + EPS) * gain


def _silu(x):
    return x * jax.nn.sigmoid(x)


def _log_gamma(a):
    return jnp.log1p(-jnp.exp(a))


def _ada_kernel(c_ref, w_ref, b_ref, o_ref):
    s = _silu(c_ref[...])
    o_ref[...] = jnp.dot(s, w_ref[...], preferred_element_type=F32) + b_ref[...]


def _ada(c_all, w_ada, b_ada):
    rows = c_all.shape[0]
    width = w_ada.shape[1]
    tn = 1536
    return pl.pallas_call(
        _ada_kernel,
        out_shape=jax.ShapeDtypeStruct((rows, width), F32),
        grid=(width // tn,),
        in_specs=[
            pl.BlockSpec((rows, D_MODEL), lambda j: (0, 0)),
            pl.BlockSpec((D_MODEL, tn), lambda j: (0, j)),
            pl.BlockSpec((1, tn), lambda j: (0, j)),
        ],
        out_specs=pl.BlockSpec((rows, tn), lambda j: (0, j)),
        compiler_params=_cparams(("arbitrary",)),
        name="ada",
    )(c_all, w_ada, b_ada)


def _ctx_kernel(ctx_ref, g_ref, sh_ref, sc_ref, w_ref, af_ref, ab_ref, stf_ref, stb_ref):
    x = ctx_ref[0]
    n = x.shape[0]
    h = (_rms_rows(x, g_ref[...]) * (1.0 + sc_ref[...]) + sh_ref[...]).astype(BF16)
    kv = jnp.dot(h, w_ref[...], preferred_element_type=F32)
    k = kv[:, :QK_WIDTH] * (RET_DK ** -0.5)
    v = kv[:, QK_WIDTH:].astype(BF16)
    pos = lax.broadcasted_iota(I32, (n, QK_WIDTH), 0).astype(F32)
    lgf = _log_gamma(af_ref[...])
    lgb = _log_gamma(ab_ref[...])
    kf = (k * jnp.exp(lgf * (n - 1.0 - pos))).astype(BF16)
    kb = (k * jnp.exp(lgb * pos)).astype(BF16)
    tn = (((0,), (0,)), ((), ()))
    gf = lax.dot_general(kf, v, tn, preferred_element_type=F32)
    gb = lax.dot_general(kb, v, tn, preferred_element_type=F32)
    ri = lax.broadcasted_iota(I32, (LANES, LANES), 0) // RET_DK
    ci = lax.broadcasted_iota(I32, (LANES, LANES), 1) // RET_DK
    bd = ri == ci
    for p in range(N_PAIRS):
        sl = slice(p * LANES, (p + 1) * LANES)
        stf_ref[0, p] = jnp.where(bd, gf[sl, sl], 0.0)
        stb_ref[0, p] = jnp.where(bd, gb[sl, sl], 0.0)


def _ctx_states(ctx, gain, csh, csc, w_kv, a_f, a_b):
    b, n, _ = ctx.shape
    vec = lambda: pl.BlockSpec((1, D_MODEL), lambda i: (0, 0))
    lane = lambda: pl.BlockSpec((1, QK_WIDTH), lambda i: (0, 0))
    st = jax.ShapeDtypeStruct((b, N_PAIRS, LANES, LANES), F32)
    st_spec = lambda: pl.BlockSpec((1, N_PAIRS, LANES, LANES), lambda i: (i, 0, 0, 0))
    return pl.pallas_call(
        _ctx_kernel,
        out_shape=(st, st),
        grid=(b,),
        in_specs=[
            pl.BlockSpec((1, n, D_MODEL), lambda i: (i, 0, 0)),
            vec(), vec(), vec(),
            pl.BlockSpec((D_MODEL, 2 * QK_WIDTH), lambda i: (0, 0)),
            lane(), lane(),
        ],
        out_specs=(st_spec(), st_spec()),
        compiler_params=_cparams(("arbitrary",)),
        name="ctx_states",
    )(ctx, gain, csh, csc, w_kv, a_f, a_b)


def _inproj_kernel(x_ref, sh_ref, sc_ref, g_ref, w_ref, cos_ref, sin_ref,
                   q_ref, k_ref, v_ref, gate_ref, f_ref):
    x = x_ref[0]
    tm = x.shape[0]
    h = (_rms_rows(x, g_ref[...]) * (1.0 + sc_ref[0]) + sh_ref[0]).astype(BF16)

    def proj(lo, hi):
        return jnp.dot(h, w_ref[:, lo:hi], preferred_element_type=F32)

    cos = cos_ref[...]
    sin = sin_ref[...]
    lane = lax.broadcasted_iota(I32, (tm, QK_WIDTH), 1)
    first = (lane % RET_DK) < (RET_DK // 2)

    def rope(t):
        swapped = jnp.where(first,
                            pltpu.roll(t, QK_WIDTH - RET_DK // 2, 1),
                            pltpu.roll(t, RET_DK // 2, 1))
        return t * cos + swapped * sin

    q_ref[0] = rope(proj(0, 512)).astype(BF16)
    k_ref[0] = rope(proj(512, 1024) * (RET_DK ** -0.5)).astype(BF16)
    v_ref[0] = proj(1024, 1536).astype(BF16)
    gate_ref[0] = proj(1536, 2048)
    f_ref[0] = proj(2048, 2560).astype(BF16)


def _inproj(x, sh, sc, gain, w_in, cos, sin, tm):
    b, n, _ = x.shape
    mod = lambda: pl.BlockSpec((1, 1, D_MODEL), lambda i, j: (i, 0, 0))
    tab = lambda: pl.BlockSpec((tm, QK_WIDTH), lambda i, j: (j, 0))
    out = lambda: pl.BlockSpec((1, tm, 512), lambda i, j: (i, j, 0))
    sd = lambda dt: jax.ShapeDtypeStruct((b, n, 512), dt)
    return pl.pallas_call(
        _inproj_kernel,
        out_shape=(sd(BF16), sd(BF16), sd(BF16), sd(F32), sd(BF16)),
        grid=(b, n // tm),
        in_specs=[
            pl.BlockSpec((1, tm, D_MODEL), lambda i, j: (i, j, 0)),
            mod(), mod(),
            pl.BlockSpec((1, D_MODEL), lambda i, j: (0, 0)),
            pl.BlockSpec((D_MODEL, IN_WIDTH), lambda i, j: (0, 0)),
            tab(), tab(),
        ],
        out_specs=(out(), out(), out(), out(), out()),
        compiler_params=_cparams(("arbitrary", "arbitrary")),
        name="inproj",
    )(x, sh, sc, gain, w_in, cos, sin)


def _ret_kernel(af_s, ab_s, q_ref, k_ref, v_ref, gate_ref, stf_ref, stb_ref, af_ref, ab_ref,
                gain_ref, o_ref, mask_scr, qdf_scr, qdb_scr, kdf_scr, kdb_scr, sf_scr, sb_scr,
                sb_store):
    c = RET_CHUNK
    n = q_ref.shape[1]
    nc = n // c
    lgf = _log_gamma(af_ref[...])
    lgb = _log_gamma(ab_ref[...])
    cdf = jnp.exp(lgf * float(c))
    cdb = jnp.exp(lgb * float(c))

    @pl.when(pl.program_id(0) == 0)
    def _():
        r = lax.broadcasted_iota(I32, (c, RET_WIDTH), 0).astype(F32)
        qdf_scr[...] = jnp.exp(lgf * (r + 1.0))
        kdf_scr[...] = jnp.exp(lgf * (c - 1.0 - r))
        qdb_scr[...] = jnp.exp(lgb * (c - r))
        kdb_scr[...] = jnp.exp(lgb * r)
        ri = lax.broadcasted_iota(I32, (c, c), 0)
        ci = lax.broadcasted_iota(I32, (c, c), 1)
        d = (ri - ci).astype(F32)
        for hh in range(RET_HEADS):
            gf = _log_gamma(jnp.full((c, c), af_s[hh], F32))
            gb = _log_gamma(jnp.full((c, c), ab_s[hh], F32))
            mask_scr[hh] = jnp.where(d >= 0.0, jnp.exp(gf * jnp.maximum(d, 0.0)),
                                     jnp.exp(gb * jnp.maximum(-d, 0.0)))

    ri = lax.broadcasted_iota(I32, (LANES, LANES), 0) // RET_DK
    ci = lax.broadcasted_iota(I32, (LANES, LANES), 1) // RET_DK
    bd = ri == ci
    bd_ones = jnp.where(bd, 1.0, 0.0).astype(BF16)
    lane_head = lax.broadcasted_iota(I32, (c, LANES), 1) // RET_DK
    tn = (((0,), (0,)), ((), ()))
    nt = (((1,), (1,)), ((), ()))

    for p in range(N_PAIRS):
        sf_scr[p] = stf_ref[0, p]
        sb_scr[p] = stb_ref[0, p]

    def bwd_states(i, carry):
        cc = nc - 1 - i
        rows = pl.ds(pl.multiple_of(cc * c, c), c)
        for p in range(N_PAIRS):
            sl = slice(p * LANES, (p + 1) * LANES)
            sb = sb_scr[p]
            sb_store[cc, p] = sb.astype(BF16)
            kw = (k_ref[0, rows, sl].astype(F32) * kdb_scr[:, sl]).astype(BF16)
            g = lax.dot_general(kw, v_ref[0, rows, sl], tn, preferred_element_type=F32)
            sb_scr[p] = sb * cdb[:, sl] + jnp.where(bd, g, 0.0)
        return carry

    lax.fori_loop(0, nc, bwd_states, 0)

    def fwd(cc, carry):
        rows = pl.ds(pl.multiple_of(cc * c, c), c)
        outs, states = [], []
        for p in range(N_PAIRS):
            sl = slice(p * LANES, (p + 1) * LANES)
            qp = q_ref[0, rows, sl]
            kp = k_ref[0, rows, sl]
            vp = v_ref[0, rows, sl]
            qf = qp.astype(F32)
            sf = sf_scr[p]
            q_dec = jnp.concatenate([(qf * qdf_scr[:, sl]).astype(BF16), (qf * qdb_scr[:, sl]).astype(BF16)], axis=1)
            s_both = jnp.concatenate([sf.astype(BF16), sb_store[cc, p]], axis=0)
            acc = jnp.dot(q_dec, s_both, preferred_element_type=F32)
            zero = jnp.zeros_like(qp)
            q_heads = jnp.concatenate([jnp.where(lane_head == j, qp, zero) for j in range(2)], axis=0)
            s = lax.dot_general(q_heads, kp, nt, preferred_element_type=F32)
            pm = jnp.concatenate([(s[j * c:(j + 1) * c] * mask_scr[2 * p + j]).astype(BF16) for j in range(2)], axis=1)
            v_heads = jnp.concatenate([jnp.where(lane_head == j, vp, zero) for j in range(2)], axis=0)
            acc += jnp.dot(pm, v_heads, preferred_element_type=F32)
            sq_hi, sq_lo = _split_bf16(acc * acc)
            ms = jnp.dot(jnp.concatenate([sq_hi, sq_lo], axis=1), jnp.concatenate([bd_ones, bd_ones], axis=0),
                         preferred_element_type=F32) * (1.0 / RET_DK)
            y = acc * lax.rsqrt(ms + EPS) * gain_ref[:, sl] * _silu(gate_ref[0, rows, sl])
            outs.append(y.astype(BF16))
            kw = (kp.astype(F32) * kdf_scr[:, sl]).astype(BF16)
            g = lax.dot_general(kw, vp, tn, preferred_element_type=F32)
            states.append(sf * cdf[:, sl] + jnp.where(bd, g, 0.0))
        for p in range(N_PAIRS):
            o_ref[0, rows, p * LANES:(p + 1) * LANES] = outs[p]
            sf_scr[p] = states[p]
        return carry

    lax.fori_loop(0, nc, fwd, 0)


def _retention(q, k, v, gate, stf, stb, a_f, a_b, a_f_lane, a_b_lane, gain):
    b, n, _ = q.shape
    nc = n // RET_CHUNK
    seq = lambda: pl.BlockSpec((1, n, RET_WIDTH), lambda i, *_: (i, 0, 0))
    st = lambda: pl.BlockSpec((1, N_PAIRS, LANES, LANES), lambda i, *_: (i, 0, 0, 0))
    lane = lambda: pl.BlockSpec((1, RET_WIDTH), lambda i, *_: (0, 0))
    grid_spec = pltpu.PrefetchScalarGridSpec(
        num_scalar_prefetch=2,
        grid=(b,),
        in_specs=[seq(), seq(), seq(), seq(), st(), st(), lane(), lane(), lane()],
        out_specs=seq(),
        scratch_shapes=[
            pltpu.VMEM((RET_HEADS, RET_CHUNK, RET_CHUNK), F32),
            pltpu.VMEM((RET_CHUNK, RET_WIDTH), F32),
            pltpu.VMEM((RET_CHUNK, RET_WIDTH), F32),
            pltpu.VMEM((RET_CHUNK, RET_WIDTH), F32),
            pltpu.VMEM((RET_CHUNK, RET_WIDTH), F32),
            pltpu.VMEM((N_PAIRS, LANES, LANES), F32),
            pltpu.VMEM((N_PAIRS, LANES, LANES), F32),
            pltpu.VMEM((nc, N_PAIRS, LANES, LANES), BF16),
        ],
    )
    return pl.pallas_call(
        _ret_kernel,
        out_shape=jax.ShapeDtypeStruct((b, n, RET_WIDTH), BF16),
        grid_spec=grid_spec,
        compiler_params=_cparams(("arbitrary",)),
        name="retention",
    )(a_f, a_b, q, k, v, gate, stf, stb, a_f_lane, a_b_lane, gain)


def _four_kernel(f_ref, bc_ref, bs_ref, cs_ref, o_ref, z_scr):
    n = f_ref.shape[1]

    @pl.when(pl.program_id(1) == 0)
    def _():
        f = f_ref[0]
        z_scr[0:n, :] = jnp.dot(f, bc_ref[...], preferred_element_type=F32).astype(BF16)
        z_scr[n:2 * n, :] = jnp.dot(f, bs_ref[...], preferred_element_type=F32).astype(BF16)

    o_ref[0] = jnp.dot(cs_ref[...], z_scr[...], preferred_element_type=F32).astype(BF16)


def _fourier(f, bc, bs, cs, tn):
    b, n, _ = f.shape
    return pl.pallas_call(
        _four_kernel,
        out_shape=jax.ShapeDtypeStruct((b, n, FNET_WIDTH), BF16),
        grid=(b, n // tn),
        in_specs=[
            pl.BlockSpec((1, n, FNET_WIDTH), lambda i, j: (i, 0, 0)),
            pl.BlockSpec((FNET_WIDTH, FNET_WIDTH), lambda i, j: (0, 0)),
            pl.BlockSpec((FNET_WIDTH, FNET_WIDTH), lambda i, j: (0, 0)),
            pl.BlockSpec((tn, 2 * n), lambda i, j: (j, 0)),
        ],
        out_specs=pl.BlockSpec((1, tn, FNET_WIDTH), lambda i, j: (i, j, 0)),
        scratch_shapes=[pltpu.VMEM((2 * n, FNET_WIDTH), BF16)],
        compiler_params=_cparams(("arbitrary", "arbitrary")),
        name="fourier",
    )(f, bc, bs, cs)


def _outp_kernel(ret_ref, four_ref, x_ref, g1_ref, sh2_ref, sc2_ref, gpost_ref, gpre_ref,
                 wo_ref, wq_ref, x1_ref, h2_ref, qp_ref):
    mix = jnp.dot(ret_ref[0], wo_ref[0:RET_WIDTH, :], preferred_element_type=F32)
    mix += jnp.dot(four_ref[0], wo_ref[RET_WIDTH:, :], preferred_element_type=F32)
    x1 = x_ref[0] + g1_ref[0] * _rms_rows(mix, gpost_ref[...])
    x1_ref[0] = x1
    h2 = _rms_rows(x1, gpre_ref[...]) * (1.0 + sc2_ref[0]) + sh2_ref[0]
    h2_ref[0] = h2
    qp_ref[0] = jnp.dot(h2.astype(BF16), wq_ref[...], preferred_element_type=F32).astype(BF16)


def _outproj(ret, four, x, g1, sh2, sc2, gpost, gpre, w_out, w_q, tm):
    b, n, _ = x.shape
    qw = w_q.shape[1]
    mod = lambda: pl.BlockSpec((1, 1, D_MODEL), lambda i, j: (i, 0, 0))
    vec = lambda: pl.BlockSpec((1, D_MODEL), lambda i, j: (0, 0))
    half = lambda: pl.BlockSpec((1, tm, 512), lambda i, j: (i, j, 0))
    full = lambda: pl.BlockSpec((1, tm, D_MODEL), lambda i, j: (i, j, 0))
    return pl.pallas_call(
        _outp_kernel,
        out_shape=(jax.ShapeDtypeStruct((b, n, D_MODEL), F32),
                   jax.ShapeDtypeStruct((b, n, D_MODEL), F32),
                   jax.ShapeDtypeStruct((b, n, qw), BF16)),
        grid=(b, n // tm),
        in_specs=[half(), half(), full(), mod(), mod(), mod(), vec(), vec(),
                  pl.BlockSpec(w_out.shape, lambda i, j: (0, 0)),
                  pl.BlockSpec(w_q.shape, lambda i, j: (0, 0))],
        out_specs=(full(), full(), pl.BlockSpec((1, tm, qw), lambda i, j: (i, j, 0))),
        compiler_params=_cparams(("arbitrary", "arbitrary")),
        name="outproj",
    )(ret, four, x, g1, sh2, sc2, gpost, gpre, w_out, w_q)


def _sorting_network(n):
    pairs = []
    p = 1
    while p < n:
        k = p
        while k >= 1:
            for j in range(k % p, n - k, 2 * k):
                for i in range(min(k, n - j - k)):
                    if (i + j) // (2 * p) == (i + j + k) // (2 * p):
                        pairs.append((i + j, i + j + k))
            k //= 2
        p *= 2
    return pairs


def _top_rows_by_columns(s, kk):
    rows, tq = s.shape
    nv = rows // SUBLANES
    assert nv == kk
    sub = lax.broadcasted_iota(I32, (SUBLANES, tq), 0).astype(F32)
    v = [s[k * SUBLANES:(k + 1) * SUBLANES] for k in range(nv)]
    ids = [sub + float(k * SUBLANES) for k in range(nv)]
    for a, b in _sorting_network(nv):
        swap = (v[b] > v[a]) | ((v[b] == v[a]) & (ids[b] < ids[a]))
        v[a], v[b] = jnp.where(swap, v[b], v[a]), jnp.where(swap, v[a], v[b])
        ids[a], ids[b] = jnp.where(swap, ids[b], ids[a]), jnp.where(swap, ids[a], ids[b])
    slot = lax.broadcasted_iota(I32, (kk, tq), 0)
    vals = jnp.zeros((kk, tq), F32)
    idxs = jnp.zeros((kk, tq), F32)
    for it in range(kk):
        m = jnp.max(v[0], axis=0, keepdims=True)
        am = jnp.min(jnp.where(v[0] == m, ids[0], float(rows)), axis=0, keepdims=True)
        vals = jnp.where(slot == it, m, vals)
        idxs = jnp.where(slot == it, am, idxs)
        hit = ids[0] == am
        for k in range(kk - 1 - it):
            v[k] = jnp.where(hit, v[k + 1], v[k])
            ids[k] = jnp.where(hit, ids[k + 1], ids[k])
    return vals, idxs


def _pick_rows(table, sel, kk):
    out = jnp.zeros_like(table)
    for i in range(kk):
        out = jnp.where(sel == float(i), table[i:i + 1, :], out)
    return out


def _top_pairs(v1, v2):
    kk = PEER_TOPK
    tq = v1.shape[1]
    big = float(2 ** 20)
    sub = lax.broadcasted_iota(I32, (SUBLANES, tq), 0)
    subf = sub.astype(F32)
    near = v1[0:SUBLANES]
    depth = []
    for k in range(kk):
        reach = kk // (k + 1)
        row = near + v2[k:k + 1]
        depth.append(row if reach >= SUBLANES else jnp.where(sub < reach, row, -jnp.inf))
    far = v1[SUBLANES:kk] + v2[0:1]
    id_near = subf * float(kk)
    id_far = (subf + float(SUBLANES)) * float(kk)
    slot = lax.broadcasted_iota(I32, (kk, tq), 0)
    vals = jnp.zeros((kk, tq), F32)
    idxs = jnp.zeros((kk, tq), F32)
    for it in range(kk):
        m = jnp.max(jnp.maximum(depth[0], far), axis=0, keepdims=True)
        am = jnp.min(jnp.minimum(jnp.where(depth[0] == m, id_near, big), jnp.where(far == m, id_far, big)),
                     axis=0, keepdims=True)
        vals = jnp.where(slot == it, m, vals)
        idxs = jnp.where(slot == it, am, idxs)
        hit_near = id_near == am
        for k in range(kk - 1 - it):
            depth[k] = jnp.where(hit_near, depth[k + 1], depth[k])
        id_near = jnp.where(hit_near, id_near + 1.0, id_near)
        far = jnp.where(id_far == am, -jnp.inf, far)
    return vals, idxs


def _topk_kernel(qp_ref, k1_ref, k2_ref, off_ref, wts_ref):
    kk = PEER_TOPK
    half = PEER_D_KEY // 2
    nt = (((1,), (1,)), ((), ()))
    k1 = k1_ref[...]
    k2 = k2_ref[...]
    ids_all, w_all = [], []
    for hh in range(PEER_HEADS):
        q1 = qp_ref[:, hh * PEER_D_KEY: hh * PEER_D_KEY + half]
        q2 = qp_ref[:, hh * PEER_D_KEY + half: (hh + 1) * PEER_D_KEY]
        s1 = lax.dot_general(k1, q1, nt, preferred_element_type=F32)
        s2 = lax.dot_general(k2, q2, nt, preferred_element_type=F32)
        v1, i1 = _top_rows_by_columns(s1, kk)
        v2, i2 = _top_rows_by_columns(s2, kk)
        tv, ti = _top_pairs(v1, v2)
        hi = jnp.floor(ti * (1.0 / kk))
        lo = ti - hi * kk
        ids = _pick_rows(i1, hi, kk) * float(PEER_N_KEYS) + _pick_rows(i2, lo, kk)
        p = jnp.exp(tv - tv[0:1, :])
        w = p / jnp.sum(p, axis=0, keepdims=True)
        ids_all.append(ids)
        w_all.append(w)
    ids_t = jnp.concatenate(ids_all, axis=0).T.astype(I32)
    slot = lax.broadcasted_iota(I32, ids_t.shape, 1)
    off_ref[...] = (ids_t * TABLE_ROWS_PER_EXPERT
                    + jnp.where((slot % SUBLANES) < 4, TABLE_PAD_ROWS, TABLE_PAD_ROWS - TABLE_ROWS_PER_EXPERT))
    wts_ref[...] = jnp.concatenate(w_all, axis=0).T


def _topk(qp, k1, k2):
    t, qw = qp.shape
    tq = LANES
    out = lambda: pl.BlockSpec((tq, PEER_SLOTS), lambda i: (i, 0))
    key = lambda: pl.BlockSpec((PEER_N_KEYS, PEER_D_KEY // 2), lambda i: (0, 0))
    return pl.pallas_call(
        _topk_kernel,
        out_shape=(jax.ShapeDtypeStruct((t, PEER_SLOTS), I32),
                   jax.ShapeDtypeStruct((t, PEER_SLOTS), F32)),
        grid=(t // tq,),
        in_specs=[pl.BlockSpec((tq, qw), lambda i: (i, 0)), key(), key()],
        out_specs=(out(), out()),
        compiler_params=_cparams(("arbitrary",)),
        name="peer_topk",
    )(qp, k1, k2)


PEER_SLOTS = PEER_HEADS * PEER_TOPK
PEER_TOKENS_PER_STEP = 128
PEER_TOKENS_PER_HALF = PEER_TOKENS_PER_STEP // 2
TABLE_ROWS_PER_EXPERT = 4
PACK_EXPERTS_PER_STEP = 256
TABLE_PAD_ROWS = PACK_EXPERTS_PER_STEP * TABLE_ROWS_PER_EXPERT
HI_MASK = np.uint32(0xFFFF0000)


def _pack_kernel(w_ref, o_ref):
    step = pl.program_id(0)
    last = pl.num_programs(0) - 1

    @pl.when((step == 0) | (step == last))
    def _():
        o_ref[...] = jnp.zeros(o_ref.shape, U32)

    @pl.when((step > 0) & (step < last))
    def _():
        half = D_MODEL // 2
        x = w_ref[...].astype(BF16).astype(F32)
        bits = lax.bitcast_convert_type(x, U32)
        words = (bits[:, :half] >> 16) | (bits[:, half:] & HI_MASK)
        for s in range(TABLE_ROWS_PER_EXPERT):
            rows = pl.ds(s, PACK_EXPERTS_PER_STEP, stride=TABLE_ROWS_PER_EXPERT)
            o_ref[rows, :] = words[:, s * LANES:(s + 1) * LANES]


def _pack_table(w):
    e = w.shape[0]
    nblk = e // PACK_EXPERTS_PER_STEP
    return pl.pallas_call(
        _pack_kernel,
        out_shape=jax.ShapeDtypeStruct(((nblk + 2) * TABLE_PAD_ROWS, LANES), U32),
        grid=(nblk + 2,),
        in_specs=[pl.BlockSpec((PACK_EXPERTS_PER_STEP, D_MODEL), lambda i: (jnp.clip(i - 1, 0, nblk - 1), 0))],
        out_specs=pl.BlockSpec((TABLE_PAD_ROWS, LANES), lambda i: (i, 0)),
        compiler_params=_cparams(("arbitrary",)),
        name="pack_table",
    )(w)


def _expert_pair(tbl_ref, off_s, j, g, i, low4):
    wa = tbl_ref[pl.ds(off_s[j, g * SUBLANES + i], SUBLANES), :]
    wb = tbl_ref[pl.ds(off_s[j, g * SUBLANES + i + 4], SUBLANES), :]
    return jnp.where(low4, wa, wb)


def _for_token_halves(off_hbm, bufs, sem, half_body):
    step = pl.program_id(0)
    last = pl.num_programs(0) - 1

    def fill(at_step, half):
        start = at_step * PEER_TOKENS_PER_STEP + half * PEER_TOKENS_PER_HALF
        return pltpu.make_async_copy(off_hbm.at[pl.ds(start, PEER_TOKENS_PER_HALF)], bufs[half], sem.at[half])

    @pl.when(step == 0)
    def _():
        for half in range(2):
            fill(0, half).start()

    for half in range(2):
        fill(step, half).wait()
        half_body(half, bufs[half])

        @pl.when(step < last)
        def _():
            fill(step + 1, half).start()


def _offset_scratch():
    return [pltpu.SMEM((PEER_TOKENS_PER_HALF, PEER_SLOTS), I32),
            pltpu.SMEM((PEER_TOKENS_PER_HALF, PEER_SLOTS), I32),
            pltpu.SemaphoreType.DMA((2,))]


def _gelu_exact(x):
    return 0.5 * x * (1.0 + lax.erf(x * (2.0 ** -0.5)))


def _peer_u_slot_sum():
    col = np.arange(PEER_SLOTS * TABLE_ROWS_PER_EXPERT)
    q, r = col // SUBLANES, col % SUBLANES
    slot = SUBLANES * (q // 4) + (q % 4) + np.where(r < 4, 0, 4)
    return jnp.asarray((slot[:, None] == np.arange(PEER_SLOTS)[None, :]).astype(np.float32), BF16)


def _split_bf16(x):
    hi = x.astype(BF16)
    return hi, (x - hi.astype(F32)).astype(BF16)


def _peer_u_kernel(off_hbm, h_ref, wts_ref, slotsum_ref, tbl_ref, coef_ref, off_a, off_b, sem):
    nt = (((1,), (1,)), ((), ()))
    ones = jnp.ones((SUBLANES, LANES), BF16)
    sub = lax.broadcasted_iota(I32, (SUBLANES, LANES), 0)
    low4 = sub < 4
    r4 = sub % TABLE_ROWS_PER_EXPERT

    def token_row_sums(off_s, jj, j):
        def chunk(c):
            return jnp.broadcast_to(h_ref[j:j + 1, c * LANES:(c + 1) * LANES], (SUBLANES, LANES))

        def by_row(c0):
            return jnp.where(r4 == 0, chunk(c0), jnp.where(r4 == 1, chunk(c0 + 1),
                             jnp.where(r4 == 2, chunk(c0 + 2), chunk(c0 + 3))))

        hlo = by_row(0)
        hhi = by_row(TABLE_ROWS_PER_EXPERT)
        prods = []
        for q in range(PEER_SLOTS // 2):
            w = _expert_pair(tbl_ref, off_s, jj, q // 4, q % 4, low4)
            lo = lax.bitcast_convert_type(w << 16, F32)
            hi = lax.bitcast_convert_type(w & HI_MASK, F32)
            prods.append(lo * hlo + hi * hhi)
        stacked = jnp.concatenate(prods, axis=0).astype(BF16)
        return lax.dot_general(ones, stacked, nt, preferred_element_type=F32)[0:1, :]

    def finish(half, sums):
        hi, lo = _split_bf16(sums)
        acts = (jnp.dot(hi, slotsum_ref[...], preferred_element_type=F32)
                + jnp.dot(lo, slotsum_ref[...], preferred_element_type=F32))
        rows = pl.ds(half * PEER_TOKENS_PER_HALF, PEER_TOKENS_PER_HALF)
        coef_ref[rows, :] = wts_ref[rows, :] * _gelu_exact(acts)

    pending = []

    def half_body(half, off_s):
        while pending:
            finish(*pending.pop())
        base = half * PEER_TOKENS_PER_HALF
        sums = [token_row_sums(off_s, jj, base + jj) for jj in range(PEER_TOKENS_PER_HALF)]
        pending.append((half, jnp.concatenate(sums, axis=0)))

    _for_token_halves(off_hbm, (off_a, off_b), sem, half_body)
    finish(*pending.pop())


def _peer_u(off, h, wts, tbl):
    t = h.shape[0]
    tb = PEER_TOKENS_PER_STEP
    slotsum = _peer_u_slot_sum()
    return pl.pallas_call(
        _peer_u_kernel,
        out_shape=jax.ShapeDtypeStruct((t, PEER_SLOTS), F32),
        grid=(t // tb,),
        in_specs=[pl.BlockSpec(memory_space=pl.ANY),
                  pl.BlockSpec((tb, D_MODEL), lambda i: (i, 0)),
                  pl.BlockSpec((tb, PEER_SLOTS), lambda i: (i, 0)),
                  pl.BlockSpec(slotsum.shape, lambda i: (0, 0)),
                  pl.BlockSpec(memory_space=pltpu.VMEM)],
        out_specs=pl.BlockSpec((tb, PEER_SLOTS), lambda i: (i, 0)),
        scratch_shapes=_offset_scratch(),
        compiler_params=_cparams(("arbitrary",)),
        name="peer_u",
    )(off, h, wts, slotsum, tbl)


def _peer_v_tables():
    col = np.arange(PEER_SLOTS * SUBLANES)
    q, r = col // 16, col % 16
    slot = SUBLANES * (q // 4) + (q % 4) + np.where(r < 8, 0, 4)
    expand = (slot[None, :] == np.arange(PEER_SLOTS)[:, None]).astype(np.float32)
    out_row = (r // 2) % 4 + 4 * (r % 2)
    rowsel = (out_row[None, :] == np.arange(SUBLANES)[:, None]).astype(np.float32)
    return jnp.asarray(expand, BF16), jnp.asarray(rowsel, F32)


def _peer_v_kernel(off_hbm, coef_ref, x1_ref, g2_ref, gain_ref, expand_ref, rowsel_ref, tbl_ref, o_ref,
                   out_scr, off_a, off_b, sem):
    sub = lax.broadcasted_iota(I32, (SUBLANES, LANES), 0)
    low4 = sub < 4
    cexp = jnp.dot(coef_ref[...].astype(BF16), expand_ref[...], preferred_element_type=F32)
    rowsel = rowsel_ref[...]

    def half_body(half, off_s):
        for jj in range(PEER_TOKENS_PER_HALF):
            j = half * PEER_TOKENS_PER_HALF + jj
            tiles = []
            for q in range(PEER_SLOTS // 2):
                w = _expert_pair(tbl_ref, off_s, jj, q // 4, q % 4, low4)
                tiles.append(pltpu.bitcast(w, BF16))
            wmat = jnp.concatenate(tiles, axis=0)
            c = (cexp[j:j + 1, :] * rowsel).astype(BF16)
            out = jnp.dot(c, wmat, preferred_element_type=F32)
            for s in range(SUBLANES):
                out_scr[j:j + 1, s * LANES:(s + 1) * LANES] = out[s:s + 1, :]

    _for_token_halves(off_hbm, (off_a, off_b), sem, half_body)
    y = _rms_rows(out_scr[...], gain_ref[...])
    o_ref[...] = x1_ref[...] + g2_ref[0] * y


def _peer_v(off, coef, x1, g2, gain, tbl, tokens_per_sample):
    t = x1.shape[0]
    tb = PEER_TOKENS_PER_STEP
    steps_per_sample = tokens_per_sample // tb
    expand, rowsel = _peer_v_tables()
    tok = lambda: pl.BlockSpec((tb, D_MODEL), lambda i: (i, 0))
    return pl.pallas_call(
        _peer_v_kernel,
        out_shape=jax.ShapeDtypeStruct((t, D_MODEL), F32),
        grid=(t // tb,),
        in_specs=[pl.BlockSpec(memory_space=pl.ANY),
                  pl.BlockSpec((tb, PEER_SLOTS), lambda i: (i, 0)),
                  tok(),
                  pl.BlockSpec((1, 1, D_MODEL), lambda i: (i // steps_per_sample, 0, 0)),
                  pl.BlockSpec((1, D_MODEL), lambda i: (0, 0)),
                  pl.BlockSpec(expand.shape, lambda i: (0, 0)),
                  pl.BlockSpec(rowsel.shape, lambda i: (0, 0)),
                  pl.BlockSpec(memory_space=pltpu.VMEM)],
        out_specs=tok(),
        scratch_shapes=[pltpu.VMEM((tb, D_MODEL), F32)] + _offset_scratch(),
        compiler_params=_cparams(("arbitrary",)),
        name="peer_v",
    )(off, coef, x1, g2, gain, expand, rowsel, tbl)


@functools.lru_cache(maxsize=None)
def _rope_tables_np(n):
    rows = n // GRID_W
    row_ids = np.repeat(np.arange(rows, dtype=np.float64), GRID_W)
    col_ids = np.tile(np.arange(GRID_W, dtype=np.float64), rows)
    freqs = ROPE_BASE ** (-np.arange(ROPE_PAIRS, dtype=np.float64) / ROPE_PAIRS)
    ang = np.concatenate([row_ids[:, None] * freqs[None, :], col_ids[:, None] * freqs[None, :]], axis=-1)
    cos, sin = np.cos(ang), np.sin(ang)
    cos_h = np.concatenate([cos, cos], axis=-1)
    sin_h = np.concatenate([-sin, sin], axis=-1)
    return (np.tile(cos_h, (1, RET_HEADS)).astype(np.float32),
            np.tile(sin_h, (1, RET_HEADS)).astype(np.float32))


@functools.lru_cache(maxsize=None)
def _dft_tables_np(n):
    scale = (n * FNET_GROUP_DIM) ** -0.5
    jn = np.arange(n, dtype=np.int64)
    ang_n = ((jn[:, None] * jn[None, :]) % n).astype(np.float64) * (2.0 * np.pi / n)
    cs = np.concatenate([np.cos(ang_n), -np.sin(ang_n)], axis=1).astype(np.float32)
    jc = np.arange(FNET_GROUP_DIM, dtype=np.int64)
    ang_c = ((jc[:, None] * jc[None, :]) % FNET_GROUP_DIM).astype(np.float64) * (2.0 * np.pi / FNET_GROUP_DIM)
    eye = np.eye(FNET_GROUPS)
    bc = np.kron(eye, np.cos(ang_c) * scale).astype(np.float32)
    bs = np.kron(eye, np.sin(ang_c) * scale).astype(np.float32)
    return bc, bs, cs


def kernel(x, c, ctx, c_ctx, w_ada, b_ada, norm_pre_mix, norm_post_mix, w_in, ret_decay_fwd, ret_decay_bwd, ret_norm_gain, w_out, norm_pre_ffn, norm_post_ffn, peer_w_query, peer_sub_keys_1, peer_sub_keys_2, peer_u, peer_v):
    b, n, d = x.shape
    depth = w_ada.shape[0]
    assert depth == 1 and d == D_MODEL and n % RET_CHUNK == 0 and n % GRID_W == 0
    t = b * n
    tm = min(256, n)
    row = lambda a: a.reshape(1, -1)

    cos, sin = [jnp.asarray(a) for a in _rope_tables_np(n)]
    bc, bs, cs = [jnp.asarray(a).astype(BF16) for a in _dft_tables_np(n)]

    pad = (-(b + 1)) % SUBLANES
    c_all = jnp.concatenate([c, c_ctx[None, :], jnp.zeros((pad, d), F32)], axis=0)
    mod = _ada(c_all, w_ada[0], row(b_ada[0]))
    sh1, sc1, g1, sh2, sc2, g2 = [mod[:b, i * d:(i + 1) * d].reshape(b, 1, d) for i in range(N_MOD)]
    csh1 = mod[b:b + 1, 0:d]
    csc1 = mod[b:b + 1, d:2 * d]

    a_f, a_b = ret_decay_fwd[0], ret_decay_bwd[0]
    a_f_lane = row(jnp.repeat(a_f, RET_DK))
    a_b_lane = row(jnp.repeat(a_b, RET_DK))

    w_in_b = w_in[0].astype(BF16)
    stf, stb = _ctx_states(ctx, row(norm_pre_mix[0]), csh1, csc1,
                           w_in_b[:, QK_WIDTH:QK_WIDTH + 2 * QK_WIDTH], a_f_lane, a_b_lane)

    q, k, v, gate, f = _inproj(x, sh1, sc1, row(norm_pre_mix[0]), w_in_b, cos, sin, tm)
    ret = _retention(q, k, v, gate, stf, stb, a_f, a_b, a_f_lane, a_b_lane, row(ret_norm_gain[0]))
    four = _fourier(f, bc, bs, cs, tm)

    x1, h2, qp = _outproj(ret, four, x, g1, sh2, sc2, row(norm_post_mix[0]), row(norm_pre_ffn[0]),
                          w_out[0].astype(BF16), peer_w_query[0].astype(BF16), tm)

    off, wts = _topk(qp.reshape(t, -1), peer_sub_keys_1[0].astype(BF16), peer_sub_keys_2[0].astype(BF16))

    coef = _peer_u(off, h2.reshape(t, d), wts, _pack_table(peer_u[0]))
    out = _peer_v(off, coef, x1.reshape(t, d), g2, row(norm_post_ffn[0]), _pack_table(peer_v[0]), n)
    return out.reshape(b, n, d)
```

```python
import functools

import numpy as np
import jax
import jax.numpy as jnp
from jax import lax
from jax.experimental import pallas as pl
from jax.experimental.pallas import tpu as pltpu

F32 = jnp.float32
BF16 = jnp.bfloat16
I32 = jnp.int32
U32 = jnp.uint32

D_MODEL = 1024
GRID_W = 64
RET_HEADS = 8
RET_DK = 64
RET_WIDTH = 512
QK_WIDTH = 512
RET_CHUNK = 128
ROPE_PAIRS = RET_DK // 4
ROPE_BASE = 10000.0
FNET_GROUPS = 4
FNET_GROUP_DIM = 128
FNET_WIDTH = 512
IN_WIDTH = 2560
PEER_HEADS = 8
PEER_D_KEY = 256
PEER_N_KEYS = 128
PEER_TOPK = 16
N_MOD = 6
EPS = 1e-6

LANES = 128
SUBLANES = 8
VMEM_LIMIT = 56 * 1024 * 1024
N_PAIRS = RET_HEADS // 2
ROW_TILE = 512


def _cparams(sem):
    return pltpu.CompilerParams(dimension_semantics=sem, vmem_limit_bytes=VMEM_LIMIT)


def _rms_rows(x, gain):
    ms = jnp.mean(x * x, axis=-1, keepdims=True)
    return x * lax.rsqrt(ms + EPS) * gain


def _silu(x):
    return x * jax.nn.sigmoid(x)


def _log_gamma(a):
    return jnp.log1p(-jnp.exp(a))


def _ada_kernel(c_ref, w_ref, b_ref, o_ref):
    s = _silu(c_ref[...])
    o_ref[...] = jnp.dot(s, w_ref[...], preferred_element_type=F32) + b_ref[...]


def _ada(c_all, w_ada, b_ada):
    rows = c_all.shape[0]
    width = w_ada.shape[1]
    tn = 1536
    return pl.pallas_call(
        _ada_kernel,
        out_shape=jax.ShapeDtypeStruct((rows, width), F32),
        grid=(width // tn,),
        in_specs=[
            pl.BlockSpec((rows, D_MODEL), lambda j: (0, 0)),
            pl.BlockSpec((D_MODEL, tn), lambda j: (0, j)),
            pl.BlockSpec((1, tn), lambda j: (0, j)),
        ],
        out_specs=pl.BlockSpec((rows, tn), lambda j: (0, j)),
        compiler_params=_cparams(("arbitrary",)),
        name="ada",
    )(c_all, w_ada, b_ada)


def _ctx_kernel(ctx_ref, g_ref, sh_ref, sc_ref, w_ref, af_ref, ab_ref, stf_ref, stb_ref):
    x = ctx_ref[0]
    n = x.shape[0]
    h = (_rms_rows(x, g_ref[...]) * (1.0 + sc_ref[...]) + sh_ref[...]).astype(BF16)
    kv = jnp.dot(h, w_ref[...], preferred_element_type=F32)
    k = kv[:, :QK_WIDTH] * (RET_DK ** -0.5)
    v = kv[:, QK_WIDTH:].astype(BF16)
    pos = lax.broadcasted_iota(I32, (n, QK_WIDTH), 0).astype(F32)
    lgf = _log_gamma(af_ref[...])
    lgb = _log_gamma(ab_ref[...])
    kf = (k * jnp.exp(lgf * (n - 1.0 - pos))).astype(BF16)
    kb = (k * jnp.exp(lgb * pos)).astype(BF16)
    tn = (((0,), (0,)), ((), ()))
    gf = lax.dot_general(kf, v, tn, preferred_element_type=F32)
    gb = lax.dot_general(kb, v, tn, preferred_element_type=F32)
    ri = lax.broadcasted_iota(I32, (LANES, LANES), 0) // RET_DK
    ci = lax.broadcasted_iota(I32, (LANES, LANES), 1) // RET_DK
    bd = ri == ci
    for p in range(N_PAIRS):
        sl = slice(p * LANES, (p + 1) * LANES)
        stf_ref[0, p] = jnp.where(bd, gf[sl, sl], 0.0)
        stb_ref[0, p] = jnp.where(bd, gb[sl, sl], 0.0)


def _ctx_states(ctx, gain, csh, csc, w_kv, a_f, a_b):
    b, n, _ = ctx.shape
    vec = lambda: pl.BlockSpec((1, D_MODEL), lambda i: (0, 0))
    lane = lambda: pl.BlockSpec((1, QK_WIDTH), lambda i: (0, 0))
    st = jax.ShapeDtypeStruct((b, N_PAIRS, LANES, LANES), F32)
    st_spec = lambda: pl.BlockSpec((1, N_PAIRS, LANES, LANES), lambda i: (i, 0, 0, 0))
    return pl.pallas_call(
        _ctx_kernel,
        out_shape=(st, st),
        grid=(b,),
        in_specs=[
            pl.BlockSpec((1, n, D_MODEL), lambda i: (i, 0, 0)),
            vec(), vec(), vec(),
            pl.BlockSpec((D_MODEL, 2 * QK_WIDTH), lambda i: (0, 0)),
            lane(), lane(),
        ],
        out_specs=(st_spec(), st_spec()),
        compiler_params=_cparams(("arbitrary",)),
        name="ctx_states",
    )(ctx, gain, csh, csc, w_kv, a_f, a_b)


def _inproj_kernel(x_ref, sh_ref, sc_ref, g_ref, w_ref, cos_ref, sin_ref,
                   q_ref, k_ref, v_ref, gate_ref, f_ref):
    x = x_ref[0]
    tm = x.shape[0]
    h = (_rms_rows(x, g_ref[...]) * (1.0 + sc_ref[0]) + sh_ref[0]).astype(BF16)

    def proj(lo, hi):
        return jnp.dot(h, w_ref[:, lo:hi], preferred_element_type=F32)

    cos = cos_ref[...]
    sin = sin_ref[...]
    lane = lax.broadcasted_iota(I32, (tm, QK_WIDTH), 1)
    first = (lane % RET_DK) < (RET_DK // 2)

    def rope(t):
        swapped = jnp.where(first,
                            pltpu.roll(t, QK_WIDTH - RET_DK // 2, 1),
                            pltpu.roll(t, RET_DK // 2, 1))
        return t * cos + swapped * sin

    q_ref[0] = rope(proj(0, 512)).astype(BF16)
    k_ref[0] = rope(proj(512, 1024) * (RET_DK ** -0.5)).astype(BF16)
    v_ref[0] = proj(1024, 1536).astype(BF16)
    gate_ref[0] = proj(1536, 2048)
    f_ref[0] = proj(2048, 2560).astype(BF16)


def _inproj(x, sh, sc, gain, w_in, cos, sin, tm):
    b, n, _ = x.shape
    mod = lambda: pl.BlockSpec((1, 1, D_MODEL), lambda i, j: (i, 0, 0))
    tab = lambda: pl.BlockSpec((tm, QK_WIDTH), lambda i, j: (j, 0))
    out = lambda: pl.BlockSpec((1, tm, 512), lambda i, j: (i, j, 0))
    sd = lambda dt: jax.ShapeDtypeStruct((b, n, 512), dt)
    return pl.pallas_call(
        _inproj_kernel,
        out_shape=(sd(BF16), sd(BF16), sd(BF16), sd(F32), sd(BF16)),
        grid=(b, n // tm),
        in_specs=[
            pl.BlockSpec((1, tm, D_MODEL), lambda i, j: (i, j, 0)),
            mod(), mod(),
            pl.BlockSpec((1, D_MODEL), lambda i, j: (0, 0)),
            pl.BlockSpec((D_MODEL, IN_WIDTH), lambda i, j: (0, 0)),
            tab(), tab(),
        ],
        out_specs=(out(), out(), out(), out(), out()),
        compiler_params=_cparams(("arbitrary", "arbitrary")),
        name="inproj",
    )(x, sh, sc, gain, w_in, cos, sin)


def _ret_kernel(af_s, ab_s, q_ref, k_ref, v_ref, gate_ref, stf_ref, stb_ref, af_ref, ab_ref,
                gain_ref, o_ref, mask_scr, qdf_scr, qdb_scr, kdf_scr, kdb_scr, sf_scr, sb_scr,
                sb_store):
    c = RET_CHUNK
    n = q_ref.shape[1]
    nc = n // c
    lgf = _log_gamma(af_ref[...])
    lgb = _log_gamma(ab_ref[...])
    cdf = jnp.exp(lgf * float(c))
    cdb = jnp.exp(lgb * float(c))

    @pl.when(pl.program_id(0) == 0)
    def _():
        r = lax.broadcasted_iota(I32, (c, RET_WIDTH), 0).astype(F32)
        qdf_scr[...] = jnp.exp(lgf * (r + 1.0))
        kdf_scr[...] = jnp.exp(lgf * (c - 1.0 - r))
        qdb_scr[...] = jnp.exp(lgb * (c - r))
        kdb_scr[...] = jnp.exp(lgb * r)
        ri = lax.broadcasted_iota(I32, (c, c), 0)
        ci = lax.broadcasted_iota(I32, (c, c), 1)
        d = (ri - ci).astype(F32)
        for hh in range(RET_HEADS):
            gf = _log_gamma(jnp.full((c, c), af_s[hh], F32))
            gb = _log_gamma(jnp.full((c, c), ab_s[hh], F32))
            mask_scr[hh] = jnp.where(d >= 0.0, jnp.exp(gf * jnp.maximum(d, 0.0)),
                                     jnp.exp(gb * jnp.maximum(-d, 0.0)))

    ri = lax.broadcasted_iota(I32, (LANES, LANES), 0) // RET_DK
    ci = lax.broadcasted_iota(I32, (LANES, LANES), 1) // RET_DK
    bd = ri == ci
    bd_ones = jnp.where(bd, 1.0, 0.0).astype(BF16)
    lane_head = lax.broadcasted_iota(I32, (c, LANES), 1) // RET_DK
    tn = (((0,), (0,)), ((), ()))
    nt = (((1,), (1,)), ((), ()))

    for p in range(N_PAIRS):
        sf_scr[p] = stf_ref[0, p]
        sb_scr[p] = stb_ref[0, p]

    def bwd_states(i, carry):
        cc = nc - 1 - i
        rows = pl.ds(pl.multiple_of(cc * c, c), c)
        for p in range(N_PAIRS):
            sl = slice(p * LANES, (p + 1) * LANES)
            sb = sb_scr[p]
            sb_store[cc, p] = sb.astype(BF16)
            kw = (k_ref[0, rows, sl].astype(F32) * kdb_scr[:, sl]).astype(BF16)
            g = lax.dot_general(kw, v_ref[0, rows, sl], tn, preferred_element_type=F32)
            sb_scr[p] = sb * cdb[:, sl] + jnp.where(bd, g, 0.0)
        return carry

    lax.fori_loop(0, nc, bwd_states, 0)

    def fwd(cc, carry):
        rows = pl.ds(pl.multiple_of(cc * c, c), c)
        outs, states = [], []
        for p in range(N_PAIRS):
            sl = slice(p * LANES, (p + 1) * LANES)
            qp = q_ref[0, rows, sl]
            kp = k_ref[0, rows, sl]
            vp = v_ref[0, rows, sl]
            qf = qp.astype(F32)
            sf = sf_scr[p]
            q_dec = jnp.concatenate([(qf * qdf_scr[:, sl]).astype(BF16), (qf * qdb_scr[:, sl]).astype(BF16)], axis=1)
            s_both = jnp.concatenate([sf.astype(BF16), sb_store[cc, p]], axis=0)
            acc = jnp.dot(q_dec, s_both, preferred_element_type=F32)
            zero = jnp.zeros_like(qp)
            q_heads = jnp.concatenate([jnp.where(lane_head == j, qp, zero) for j in range(2)], axis=0)
            s = lax.dot_general(q_heads, kp, nt, preferred_element_type=F32)
            pm = jnp.concatenate([(s[j * c:(j + 1) * c] * mask_scr[2 * p + j]).astype(BF16) for j in range(2)], axis=1)
            v_heads = jnp.concatenate([jnp.where(lane_head == j, vp, zero) for j in range(2)], axis=0)
            acc += jnp.dot(pm, v_heads, preferred_element_type=F32)
            sq_hi, sq_lo = _split_bf16(acc * acc)
            ms = jnp.dot(jnp.concatenate([sq_hi, sq_lo], axis=1), jnp.concatenate([bd_ones, bd_ones], axis=0),
                         preferred_element_type=F32) * (1.0 / RET_DK)
            y = acc * lax.rsqrt(ms + EPS) * gain_ref[:, sl] * _silu(gate_ref[0, rows, sl])
            outs.append(y.astype(BF16))
            kw = (kp.astype(F32) * kdf_scr[:, sl]).astype(BF16)
            g = lax.dot_general(kw, vp, tn, preferred_element_type=F32)
            states.append(sf * cdf[:, sl] + jnp.where(bd, g, 0.0))
        for p in range(N_PAIRS):
            o_ref[0, rows, p * LANES:(p + 1) * LANES] = outs[p]
            sf_scr[p] = states[p]
        return carry

    lax.fori_loop(0, nc, fwd, 0)


def _retention(q, k, v, gate, stf, stb, a_f, a_b, a_f_lane, a_b_lane, gain):
    b, n, _ = q.shape
    nc = n // RET_CHUNK
    seq = lambda: pl.BlockSpec((1, n, RET_WIDTH), lambda i, *_: (i, 0, 0))
    st = lambda: pl.BlockSpec((1, N_PAIRS, LANES, LANES), lambda i, *_: (i, 0, 0, 0))
    lane = lambda: pl.BlockSpec((1, RET_WIDTH), lambda i, *_: (0, 0))
    grid_spec = pltpu.PrefetchScalarGridSpec(
        num_scalar_prefetch=2,
        grid=(b,),
        in_specs=[seq(), seq(), seq(), seq(), st(), st(), lane(), lane(), lane()],
        out_specs=seq(),
        scratch_shapes=[
            pltpu.VMEM((RET_HEADS, RET_CHUNK, RET_CHUNK), F32),
            pltpu.VMEM((RET_CHUNK, RET_WIDTH), F32),
            pltpu.VMEM((RET_CHUNK, RET_WIDTH), F32),
            pltpu.VMEM((RET_CHUNK, RET_WIDTH), F32),
            pltpu.VMEM((RET_CHUNK, RET_WIDTH), F32),
            pltpu.VMEM((N_PAIRS, LANES, LANES), F32),
            pltpu.VMEM((N_PAIRS, LANES, LANES), F32),
            pltpu.VMEM((nc, N_PAIRS, LANES, LANES), BF16),
        ],
    )
    return pl.pallas_call(
        _ret_kernel,
        out_shape=jax.ShapeDtypeStruct((b, n, RET_WIDTH), BF16),
        grid_spec=grid_spec,
        compiler_params=_cparams(("arbitrary",)),
        name="retention",
    )(a_f, a_b, q, k, v, gate, stf, stb, a_f_lane, a_b_lane, gain)


def _four_kernel(f_ref, bc_ref, bs_ref, cs_ref, o_ref, z_scr):
    n = f_ref.shape[1]

    @pl.when(pl.program_id(1) == 0)
    def _():
        f = f_ref[0]
        z_scr[0:n, :] = jnp.dot(f, bc_ref[...], preferred_element_type=F32).astype(BF16)
        z_scr[n:2 * n, :] = jnp.dot(f, bs_ref[...], preferred_element_type=F32).astype(BF16)

    o_ref[0] = jnp.dot(cs_ref[...], z_scr[...], preferred_element_type=F32).astype(BF16)


def _fourier(f, bc, bs, cs, tn):
    b, n, _ = f.shape
    return pl.pallas_call(
        _four_kernel,
        out_shape=jax.ShapeDtypeStruct((b, n, FNET_WIDTH), BF16),
        grid=(b, n // tn),
        in_specs=[
            pl.BlockSpec((1, n, FNET_WIDTH), lambda i, j: (i, 0, 0)),
            pl.BlockSpec((FNET_WIDTH, FNET_WIDTH), lambda i, j: (0, 0)),
            pl.BlockSpec((FNET_WIDTH, FNET_WIDTH), lambda i, j: (0, 0)),
            pl.BlockSpec((tn, 2 * n), lambda i, j: (j, 0)),
        ],
        out_specs=pl.BlockSpec((1, tn, FNET_WIDTH), lambda i, j: (i, j, 0)),
        scratch_shapes=[pltpu.VMEM((2 * n, FNET_WIDTH), BF16)],
        compiler_params=_cparams(("arbitrary", "arbitrary")),
        name="fourier",
    )(f, bc, bs, cs)


def _outp_kernel(ret_ref, four_ref, x_ref, g1_ref, sh2_ref, sc2_ref, gpost_ref, gpre_ref,
                 wo_ref, wq_ref, x1_ref, h2_ref, qp_ref):
    mix = jnp.dot(ret_ref[0], wo_ref[0:RET_WIDTH, :], preferred_element_type=F32)
    mix += jnp.dot(four_ref[0], wo_ref[RET_WIDTH:, :], preferred_element_type=F32)
    x1 = x_ref[0] + g1_ref[0] * _rms_rows(mix, gpost_ref[...])
    x1_ref[0] = x1
    h2 = _rms_rows(x1, gpre_ref[...]) * (1.0 + sc2_ref[0]) + sh2_ref[0]
    h2_ref[0] = h2
    qp_ref[0] = jnp.dot(h2.astype(BF16), wq_ref[...], preferred_element_type=F32).astype(BF16)


def _outproj(ret, four, x, g1, sh2, sc2, gpost, gpre, w_out, w_q, tm):
    b, n, _ = x.shape
    qw = w_q.shape[1]
    mod = lambda: pl.BlockSpec((1, 1, D_MODEL), lambda i, j: (i, 0, 0))
    vec = lambda: pl.BlockSpec((1, D_MODEL), lambda i, j: (0, 0))
    half = lambda: pl.BlockSpec((1, tm, 512), lambda i, j: (i, j, 0))
    full = lambda: pl.BlockSpec((1, tm, D_MODEL), lambda i, j: (i, j, 0))
    return pl.pallas_call(
        _outp_kernel,
        out_shape=(jax.ShapeDtypeStruct((b, n, D_MODEL), F32),
                   jax.ShapeDtypeStruct((b, n, D_MODEL), F32),
                   jax.ShapeDtypeStruct((b, n, qw), BF16)),
        grid=(b, n // tm),
        in_specs=[half(), half(), full(), mod(), mod(), mod(), vec(), vec(),
                  pl.BlockSpec(w_out.shape, lambda i, j: (0, 0)),
                  pl.BlockSpec(w_q.shape, lambda i, j: (0, 0))],
        out_specs=(full(), full(), pl.BlockSpec((1, tm, qw), lambda i, j: (i, j, 0))),
        compiler_params=_cparams(("arbitrary", "arbitrary")),
        name="outproj",
    )(ret, four, x, g1, sh2, sc2, gpost, gpre, w_out, w_q)


def _sorting_network(n):
    pairs = []
    p = 1
    while p < n:
        k = p
        while k >= 1:
            for j in range(k % p, n - k, 2 * k):
                for i in range(min(k, n - j - k)):
                    if (i + j) // (2 * p) == (i + j + k) // (2 * p):
                        pairs.append((i + j, i + j + k))
            k //= 2
        p *= 2
    return pairs


def _top_rows_by_columns(s, kk):
    rows, tq = s.shape
    nv = rows // SUBLANES
    assert nv == kk
    sub = lax.broadcasted_iota(I32, (SUBLANES, tq), 0).astype(F32)
    v = [s[k * SUBLANES:(k + 1) * SUBLANES] for k in range(nv)]
    ids = [sub + float(k * SUBLANES) for k in range(nv)]
    for a, b in _sorting_network(nv):
        swap = (v[b] > v[a]) | ((v[b] == v[a]) & (ids[b] < ids[a]))
        v[a], v[b] = jnp.where(swap, v[b], v[a]), jnp.where(swap, v[a], v[b])
        ids[a], ids[b] = jnp.where(swap, ids[b], ids[a]), jnp.where(swap, ids[a], ids[b])
    slot = lax.broadcasted_iota(I32, (kk, tq), 0)
    vals = jnp.zeros((kk, tq), F32)
    idxs = jnp.zeros((kk, tq), F32)
    for it in range(kk):
        m = jnp.max(v[0], axis=0, keepdims=True)
        am = jnp.min(jnp.where(v[0] == m, ids[0], float(rows)), axis=0, keepdims=True)
        vals = jnp.where(slot == it, m, vals)
        idxs = jnp.where(slot == it, am, idxs)
        hit = ids[0] == am
        for k in range(kk - 1 - it):
            v[k] = jnp.where(hit, v[k + 1], v[k])
            ids[k] = jnp.where(hit, ids[k + 1], ids[k])
    return vals, idxs


def _pick_rows(table, sel, kk):
    out = jnp.zeros_like(table)
    for i in range(kk):
        out = jnp.where(sel == float(i), table[i:i + 1, :], out)
    return out


def _top_pairs(v1, v2):
    kk = PEER_TOPK
    tq = v1.shape[1]
    big = float(2 ** 20)
    sub = lax.broadcasted_iota(I32, (SUBLANES, tq), 0)
    subf = sub.astype(F32)
    near = v1[0:SUBLANES]
    depth = []
    for k in range(kk):
        reach = kk // (k + 1)
        row = near + v2[k:k + 1]
        depth.append(row if reach >= SUBLANES else jnp.where(sub < reach, row, -jnp.inf))
    far = v1[SUBLANES:kk] + v2[0:1]
    id_near = subf * float(kk)
    id_far = (subf + float(SUBLANES)) * float(kk)
    slot = lax.broadcasted_iota(I32, (kk, tq), 0)
    vals = jnp.zeros((kk, tq), F32)
    idxs = jnp.zeros((kk, tq), F32)
    for it in range(kk):
        m = jnp.max(jnp.maximum(depth[0], far), axis=0, keepdims=True)
        am = jnp.min(jnp.minimum(jnp.where(depth[0] == m, id_near, big), jnp.where(far == m, id_far, big)),
                     axis=0, keepdims=True)
        vals = jnp.where(slot == it, m, vals)
        idxs = jnp.where(slot == it, am, idxs)
        hit_near = id_near == am
        for k in range(kk - 1 - it):
            depth[k] = jnp.where(hit_near, depth[k + 1], depth[k])
        id_near = jnp.where(hit_near, id_near + 1.0, id_near)
        far = jnp.where(id_far == am, -jnp.inf, far)
    return vals, idxs


def _topk_kernel(qp_ref, k1_ref, k2_ref, off_ref, wts_ref):
    kk = PEER_TOPK
    half = PEER_D_KEY // 2
    nt = (((1,), (1,)), ((), ()))
    k1 = k1_ref[...]
    k2 = k2_ref[...]
    ids_all, w_all = [], []
    for hh in range(PEER_HEADS):
        q1 = qp_ref[:, hh * PEER_D_KEY: hh * PEER_D_KEY + half]
        q2 = qp_ref[:, hh * PEER_D_KEY + half: (hh + 1) * PEER_D_KEY]
        s1 = lax.dot_general(k1, q1, nt, preferred_element_type=F32)
        s2 = lax.dot_general(k2, q2, nt, preferred_element_type=F32)
        v1, i1 = _top_rows_by_columns(s1, kk)
        v2, i2 = _top_rows_by_columns(s2, kk)
        tv, ti = _top_pairs(v1, v2)
        hi = jnp.floor(ti * (1.0 / kk))
        lo = ti - hi * kk
        ids = _pick_rows(i1, hi, kk) * float(PEER_N_KEYS) + _pick_rows(i2, lo, kk)
        p = jnp.exp(tv - tv[0:1, :])
        w = p / jnp.sum(p, axis=0, keepdims=True)
        ids_all.append(ids)
        w_all.append(w)
    ids_t = jnp.concatenate(ids_all, axis=0).T.astype(I32)
    slot = lax.broadcasted_iota(I32, ids_t.shape, 1)
    off_ref[...] = (ids_t * TABLE_ROWS_PER_EXPERT
                    + jnp.where((slot % SUBLANES) < 4, TABLE_PAD_ROWS, TABLE_PAD_ROWS - TABLE_ROWS_PER_EXPERT))
    wts_ref[...] = jnp.concatenate(w_all, axis=0).T


def _topk(qp, k1, k2):
    t, qw = qp.shape
    tq = LANES
    out = lambda: pl.BlockSpec((tq, PEER_SLOTS), lambda i: (i, 0))
    key = lambda: pl.BlockSpec((PEER_N_KEYS, PEER_D_KEY // 2), lambda i: (0, 0))
    return pl.pallas_call(
        _topk_kernel,
        out_shape=(jax.ShapeDtypeStruct((t, PEER_SLOTS), I32),
                   jax.ShapeDtypeStruct((t, PEER_SLOTS), F32)),
        grid=(t // tq,),
        in_specs=[pl.BlockSpec((tq, qw), lambda i: (i, 0)), key(), key()],
        out_specs=(out(), out()),
        compiler_params=_cparams(("arbitrary",)),
        name="peer_topk",
    )(qp, k1, k2)


PEER_SLOTS = PEER_HEADS * PEER_TOPK
PEER_TOKENS_PER_STEP = 256
PEER_TOKENS_PER_HALF = PEER_TOKENS_PER_STEP // 2
TABLE_ROWS_PER_EXPERT = 4
PACK_EXPERTS_PER_STEP = 256
TABLE_PAD_ROWS = PACK_EXPERTS_PER_STEP * TABLE_ROWS_PER_EXPERT
HI_MASK = np.uint32(0xFFFF0000)


def _pack_kernel(w_ref, o_ref):
    step = pl.program_id(0)
    last = pl.num_programs(0) - 1

    @pl.when((step == 0) | (step == last))
    def _():
        o_ref[...] = jnp.zeros(o_ref.shape, U32)

    @pl.when((step > 0) & (step < last))
    def _():
        half = D_MODEL // 2
        x = w_ref[...].astype(BF16).astype(F32)
        bits = lax.bitcast_convert_type(x, U32)
        words = (bits[:, :half] >> 16) | (bits[:, half:] & HI_MASK)
        for s in range(TABLE_ROWS_PER_EXPERT):
            rows = pl.ds(s, PACK_EXPERTS_PER_STEP, stride=TABLE_ROWS_PER_EXPERT)
            o_ref[rows, :] = words[:, s * LANES:(s + 1) * LANES]


def _pack_table(w):
    e = w.shape[0]
    nblk = e // PACK_EXPERTS_PER_STEP
    return pl.pallas_call(
        _pack_kernel,
        out_shape=jax.ShapeDtypeStruct(((nblk + 2) * TABLE_PAD_ROWS, LANES), U32),
        grid=(nblk + 2,),
        in_specs=[pl.BlockSpec((PACK_EXPERTS_PER_STEP, D_MODEL), lambda i: (jnp.clip(i - 1, 0, nblk - 1), 0))],
        out_specs=pl.BlockSpec((TABLE_PAD_ROWS, LANES), lambda i: (i, 0)),
        compiler_params=_cparams(("arbitrary",)),
        name="pack_table",
    )(w)


def _expert_pair(tbl_ref, off_s, j, g, i, low4):
    wa = tbl_ref[pl.ds(off_s[j, g * SUBLANES + i], SUBLANES), :]
    wb = tbl_ref[pl.ds(off_s[j, g * SUBLANES + i + 4], SUBLANES), :]
    return jnp.where(low4, wa, wb)


def _for_token_halves(off_hbm, bufs, sem, half_body):
    step = pl.program_id(0)
    last = pl.num_programs(0) - 1

    def fill(at_step, half):
        start = at_step * PEER_TOKENS_PER_STEP + half * PEER_TOKENS_PER_HALF
        return pltpu.make_async_copy(off_hbm.at[pl.ds(start, PEER_TOKENS_PER_HALF)], bufs[half], sem.at[half])

    @pl.when(step == 0)
    def _():
        for half in range(2):
            fill(0, half).start()

    for half in range(2):
        fill(step, half).wait()
        half_body(half, bufs[half])

        @pl.when(step < last)
        def _():
            fill(step + 1, half).start()


def _offset_scratch():
    return [pltpu.SMEM((PEER_TOKENS_PER_HALF, PEER_SLOTS), I32),
            pltpu.SMEM((PEER_TOKENS_PER_HALF, PEER_SLOTS), I32),
            pltpu.SemaphoreType.DMA((2,))]


def _gelu_exact(x):
    return 0.5 * x * (1.0 + lax.erf(x * (2.0 ** -0.5)))


def _peer_u_slot_sum():
    col = np.arange(PEER_SLOTS * TABLE_ROWS_PER_EXPERT)
    q, r = col // SUBLANES, col % SUBLANES
    slot = SUBLANES * (q // 4) + (q % 4) + np.where(r < 4, 0, 4)
    return jnp.asarray((slot[:, None] == np.arange(PEER_SLOTS)[None, :]).astype(np.float32), BF16)


def _split_bf16(x):
    hi = x.astype(BF16)
    return hi, (x - hi.astype(F32)).astype(BF16)


def _peer_u_kernel(off_hbm, h_ref, wts_ref, slotsum_ref, tbl_ref, coef_ref, off_a, off_b, sem):
    nt = (((1,), (1,)), ((), ()))
    ones = jnp.ones((SUBLANES, LANES), BF16)
    sub = lax.broadcasted_iota(I32, (SUBLANES, LANES), 0)
    low4 = sub < 4
    r4 = sub % TABLE_ROWS_PER_EXPERT

    def token_row_sums(off_s, jj, j):
        def chunk(c):
            return jnp.broadcast_to(h_ref[j:j + 1, c * LANES:(c + 1) * LANES], (SUBLANES, LANES))

        def by_row(c0):
            return jnp.where(r4 == 0, chunk(c0), jnp.where(r4 == 1, chunk(c0 + 1),
                             jnp.where(r4 == 2, chunk(c0 + 2), chunk(c0 + 3))))

        hlo = by_row(0)
        hhi = by_row(TABLE_ROWS_PER_EXPERT)
        prods = []
        for q in range(PEER_SLOTS // 2):
            w = _expert_pair(tbl_ref, off_s, jj, q // 4, q % 4, low4)
            lo = lax.bitcast_convert_type(w << 16, F32)
            hi = lax.bitcast_convert_type(w & HI_MASK, F32)
            prods.append(lo * hlo + hi * hhi)
        stacked = jnp.concatenate(prods, axis=0).astype(BF16)
        return lax.dot_general(ones, stacked, nt, preferred_element_type=F32)[0:1, :]

    def finish(half, sums):
        hi, lo = _split_bf16(sums)
        acts = (jnp.dot(hi, slotsum_ref[...], preferred_element_type=F32)
                + jnp.dot(lo, slotsum_ref[...], preferred_element_type=F32))
        rows = pl.ds(half * PEER_TOKENS_PER_HALF, PEER_TOKENS_PER_HALF)
        coef_ref[rows, :] = wts_ref[rows, :] * _gelu_exact(acts)

    pending = []

    def half_body(half, off_s):
        while pending:
            finish(*pending.pop())
        base = half * PEER_TOKENS_PER_HALF
        sums = [token_row_sums(off_s, jj, base + jj) for jj in range(PEER_TOKENS_PER_HALF)]
        pending.append((half, jnp.concatenate(sums, axis=0)))

    _for_token_halves(off_hbm, (off_a, off_b), sem, half_body)
    finish(*pending.pop())


def _peer_u(off, h, wts, tbl):
    t = h.shape[0]
    tb = PEER_TOKENS_PER_STEP
    slotsum = _peer_u_slot_sum()
    return pl.pallas_call(
        _peer_u_kernel,
        out_shape=jax.ShapeDtypeStruct((t, PEER_SLOTS), F32),
        grid=(t // tb,),
        in_specs=[pl.BlockSpec(memory_space=pl.ANY),
                  pl.BlockSpec((tb, D_MODEL), lambda i: (i, 0)),
                  pl.BlockSpec((tb, PEER_SLOTS), lambda i: (i, 0)),
                  pl.BlockSpec(slotsum.shape, lambda i: (0, 0)),
                  pl.BlockSpec(memory_space=pltpu.VMEM)],
        out_specs=pl.BlockSpec((tb, PEER_SLOTS), lambda i: (i, 0)),
        scratch_shapes=_offset_scratch(),
        compiler_params=_cparams(("arbitrary",)),
        name="peer_u",
    )(off, h, wts, slotsum, tbl)


def _peer_v_tables():
    col = np.arange(PEER_SLOTS * SUBLANES)
    q, r = col // 16, col % 16
    slot = SUBLANES * (q // 4) + (q % 4) + np.where(r < 8, 0, 4)
    expand = (slot[None, :] == np.arange(PEER_SLOTS)[:, None]).astype(np.float32)
    out_row = (r // 2) % 4 + 4 * (r % 2)
    rowsel = (out_row[None, :] == np.arange(SUBLANES)[:, None]).astype(np.float32)
    return jnp.asarray(expand, BF16), jnp.asarray(rowsel, F32)


def _peer_v_kernel(off_hbm, coef_ref, x1_ref, g2_ref, gain_ref, expand_ref, rowsel_ref, tbl_ref, o_ref,
                   out_scr, off_a, off_b, sem):
    sub = lax.broadcasted_iota(I32, (SUBLANES, LANES), 0)
    low4 = sub < 4
    cexp = jnp.dot(coef_ref[...].astype(BF16), expand_ref[...], preferred_element_type=F32)
    rowsel = rowsel_ref[...]

    def half_body(half, off_s):
        for jj in range(PEER_TOKENS_PER_HALF):
            j = half * PEER_TOKENS_PER_HALF + jj
            tiles = []
            for q in range(PEER_SLOTS // 2):
                w = _expert_pair(tbl_ref, off_s, jj, q // 4, q % 4, low4)
                tiles.append(pltpu.bitcast(w, BF16))
            wmat = jnp.concatenate(tiles, axis=0)
            c = (cexp[j:j + 1, :] * rowsel).astype(BF16)
            out = jnp.dot(c, wmat, preferred_element_type=F32)
            for s in range(SUBLANES):
                out_scr[j:j + 1, s * LANES:(s + 1) * LANES] = out[s:s + 1, :]

    _for_token_halves(off_hbm, (off_a, off_b), sem, half_body)
    y = _rms_rows(out_scr[...], gain_ref[...])
    o_ref[...] = x1_ref[...] + g2_ref[0] * y


def _peer_v(off, coef, x1, g2, gain, tbl, tokens_per_sample):
    t = x1.shape[0]
    tb = PEER_TOKENS_PER_STEP
    steps_per_sample = tokens_per_sample // tb
    expand, rowsel = _peer_v_tables()
    tok = lambda: pl.BlockSpec((tb, D_MODEL), lambda i: (i, 0))
    return pl.pallas_call(
        _peer_v_kernel,
        out_shape=jax.ShapeDtypeStruct((t, D_MODEL), F32),
        grid=(t // tb,),
        in_specs=[pl.BlockSpec(memory_space=pl.ANY),
                  pl.BlockSpec((tb, PEER_SLOTS), lambda i: (i, 0)),
                  tok(),
                  pl.BlockSpec((1, 1, D_MODEL), lambda i: (i // steps_per_sample, 0, 0)),
                  pl.BlockSpec((1, D_MODEL), lambda i: (0, 0)),
                  pl.BlockSpec(expand.shape, lambda i: (0, 0)),
                  pl.BlockSpec(rowsel.shape, lambda i: (0, 0)),
                  pl.BlockSpec(memory_space=pltpu.VMEM)],
        out_specs=tok(),
        scratch_shapes=[pltpu.VMEM((tb, D_MODEL), F32)] + _offset_scratch(),
        compiler_params=_cparams(("arbitrary",)),
        name="peer_v",
    )(off, coef, x1, g2, gain, expand, rowsel, tbl)


@functools.lru_cache(maxsize=None)
def _rope_tables_np(n):
    rows = n // GRID_W
    row_ids = np.repeat(np.arange(rows, dtype=np.float64), GRID_W)
    col_ids = np.tile(np.arange(GRID_W, dtype=np.float64), rows)
    freqs = ROPE_BASE ** (-np.arange(ROPE_PAIRS, dtype=np.float64) / ROPE_PAIRS)
    ang = np.concatenate([row_ids[:, None] * freqs[None, :], col_ids[:, None] * freqs[None, :]], axis=-1)
    cos, sin = np.cos(ang), np.sin(ang)
    cos_h = np.concatenate([cos, cos], axis=-1)
    sin_h = np.concatenate([-sin, sin], axis=-1)
    return (np.tile(cos_h, (1, RET_HEADS)).astype(np.float32),
            np.tile(sin_h, (1, RET_HEADS)).astype(np.float32))


@functools.lru_cache(maxsize=None)
def _dft_tables_np(n):
    scale = (n * FNET_GROUP_DIM) ** -0.5
    jn = np.arange(n, dtype=np.int64)
    ang_n = ((jn[:, None] * jn[None, :]) % n).astype(np.float64) * (2.0 * np.pi / n)
    cs = np.concatenate([np.cos(ang_n), -np.sin(ang_n)], axis=1).astype(np.float32)
    jc = np.arange(FNET_GROUP_DIM, dtype=np.int64)
    ang_c = ((jc[:, None] * jc[None, :]) % FNET_GROUP_DIM).astype(np.float64) * (2.0 * np.pi / FNET_GROUP_DIM)
    eye = np.eye(FNET_GROUPS)
    bc = np.kron(eye, np.cos(ang_c) * scale).astype(np.float32)
    bs = np.kron(eye, np.sin(ang_c) * scale).astype(np.float32)
    return bc, bs, cs


def kernel(x, c, ctx, c_ctx, w_ada, b_ada, norm_pre_mix, norm_post_mix, w_in, ret_decay_fwd, ret_decay_bwd, ret_norm_gain, w_out, norm_pre_ffn, norm_post_ffn, peer_w_query, peer_sub_keys_1, peer_sub_keys_2, peer_u, peer_v):
    b, n, d = x.shape
    depth = w_ada.shape[0]
    assert depth == 1 and d == D_MODEL and n % RET_CHUNK == 0 and n % GRID_W == 0
    t = b * n
    tm = min(ROW_TILE, n)
    row = lambda a: a.reshape(1, -1)

    cos, sin = [jnp.asarray(a) for a in _rope_tables_np(n)]
    bc, bs, cs = [jnp.asarray(a).astype(BF16) for a in _dft_tables_np(n)]

    pad = (-(b + 1)) % SUBLANES
    c_all = jnp.concatenate([c, c_ctx[None, :], jnp.zeros((pad, d), F32)], axis=0)
    mod = _ada(c_all, w_ada[0], row(b_ada[0]))
    sh1, sc1, g1, sh2, sc2, g2 = [mod[:b, i * d:(i + 1) * d].reshape(b, 1, d) for i in range(N_MOD)]
    csh1 = mod[b:b + 1, 0:d]
    csc1 = mod[b:b + 1, d:2 * d]

    a_f, a_b = ret_decay_fwd[0], ret_decay_bwd[0]
    a_f_lane = row(jnp.repeat(a_f, RET_DK))
    a_b_lane = row(jnp.repeat(a_b, RET_DK))

    w_in_b = w_in[0].astype(BF16)
    stf, stb = _ctx_states(ctx, row(norm_pre_mix[0]), csh1, csc1,
                           w_in_b[:, QK_WIDTH:QK_WIDTH + 2 * QK_WIDTH], a_f_lane, a_b_lane)

    q, k, v, gate, f = _inproj(x, sh1, sc1, row(norm_pre_mix[0]), w_in_b, cos, sin, tm)
    ret = _retention(q, k, v, gate, stf, stb, a_f, a_b, a_f_lane, a_b_lane, row(ret_norm_gain[0]))
    four = _fourier(f, bc, bs, cs, tm)

    x1, h2, qp = _outproj(ret, four, x, g1, sh2, sc2, row(norm_post_mix[0]), row(norm_pre_ffn[0]),
                          w_out[0].astype(BF16), peer_w_query[0].astype(BF16), tm)

    off, wts = _topk(qp.reshape(t, -1), peer_sub_keys_1[0].astype(BF16), peer_sub_keys_2[0].astype(BF16))

    coef = _peer_u(off, h2.reshape(t, d), wts, _pack_table(peer_u[0]))
    out = _peer_v(off, coef, x1.reshape(t, d), g2, row(norm_post_ffn[0]), _pack_table(peer_v[0]), n)
    return out.reshape(b, n, d)
```

```python
import functools

import numpy as np
import jax
import jax.numpy as jnp
from jax import lax
from jax.experimental import pallas as pl
from jax.experimental.pallas import tpu as pltpu

F32 = jnp.float32
BF16 = jnp.bfloat16
I32 = jnp.int32
U32 = jnp.uint32

D_MODEL = 1024
GRID_W = 64
RET_HEADS = 8
RET_DK = 64
RET_WIDTH = 512
QK_WIDTH = 512
RET_CHUNK = 128
ROPE_PAIRS = RET_DK // 4
ROPE_BASE = 10000.0
FNET_GROUPS = 4
FNET_GROUP_DIM = 128
FNET_WIDTH = 512
IN_WIDTH = 2560
PEER_HEADS = 8
PEER_D_KEY = 256
PEER_N_KEYS = 128
PEER_TOPK = 16
N_MOD = 6
EPS = 1e-6

LANES = 128
SUBLANES = 8
VMEM_LIMIT = 56 * 1024 * 1024
N_PAIRS = RET_HEADS // 2
ROW_TILE = 1024


def _cparams(sem):
    return pltpu.CompilerParams(dimension_semantics=sem, vmem_limit_bytes=VMEM_LIMIT)


def _rms_rows(x, gain):
    ms = jnp.mean(x * x, axis=-1, keepdims=True)
    return x * lax.rsqrt(ms + EPS) * gain


def _silu(x):
    return x * jax.nn.sigmoid(x)


def _log_gamma(a):
    return jnp.log1p(-jnp.exp(a))


def _ada_kernel(c_ref, w_ref, b_ref, o_ref):
    s = _silu(c_ref[...])
    o_ref[...] = jnp.dot(s, w_ref[...], preferred_element_type=F32) + b_ref[...]


def _ada(c_all, w_ada, b_ada):
    rows = c_all.shape[0]
    width = w_ada.shape[1]
    tn = 1536
    return pl.pallas_call(
        _ada_kernel,
        out_shape=jax.ShapeDtypeStruct((rows, width), F32),
        grid=(width // tn,),
        in_specs=[
            pl.BlockSpec((rows, D_MODEL), lambda j: (0, 0)),
            pl.BlockSpec((D_MODEL, tn), lambda j: (0, j)),
            pl.BlockSpec((1, tn), lambda j: (0, j)),
        ],
        out_specs=pl.BlockSpec((rows, tn), lambda j: (0, j)),
        compiler_params=_cparams(("arbitrary",)),
        name="ada",
    )(c_all, w_ada, b_ada)


def _ctx_kernel(ctx_ref, g_ref, sh_ref, sc_ref, w_ref, af_ref, ab_ref, stf_ref, stb_ref):
    x = ctx_ref[0]
    n = x.shape[0]
    h = (_rms_rows(x, g_ref[...]) * (1.0 + sc_ref[...]) + sh_ref[...]).astype(BF16)
    kv = jnp.dot(h, w_ref[...], preferred_element_type=F32)
    k = kv[:, :QK_WIDTH] * (RET_DK ** -0.5)
    v = kv[:, QK_WIDTH:].astype(BF16)
    pos = lax.broadcasted_iota(I32, (n, QK_WIDTH), 0).astype(F32)
    lgf = _log_gamma(af_ref[...])
    lgb = _log_gamma(ab_ref[...])
    kf = (k * jnp.exp(lgf * (n - 1.0 - pos))).astype(BF16)
    kb = (k * jnp.exp(lgb * pos)).astype(BF16)
    tn = (((0,), (0,)), ((), ()))
    gf = lax.dot_general(kf, v, tn, preferred_element_type=F32)
    gb = lax.dot_general(kb, v, tn, preferred_element_type=F32)
    ri = lax.broadcasted_iota(I32, (LANES, LANES), 0) // RET_DK
    ci = lax.broadcasted_iota(I32, (LANES, LANES), 1) // RET_DK
    bd = ri == ci
    for p in range(N_PAIRS):
        sl = slice(p * LANES, (p + 1) * LANES)
        stf_ref[0, p] = jnp.where(bd, gf[sl, sl], 0.0)
        stb_ref[0, p] = jnp.where(bd, gb[sl, sl], 0.0)


def _ctx_states(ctx, gain, csh, csc, w_kv, a_f, a_b):
    b, n, _ = ctx.shape
    vec = lambda: pl.BlockSpec((1, D_MODEL), lambda i: (0, 0))
    lane = lambda: pl.BlockSpec((1, QK_WIDTH), lambda i: (0, 0))
    st = jax.ShapeDtypeStruct((b, N_PAIRS, LANES, LANES), F32)
    st_spec = lambda: pl.BlockSpec((1, N_PAIRS, LANES, LANES), lambda i: (i, 0, 0, 0))
    return pl.pallas_call(
        _ctx_kernel,
        out_shape=(st, st),
        grid=(b,),
        in_specs=[
            pl.BlockSpec((1, n, D_MODEL), lambda i: (i, 0, 0)),
            vec(), vec(), vec(),
            pl.BlockSpec((D_MODEL, 2 * QK_WIDTH), lambda i: (0, 0)),
            lane(), lane(),
        ],
        out_specs=(st_spec(), st_spec()),
        compiler_params=_cparams(("arbitrary",)),
        name="ctx_states",
    )(ctx, gain, csh, csc, w_kv, a_f, a_b)


def _inproj_kernel(x_ref, sh_ref, sc_ref, g_ref, w_ref, cos_ref, sin_ref,
                   q_ref, k_ref, v_ref, gate_ref, f_ref):
    x = x_ref[0]
    tm = x.shape[0]
    h = (_rms_rows(x, g_ref[...]) * (1.0 + sc_ref[0]) + sh_ref[0]).astype(BF16)

    def proj(lo, hi):
        return jnp.dot(h, w_ref[:, lo:hi], preferred_element_type=F32)

    cos = cos_ref[...]
    sin = sin_ref[...]
    lane = lax.broadcasted_iota(I32, (tm, QK_WIDTH), 1)
    first = (lane % RET_DK) < (RET_DK // 2)

    def rope(t):
        swapped = jnp.where(first,
                            pltpu.roll(t, QK_WIDTH - RET_DK // 2, 1),
                            pltpu.roll(t, RET_DK // 2, 1))
        return t * cos + swapped * sin

    q_ref[0] = rope(proj(0, 512)).astype(BF16)
    k_ref[0] = rope(proj(512, 1024) * (RET_DK ** -0.5)).astype(BF16)
    v_ref[0] = proj(1024, 1536).astype(BF16)
    gate_ref[0] = proj(1536, 2048)
    f_ref[0] = proj(2048, 2560).astype(BF16)


def _inproj(x, sh, sc, gain, w_in, cos, sin, tm):
    b, n, _ = x.shape
    mod = lambda: pl.BlockSpec((1, 1, D_MODEL), lambda i, j: (i, 0, 0))
    tab = lambda: pl.BlockSpec((tm, QK_WIDTH), lambda i, j: (j, 0))
    out = lambda: pl.BlockSpec((1, tm, 512), lambda i, j: (i, j, 0))
    sd = lambda dt: jax.ShapeDtypeStruct((b, n, 512), dt)
    return pl.pallas_call(
        _inproj_kernel,
        out_shape=(sd(BF16), sd(BF16), sd(BF16), sd(F32), sd(BF16)),
        grid=(b, n // tm),
        in_specs=[
            pl.BlockSpec((1, tm, D_MODEL), lambda i, j: (i, j, 0)),
            mod(), mod(),
            pl.BlockSpec((1, D_MODEL), lambda i, j: (0, 0)),
            pl.BlockSpec((D_MODEL, IN_WIDTH), lambda i, j: (0, 0)),
            tab(), tab(),
        ],
        out_specs=(out(), out(), out(), out(), out()),
        compiler_params=_cparams(("arbitrary", "arbitrary")),
        name="inproj",
    )(x, sh, sc, gain, w_in, cos, sin)


def _ret_kernel(af_s, ab_s, q_ref, k_ref, v_ref, gate_ref, stf_ref, stb_ref, af_ref, ab_ref,
                gain_ref, o_ref, mask_scr, qdf_scr, qdb_scr, kdf_scr, kdb_scr, sf_scr, sb_scr,
                sb_store):
    c = RET_CHUNK
    n = q_ref.shape[1]
    nc = n // c
    lgf = _log_gamma(af_ref[...])
    lgb = _log_gamma(ab_ref[...])
    cdf = jnp.exp(lgf * float(c))
    cdb = jnp.exp(lgb * float(c))

    @pl.when(pl.program_id(0) == 0)
    def _():
        r = lax.broadcasted_iota(I32, (c, RET_WIDTH), 0).astype(F32)
        qdf_scr[...] = jnp.exp(lgf * (r + 1.0))
        kdf_scr[...] = jnp.exp(lgf * (c - 1.0 - r))
        qdb_scr[...] = jnp.exp(lgb * (c - r))
        kdb_scr[...] = jnp.exp(lgb * r)
        ri = lax.broadcasted_iota(I32, (c, c), 0)
        ci = lax.broadcasted_iota(I32, (c, c), 1)
        d = (ri - ci).astype(F32)
        for hh in range(RET_HEADS):
            gf = _log_gamma(jnp.full((c, c), af_s[hh], F32))
            gb = _log_gamma(jnp.full((c, c), ab_s[hh], F32))
            mask_scr[hh] = jnp.where(d >= 0.0, jnp.exp(gf * jnp.maximum(d, 0.0)),
                                     jnp.exp(gb * jnp.maximum(-d, 0.0)))

    ri = lax.broadcasted_iota(I32, (LANES, LANES), 0) // RET_DK
    ci = lax.broadcasted_iota(I32, (LANES, LANES), 1) // RET_DK
    bd = ri == ci
    bd_ones = jnp.where(bd, 1.0, 0.0).astype(BF16)
    lane_head = lax.broadcasted_iota(I32, (c, LANES), 1) // RET_DK
    tn = (((0,), (0,)), ((), ()))
    nt = (((1,), (1,)), ((), ()))

    for p in range(N_PAIRS):
        sf_scr[p] = stf_ref[0, p]
        sb_scr[p] = stb_ref[0, p]

    def bwd_states(i, carry):
        cc = nc - 1 - i
        rows = pl.ds(pl.multiple_of(cc * c, c), c)
        for p in range(N_PAIRS):
            sl = slice(p * LANES, (p + 1) * LANES)
            sb = sb_scr[p]
            sb_store[cc, p] = sb.astype(BF16)
            kw = (k_ref[0, rows, sl].astype(F32) * kdb_scr[:, sl]).astype(BF16)
            g = lax.dot_general(kw, v_ref[0, rows, sl], tn, preferred_element_type=F32)
            sb_scr[p] = sb * cdb[:, sl] + jnp.where(bd, g, 0.0)
        return carry

    lax.fori_loop(0, nc, bwd_states, 0)

    def fwd(cc, carry):
        rows = pl.ds(pl.multiple_of(cc * c, c), c)
        outs, states = [], []
        for p in range(N_PAIRS):
            sl = slice(p * LANES, (p + 1) * LANES)
            qp = q_ref[0, rows, sl]
            kp = k_ref[0, rows, sl]
            vp = v_ref[0, rows, sl]
            qf = qp.astype(F32)
            sf = sf_scr[p]
            q_dec = jnp.concatenate([(qf * qdf_scr[:, sl]).astype(BF16), (qf * qdb_scr[:, sl]).astype(BF16)], axis=1)
            s_both = jnp.concatenate([sf.astype(BF16), sb_store[cc, p]], axis=0)
            acc = jnp.dot(q_dec, s_both, preferred_element_type=F32)
            zero = jnp.zeros_like(qp)
            q_heads = jnp.concatenate([jnp.where(lane_head == j, qp, zero) for j in range(2)], axis=0)
            s = lax.dot_general(q_heads, kp, nt, preferred_element_type=F32)
            pm = jnp.concatenate([(s[j * c:(j + 1) * c] * mask_scr[2 * p + j]).astype(BF16) for j in range(2)], axis=1)
            v_heads = jnp.concatenate([jnp.where(lane_head == j, vp, zero) for j in range(2)], axis=0)
            acc += jnp.dot(pm, v_heads, preferred_element_type=F32)
            sq_hi, sq_lo = _split_bf16(acc * acc)
            ms = jnp.dot(jnp.concatenate([sq_hi, sq_lo], axis=1), jnp.concatenate([bd_ones, bd_ones], axis=0),
                         preferred_element_type=F32) * (1.0 / RET_DK)
            y = acc * lax.rsqrt(ms + EPS) * gain_ref[:, sl] * _silu(gate_ref[0, rows, sl])
            outs.append(y.astype(BF16))
            kw = (kp.astype(F32) * kdf_scr[:, sl]).astype(BF16)
            g = lax.dot_general(kw, vp, tn, preferred_element_type=F32)
            states.append(sf * cdf[:, sl] + jnp.where(bd, g, 0.0))
        for p in range(N_PAIRS):
            o_ref[0, rows, p * LANES:(p + 1) * LANES] = outs[p]
            sf_scr[p] = states[p]
        return carry

    lax.fori_loop(0, nc, fwd, 0, unroll=2)


def _retention(q, k, v, gate, stf, stb, a_f, a_b, a_f_lane, a_b_lane, gain):
    b, n, _ = q.shape
    nc = n // RET_CHUNK
    seq = lambda: pl.BlockSpec((1, n, RET_WIDTH), lambda i, *_: (i, 0, 0))
    st = lambda: pl.BlockSpec((1, N_PAIRS, LANES, LANES), lambda i, *_: (i, 0, 0, 0))
    lane = lambda: pl.BlockSpec((1, RET_WIDTH), lambda i, *_: (0, 0))
    grid_spec = pltpu.PrefetchScalarGridSpec(
        num_scalar_prefetch=2,
        grid=(b,),
        in_specs=[seq(), seq(), seq(), seq(), st(), st(), lane(), lane(), lane()],
        out_specs=seq(),
        scratch_shapes=[
            pltpu.VMEM((RET_HEADS, RET_CHUNK, RET_CHUNK), F32),
            pltpu.VMEM((RET_CHUNK, RET_WIDTH), F32),
            pltpu.VMEM((RET_CHUNK, RET_WIDTH), F32),
            pltpu.VMEM((RET_CHUNK, RET_WIDTH), F32),
            pltpu.VMEM((RET_CHUNK, RET_WIDTH), F32),
            pltpu.VMEM((N_PAIRS, LANES, LANES), F32),
            pltpu.VMEM((N_PAIRS, LANES, LANES), F32),
            pltpu.VMEM((nc, N_PAIRS, LANES, LANES), BF16),
        ],
    )
    return pl.pallas_call(
        _ret_kernel,
        out_shape=jax.ShapeDtypeStruct((b, n, RET_WIDTH), BF16),
        grid_spec=grid_spec,
        compiler_params=_cparams(("arbitrary",)),
        name="retention",
    )(a_f, a_b, q, k, v, gate, stf, stb, a_f_lane, a_b_lane, gain)


def _four_kernel(f_ref, bc_ref, bs_ref, cs_ref, o_ref, z_scr):
    n = f_ref.shape[1]

    @pl.when(pl.program_id(1) == 0)
    def _():
        f = f_ref[0]
        z_scr[0:n, :] = jnp.dot(f, bc_ref[...], preferred_element_type=F32).astype(BF16)
        z_scr[n:2 * n, :] = jnp.dot(f, bs_ref[...], preferred_element_type=F32).astype(BF16)

    o_ref[0] = jnp.dot(cs_ref[...], z_scr[...], preferred_element_type=F32).astype(BF16)


def _fourier(f, bc, bs, cs, tn):
    b, n, _ = f.shape
    return pl.pallas_call(
        _four_kernel,
        out_shape=jax.ShapeDtypeStruct((b, n, FNET_WIDTH), BF16),
        grid=(b, n // tn),
        in_specs=[
            pl.BlockSpec((1, n, FNET_WIDTH), lambda i, j: (i, 0, 0)),
            pl.BlockSpec((FNET_WIDTH, FNET_WIDTH), lambda i, j: (0, 0)),
            pl.BlockSpec((FNET_WIDTH, FNET_WIDTH), lambda i, j: (0, 0)),
            pl.BlockSpec((tn, 2 * n), lambda i, j: (j, 0)),
        ],
        out_specs=pl.BlockSpec((1, tn, FNET_WIDTH), lambda i, j: (i, j, 0)),
        scratch_shapes=[pltpu.VMEM((2 * n, FNET_WIDTH), BF16)],
        compiler_params=_cparams(("arbitrary", "arbitrary")),
        name="fourier",
    )(f, bc, bs, cs)


def _outp_kernel(ret_ref, four_ref, x_ref, g1_ref, sh2_ref, sc2_ref, gpost_ref, gpre_ref,
                 wo_ref, wq_ref, x1_ref, h2_ref, qp_ref):
    mix = jnp.dot(ret_ref[0], wo_ref[0:RET_WIDTH, :], preferred_element_type=F32)
    mix += jnp.dot(four_ref[0], wo_ref[RET_WIDTH:, :], preferred_element_type=F32)
    x1 = x_ref[0] + g1_ref[0] * _rms_rows(mix, gpost_ref[...])
    x1_ref[0] = x1
    h2 = _rms_rows(x1, gpre_ref[...]) * (1.0 + sc2_ref[0]) + sh2_ref[0]
    h2_ref[0] = h2
    qp_ref[0] = jnp.dot(h2.astype(BF16), wq_ref[...], preferred_element_type=F32).astype(BF16)


def _outproj(ret, four, x, g1, sh2, sc2, gpost, gpre, w_out, w_q, tm):
    b, n, _ = x.shape
    qw = w_q.shape[1]
    mod = lambda: pl.BlockSpec((1, 1, D_MODEL), lambda i, j: (i, 0, 0))
    vec = lambda: pl.BlockSpec((1, D_MODEL), lambda i, j: (0, 0))
    half = lambda: pl.BlockSpec((1, tm, 512), lambda i, j: (i, j, 0))
    full = lambda: pl.BlockSpec((1, tm, D_MODEL), lambda i, j: (i, j, 0))
    return pl.pallas_call(
        _outp_kernel,
        out_shape=(jax.ShapeDtypeStruct((b, n, D_MODEL), F32),
                   jax.ShapeDtypeStruct((b, n, D_MODEL), F32),
                   jax.ShapeDtypeStruct((b, n, qw), BF16)),
        grid=(b, n // tm),
        in_specs=[half(), half(), full(), mod(), mod(), mod(), vec(), vec(),
                  pl.BlockSpec(w_out.shape, lambda i, j: (0, 0)),
                  pl.BlockSpec(w_q.shape, lambda i, j: (0, 0))],
        out_specs=(full(), full(), pl.BlockSpec((1, tm, qw), lambda i, j: (i, j, 0))),
        compiler_params=_cparams(("arbitrary", "arbitrary")),
        name="outproj",
    )(ret, four, x, g1, sh2, sc2, gpost, gpre, w_out, w_q)


def _sorting_network(n):
    pairs = []
    p = 1
    while p < n:
        k = p
        while k >= 1:
            for j in range(k % p, n - k, 2 * k):
                for i in range(min(k, n - j - k)):
                    if (i + j) // (2 * p) == (i + j + k) // (2 * p):
                        pairs.append((i + j, i + j + k))
            k //= 2
        p *= 2
    return pairs


def _top_rows_by_columns(s, kk):
    rows, tq = s.shape
    nv = rows // SUBLANES
    assert nv == kk
    sub = lax.broadcasted_iota(I32, (SUBLANES, tq), 0).astype(F32)
    v = [s[k * SUBLANES:(k + 1) * SUBLANES] for k in range(nv)]
    ids = [sub + float(k * SUBLANES) for k in range(nv)]
    for a, b in _sorting_network(nv):
        swap = (v[b] > v[a]) | ((v[b] == v[a]) & (ids[b] < ids[a]))
        v[a], v[b] = jnp.where(swap, v[b], v[a]), jnp.where(swap, v[a], v[b])
        ids[a], ids[b] = jnp.where(swap, ids[b], ids[a]), jnp.where(swap, ids[a], ids[b])
    slot = lax.broadcasted_iota(I32, (kk, tq), 0)
    vals = jnp.zeros((kk, tq), F32)
    idxs = jnp.zeros((kk, tq), F32)
    for it in range(kk):
        m = jnp.max(v[0], axis=0, keepdims=True)
        am = jnp.min(jnp.where(v[0] == m, ids[0], float(rows)), axis=0, keepdims=True)
        vals = jnp.where(slot == it, m, vals)
        idxs = jnp.where(slot == it, am, idxs)
        hit = ids[0] == am
        for k in range(kk - 1 - it):
            v[k] = jnp.where(hit, v[k + 1], v[k])
            ids[k] = jnp.where(hit, ids[k + 1], ids[k])
    return vals, idxs


def _pick_rows(table, sel, kk):
    out = jnp.zeros_like(table)
    for i in range(kk):
        out = jnp.where(sel == float(i), table[i:i + 1, :], out)
    return out


def _top_pairs(v1, v2):
    kk = PEER_TOPK
    tq = v1.shape[1]
    big = float(2 ** 20)
    sub = lax.broadcasted_iota(I32, (SUBLANES, tq), 0)
    subf = sub.astype(F32)
    near = v1[0:SUBLANES]
    depth = []
    for k in range(kk):
        reach = kk // (k + 1)
        row = near + v2[k:k + 1]
        depth.append(row if reach >= SUBLANES else jnp.where(sub < reach, row, -jnp.inf))
    far = v1[SUBLANES:kk] + v2[0:1]
    id_near = subf * float(kk)
    id_far = (subf + float(SUBLANES)) * float(kk)
    slot = lax.broadcasted_iota(I32, (kk, tq), 0)
    vals = jnp.zeros((kk, tq), F32)
    idxs = jnp.zeros((kk, tq), F32)
    for it in range(kk):
        m = jnp.max(jnp.maximum(depth[0], far), axis=0, keepdims=True)
        am = jnp.min(jnp.minimum(jnp.where(depth[0] == m, id_near, big), jnp.where(far == m, id_far, big)),
                     axis=0, keepdims=True)
        vals = jnp.where(slot == it, m, vals)
        idxs = jnp.where(slot == it, am, idxs)
        hit_near = id_near == am
        for k in range(kk - 1 - it):
            depth[k] = jnp.where(hit_near, depth[k + 1], depth[k])
        id_near = jnp.where(hit_near, id_near + 1.0, id_near)
        far = jnp.where(id_far == am, -jnp.inf, far)
    return vals, idxs


def _topk_kernel(qp_ref, k1_ref, k2_ref, off_ref, wts_ref):
    kk = PEER_TOPK
    half = PEER_D_KEY // 2
    nt = (((1,), (1,)), ((), ()))
    k1 = k1_ref[...]
    k2 = k2_ref[...]
    ids_all, w_all = [], []
    for hh in range(PEER_HEADS):
        q1 = qp_ref[:, hh * PEER_D_KEY: hh * PEER_D_KEY + half]
        q2 = qp_ref[:, hh * PEER_D_KEY + half: (hh + 1) * PEER_D_KEY]
        s1 = lax.dot_general(k1, q1, nt, preferred_element_type=F32)
        s2 = lax.dot_general(k2, q2, nt, preferred_element_type=F32)
        v1, i1 = _top_rows_by_columns(s1, kk)
        v2, i2 = _top_rows_by_columns(s2, kk)
        tv, ti = _top_pairs(v1, v2)
        hi = jnp.floor(ti * (1.0 / kk))
        lo = ti - hi * kk
        ids = _pick_rows(i1, hi, kk) * float(PEER_N_KEYS) + _pick_rows(i2, lo, kk)
        p = jnp.exp(tv - tv[0:1, :])
        w = p / jnp.sum(p, axis=0, keepdims=True)
        ids_all.append(ids)
        w_all.append(w)
    ids_t = jnp.concatenate(ids_all, axis=0).T.astype(I32)
    slot = lax.broadcasted_iota(I32, ids_t.shape, 1)
    off_ref[...] = (ids_t * TABLE_ROWS_PER_EXPERT
                    + jnp.where((slot % SUBLANES) < 4, TABLE_PAD_ROWS, TABLE_PAD_ROWS - TABLE_ROWS_PER_EXPERT))
    wts_ref[...] = jnp.concatenate(w_all, axis=0).T


def _topk(qp, k1, k2):
    t, qw = qp.shape
    tq = LANES
    out = lambda: pl.BlockSpec((tq, PEER_SLOTS), lambda i: (i, 0))
    key = lambda: pl.BlockSpec((PEER_N_KEYS, PEER_D_KEY // 2), lambda i: (0, 0))
    return pl.pallas_call(
        _topk_kernel,
        out_shape=(jax.ShapeDtypeStruct((t, PEER_SLOTS), I32),
                   jax.ShapeDtypeStruct((t, PEER_SLOTS), F32)),
        grid=(t // tq,),
        in_specs=[pl.BlockSpec((tq, qw), lambda i: (i, 0)), key(), key()],
        out_specs=(out(), out()),
        compiler_params=_cparams(("arbitrary",)),
        name="peer_topk",
    )(qp, k1, k2)


PEER_SLOTS = PEER_HEADS * PEER_TOPK
PEER_TOKENS_PER_STEP = 256
PEER_TOKENS_PER_HALF = PEER_TOKENS_PER_STEP // 2
TABLE_ROWS_PER_EXPERT = 4
PACK_EXPERTS_PER_STEP = 512
TABLE_PAD_ROWS = PACK_EXPERTS_PER_STEP * TABLE_ROWS_PER_EXPERT
HI_MASK = np.uint32(0xFFFF0000)


def _pack_kernel(w_ref, o_ref):
    step = pl.program_id(0)
    last = pl.num_programs(0) - 1

    @pl.when((step == 0) | (step == last))
    def _():
        o_ref[...] = jnp.zeros(o_ref.shape, U32)

    @pl.when((step > 0) & (step < last))
    def _():
        half = D_MODEL // 2
        x = w_ref[...].astype(BF16).astype(F32)
        bits = lax.bitcast_convert_type(x, U32)
        words = (bits[:, :half] >> 16) | (bits[:, half:] & HI_MASK)
        for s in range(TABLE_ROWS_PER_EXPERT):
            rows = pl.ds(s, PACK_EXPERTS_PER_STEP, stride=TABLE_ROWS_PER_EXPERT)
            o_ref[rows, :] = words[:, s * LANES:(s + 1) * LANES]


def _pack_table(w):
    e = w.shape[0]
    nblk = e // PACK_EXPERTS_PER_STEP
    return pl.pallas_call(
        _pack_kernel,
        out_shape=jax.ShapeDtypeStruct(((nblk + 2) * TABLE_PAD_ROWS, LANES), U32),
        grid=(nblk + 2,),
        in_specs=[pl.BlockSpec((PACK_EXPERTS_PER_STEP, D_MODEL), lambda i: (jnp.clip(i - 1, 0, nblk - 1), 0))],
        out_specs=pl.BlockSpec((TABLE_PAD_ROWS, LANES), lambda i: (i, 0)),
        compiler_params=_cparams(("arbitrary",)),
        name="pack_table",
    )(w)


def _expert_pair(tbl_ref, off_s, j, g, i, low4):
    wa = tbl_ref[pl.ds(off_s[j, g * SUBLANES + i], SUBLANES), :]
    wb = tbl_ref[pl.ds(off_s[j, g * SUBLANES + i + 4], SUBLANES), :]
    return jnp.where(low4, wa, wb)


def _for_token_halves(off_hbm, bufs, sem, half_body):
    step = pl.program_id(0)
    last = pl.num_programs(0) - 1

    def fill(at_step, half):
        start = at_step * PEER_TOKENS_PER_STEP + half * PEER_TOKENS_PER_HALF
        return pltpu.make_async_copy(off_hbm.at[pl.ds(start, PEER_TOKENS_PER_HALF)], bufs[half], sem.at[half])

    @pl.when(step == 0)
    def _():
        for half in range(2):
            fill(0, half).start()

    for half in range(2):
        fill(step, half).wait()
        half_body(half, bufs[half])

        @pl.when(step < last)
        def _():
            fill(step + 1, half).start()


def _offset_scratch():
    return [pltpu.SMEM((PEER_TOKENS_PER_HALF, PEER_SLOTS), I32),
            pltpu.SMEM((PEER_TOKENS_PER_HALF, PEER_SLOTS), I32),
            pltpu.SemaphoreType.DMA((2,))]


def _gelu_exact(x):
    return 0.5 * x * (1.0 + lax.erf(x * (2.0 ** -0.5)))


def _peer_u_slot_sum():
    col = np.arange(PEER_SLOTS * TABLE_ROWS_PER_EXPERT)
    q, r = col // SUBLANES, col % SUBLANES
    slot = SUBLANES * (q // 4) + (q % 4) + np.where(r < 4, 0, 4)
    return jnp.asarray((slot[:, None] == np.arange(PEER_SLOTS)[None, :]).astype(np.float32), BF16)


def _split_bf16(x):
    hi = x.astype(BF16)
    return hi, (x - hi.astype(F32)).astype(BF16)


def _peer_u_kernel(off_hbm, h_ref, wts_ref, slotsum_ref, tbl_ref, coef_ref, off_a, off_b, sem):
    nt = (((1,), (1,)), ((), ()))
    ones = jnp.ones((SUBLANES, LANES), BF16)
    sub = lax.broadcasted_iota(I32, (SUBLANES, LANES), 0)
    low4 = sub < 4
    r4 = sub % TABLE_ROWS_PER_EXPERT

    def token_row_sums(off_s, jj, j):
        def chunk(c):
            return jnp.broadcast_to(h_ref[j:j + 1, c * LANES:(c + 1) * LANES], (SUBLANES, LANES))

        def by_row(c0):
            return jnp.where(r4 == 0, chunk(c0), jnp.where(r4 == 1, chunk(c0 + 1),
                             jnp.where(r4 == 2, chunk(c0 + 2), chunk(c0 + 3))))

        hlo = by_row(0)
        hhi = by_row(TABLE_ROWS_PER_EXPERT)
        prods = []
        for q in range(PEER_SLOTS // 2):
            w = _expert_pair(tbl_ref, off_s, jj, q // 4, q % 4, low4)
            lo = lax.bitcast_convert_type(w << 16, F32)
            hi = lax.bitcast_convert_type(w & HI_MASK, F32)
            prods.append(lo * hlo + hi * hhi)
        stacked = jnp.concatenate(prods, axis=0).astype(BF16)
        return lax.dot_general(ones, stacked, nt, preferred_element_type=F32)[0:1, :]

    def finish(half, sums):
        hi, lo = _split_bf16(sums)
        acts = (jnp.dot(hi, slotsum_ref[...], preferred_element_type=F32)
                + jnp.dot(lo, slotsum_ref[...], preferred_element_type=F32))
        rows = pl.ds(half * PEER_TOKENS_PER_HALF, PEER_TOKENS_PER_HALF)
        coef_ref[rows, :] = wts_ref[rows, :] * _gelu_exact(acts)

    pending = []

    def half_body(half, off_s):
        while pending:
            finish(*pending.pop())
        base = half * PEER_TOKENS_PER_HALF
        sums = [token_row_sums(off_s, jj, base + jj) for jj in range(PEER_TOKENS_PER_HALF)]
        pending.append((half, jnp.concatenate(sums, axis=0)))

    _for_token_halves(off_hbm, (off_a, off_b), sem, half_body)
    finish(*pending.pop())


def _peer_u(off, h, wts, tbl):
    t = h.shape[0]
    tb = PEER_TOKENS_PER_STEP
    slotsum = _peer_u_slot_sum()
    return pl.pallas_call(
        _peer_u_kernel,
        out_shape=jax.ShapeDtypeStruct((t, PEER_SLOTS), F32),
        grid=(t // tb,),
        in_specs=[pl.BlockSpec(memory_space=pl.ANY),
                  pl.BlockSpec((tb, D_MODEL), lambda i: (i, 0)),
                  pl.BlockSpec((tb, PEER_SLOTS), lambda i: (i, 0)),
                  pl.BlockSpec(slotsum.shape, lambda i: (0, 0)),
                  pl.BlockSpec(memory_space=pltpu.VMEM)],
        out_specs=pl.BlockSpec((tb, PEER_SLOTS), lambda i: (i, 0)),
        scratch_shapes=_offset_scratch(),
        compiler_params=_cparams(("arbitrary",)),
        name="peer_u",
    )(off, h, wts, slotsum, tbl)


def _peer_v_tables():
    col = np.arange(PEER_SLOTS * SUBLANES)
    q, r = col // 16, col % 16
    slot = SUBLANES * (q // 4) + (q % 4) + np.where(r < 8, 0, 4)
    expand = (slot[None, :] == np.arange(PEER_SLOTS)[:, None]).astype(np.float32)
    out_row = (r // 2) % 4 + 4 * (r % 2)
    rowsel = (out_row[None, :] == np.arange(SUBLANES)[:, None]).astype(np.float32)
    return jnp.asarray(expand, BF16), jnp.asarray(rowsel, F32)


def _peer_v_kernel(off_hbm, coef_ref, x1_ref, g2_ref, gain_ref, expand_ref, rowsel_ref, tbl_ref, o_ref,
                   out_scr, off_a, off_b, sem):
    sub = lax.broadcasted_iota(I32, (SUBLANES, LANES), 0)
    low4 = sub < 4
    cexp = jnp.dot(coef_ref[...].astype(BF16), expand_ref[...], preferred_element_type=F32)
    rowsel = rowsel_ref[...]

    def half_body(half, off_s):
        for jj in range(PEER_TOKENS_PER_HALF):
            j = half * PEER_TOKENS_PER_HALF + jj
            tiles = []
            for q in range(PEER_SLOTS // 2):
                w = _expert_pair(tbl_ref, off_s, jj, q // 4, q % 4, low4)
                tiles.append(pltpu.bitcast(w, BF16))
            wmat = jnp.concatenate(tiles, axis=0)
            c = (cexp[j:j + 1, :] * rowsel).astype(BF16)
            out = jnp.dot(c, wmat, preferred_element_type=F32)
            for s in range(SUBLANES):
                out_scr[j:j + 1, s * LANES:(s + 1) * LANES] = out[s:s + 1, :]

    _for_token_halves(off_hbm, (off_a, off_b), sem, half_body)
    y = _rms_rows(out_scr[...], gain_ref[...])
    o_ref[...] = x1_ref[...] + g2_ref[0] * y


def _peer_v(off, coef, x1, g2, gain, tbl, tokens_per_sample):
    t = x1.shape[0]
    tb = PEER_TOKENS_PER_STEP
    steps_per_sample = tokens_per_sample // tb
    expand, rowsel = _peer_v_tables()
    tok = lambda: pl.BlockSpec((tb, D_MODEL), lambda i: (i, 0))
    return pl.pallas_call(
        _peer_v_kernel,
        out_shape=jax.ShapeDtypeStruct((t, D_MODEL), F32),
        grid=(t // tb,),
        in_specs=[pl.BlockSpec(memory_space=pl.ANY),
                  pl.BlockSpec((tb, PEER_SLOTS), lambda i: (i, 0)),
                  tok(),
                  pl.BlockSpec((1, 1, D_MODEL), lambda i: (i // steps_per_sample, 0, 0)),
                  pl.BlockSpec((1, D_MODEL), lambda i: (0, 0)),
                  pl.BlockSpec(expand.shape, lambda i: (0, 0)),
                  pl.BlockSpec(rowsel.shape, lambda i: (0, 0)),
                  pl.BlockSpec(memory_space=pltpu.VMEM)],
        out_specs=tok(),
        scratch_shapes=[pltpu.VMEM((tb, D_MODEL), F32)] + _offset_scratch(),
        compiler_params=_cparams(("arbitrary",)),
        name="peer_v",
    )(off, coef, x1, g2, gain, expand, rowsel, tbl)


@functools.lru_cache(maxsize=None)
def _rope_tables_np(n):
    rows = n // GRID_W
    row_ids = np.repeat(np.arange(rows, dtype=np.float64), GRID_W)
    col_ids = np.tile(np.arange(GRID_W, dtype=np.float64), rows)
    freqs = ROPE_BASE ** (-np.arange(ROPE_PAIRS, dtype=np.float64) / ROPE_PAIRS)
    ang = np.concatenate([row_ids[:, None] * freqs[None, :], col_ids[:, None] * freqs[None, :]], axis=-1)
    cos, sin = np.cos(ang), np.sin(ang)
    cos_h = np.concatenate([cos, cos], axis=-1)
    sin_h = np.concatenate([-sin, sin], axis=-1)
    return (np.tile(cos_h, (1, RET_HEADS)).astype(np.float32),
            np.tile(sin_h, (1, RET_HEADS)).astype(np.float32))


@functools.lru_cache(maxsize=None)
def _dft_tables_np(n):
    scale = (n * FNET_GROUP_DIM) ** -0.5
    jn = np.arange(n, dtype=np.int64)
    ang_n = ((jn[:, None] * jn[None, :]) % n).astype(np.float64) * (2.0 * np.pi / n)
    cs = np.concatenate([np.cos(ang_n), -np.sin(ang_n)], axis=1).astype(np.float32)
    jc = np.arange(FNET_GROUP_DIM, dtype=np.int64)
    ang_c = ((jc[:, None] * jc[None, :]) % FNET_GROUP_DIM).astype(np.float64) * (2.0 * np.pi / FNET_GROUP_DIM)
    eye = np.eye(FNET_GROUPS)
    bc = np.kron(eye, np.cos(ang_c) * scale).astype(np.float32)
    bs = np.kron(eye, np.sin(ang_c) * scale).astype(np.float32)
    return bc, bs, cs


def kernel(x, c, ctx, c_ctx, w_ada, b_ada, norm_pre_mix, norm_post_mix, w_in, ret_decay_fwd, ret_decay_bwd, ret_norm_gain, w_out, norm_pre_ffn, norm_post_ffn, peer_w_query, peer_sub_keys_1, peer_sub_keys_2, peer_u, peer_v):
    b, n, d = x.shape
    depth = w_ada.shape[0]
    assert depth == 1 and d == D_MODEL and n % RET_CHUNK == 0 and n % GRID_W == 0
    t = b * n
    tm = min(ROW_TILE, n)
    row = lambda a: a.reshape(1, -1)

    cos, sin = [jnp.asarray(a) for a in _rope_tables_np(n)]
    bc, bs, cs = [jnp.asarray(a).astype(BF16) for a in _dft_tables_np(n)]

    pad = (-(b + 1)) % SUBLANES
    c_all = jnp.concatenate([c, c_ctx[None, :], jnp.zeros((pad, d), F32)], axis=0)
    mod = _ada(c_all, w_ada[0], row(b_ada[0]))
    sh1, sc1, g1, sh2, sc2, g2 = [mod[:b, i * d:(i + 1) * d].reshape(b, 1, d) for i in range(N_MOD)]
    csh1 = mod[b:b + 1, 0:d]
    csc1 = mod[b:b + 1, d:2 * d]

    a_f, a_b = ret_decay_fwd[0], ret_decay_bwd[0]
    a_f_lane = row(jnp.repeat(a_f, RET_DK))
    a_b_lane = row(jnp.repeat(a_b, RET_DK))

    w_in_b = w_in[0].astype(BF16)
    stf, stb = _ctx_states(ctx, row(norm_pre_mix[0]), csh1, csc1,
                           w_in_b[:, QK_WIDTH:QK_WIDTH + 2 * QK_WIDTH], a_f_lane, a_b_lane)

    q, k, v, gate, f = _inproj(x, sh1, sc1, row(norm_pre_mix[0]), w_in_b, cos, sin, tm)
    ret = _retention(q, k, v, gate, stf, stb, a_f, a_b, a_f_lane, a_b_lane, row(ret_norm_gain[0]))
    four = _fourier(f, bc, bs, cs, tm)

    x1, h2, qp = _outproj(ret, four, x, g1, sh2, sc2, row(norm_post_mix[0]), row(norm_pre_ffn[0]),
                          w_out[0].astype(BF16), peer_w_query[0].astype(BF16), tm)

    off, wts = _topk(qp.reshape(t, -1), peer_sub_keys_1[0].astype(BF16), peer_sub_keys_2[0].astype(BF16))

    coef = _peer_u(off, h2.reshape(t, d), wts, _pack_table(peer_u[0]))
    out = _peer_v(off, coef, x1.reshape(t, d), g2, row(norm_post_ffn[0]), _pack_table(peer_v[0]), n)
    return out.reshape(b, n, d)
```

```python
import functools

import numpy as np
import jax
import jax.numpy as jnp
from jax import lax
from jax.experimental import pallas as pl
from jax.experimental.pallas import tpu as pltpu

F32 = jnp.float32
BF16 = jnp.bfloat16
I32 = jnp.int32
U32 = jnp.uint32

D_MODEL = 1024
GRID_W = 64
RET_HEADS = 8
RET_DK = 64
RET_WIDTH = 512
QK_WIDTH = 512
RET_CHUNK = 128
ROPE_PAIRS = RET_DK // 4
ROPE_BASE = 10000.0
FNET_GROUPS = 4
FNET_GROUP_DIM = 128
FNET_WIDTH = 512
IN_WIDTH = 2560
PEER_HEADS = 8
PEER_D_KEY = 256
PEER_N_KEYS = 128
PEER_TOPK = 16
N_MOD = 6
EPS = 1e-6

LANES = 128
SUBLANES = 8
VMEM_LIMIT = 56 * 1024 * 1024
N_PAIRS = RET_HEADS // 2
ROW_TILE = 1024


def _cparams(sem):
    return pltpu.CompilerParams(dimension_semantics=sem, vmem_limit_bytes=VMEM_LIMIT)


def _rms_rows(x, gain):
    ms = jnp.mean(x * x, axis=-1, keepdims=True)
    return x * lax.rsqrt(ms + EPS) * gain


def _silu(x):
    return x * jax.nn.sigmoid(x)


def _log_gamma(a):
    return jnp.log1p(-jnp.exp(a))


def _ada_kernel(c_ref, w_ref, b_ref, o_ref):
    s = _silu(c_ref[...])
    o_ref[...] = jnp.dot(s, w_ref[...], preferred_element_type=F32) + b_ref[...]


def _ada(c_all, w_ada, b_ada):
    rows = c_all.shape[0]
    width = w_ada.shape[1]
    tn = 1536
    return pl.pallas_call(
        _ada_kernel,
        out_shape=jax.ShapeDtypeStruct((rows, width), F32),
        grid=(width // tn,),
        in_specs=[
            pl.BlockSpec((rows, D_MODEL), lambda j: (0, 0)),
            pl.BlockSpec((D_MODEL, tn), lambda j: (0, j)),
            pl.BlockSpec((1, tn), lambda j: (0, j)),
        ],
        out_specs=pl.BlockSpec((rows, tn), lambda j: (0, j)),
        compiler_params=_cparams(("arbitrary",)),
        name="ada",
    )(c_all, w_ada, b_ada)


def _ctx_kernel(ctx_ref, g_ref, sh_ref, sc_ref, w_ref, af_ref, ab_ref, stf_ref, stb_ref):
    x = ctx_ref[0]
    n = x.shape[0]
    h = (_rms_rows(x, g_ref[...]) * (1.0 + sc_ref[...]) + sh_ref[...]).astype(BF16)
    kv = jnp.dot(h, w_ref[...], preferred_element_type=F32)
    k = kv[:, :QK_WIDTH] * (RET_DK ** -0.5)
    v = kv[:, QK_WIDTH:].astype(BF16)
    pos = lax.broadcasted_iota(I32, (n, QK_WIDTH), 0).astype(F32)
    lgf = _log_gamma(af_ref[...])
    lgb = _log_gamma(ab_ref[...])
    kf = (k * jnp.exp(lgf * (n - 1.0 - pos))).astype(BF16)
    kb = (k * jnp.exp(lgb * pos)).astype(BF16)
    tn = (((0,), (0,)), ((), ()))
    gf = lax.dot_general(kf, v, tn, preferred_element_type=F32)
    gb = lax.dot_general(kb, v, tn, preferred_element_type=F32)
    ri = lax.broadcasted_iota(I32, (LANES, LANES), 0) // RET_DK
    ci = lax.broadcasted_iota(I32, (LANES, LANES), 1) // RET_DK
    bd = ri == ci
    for p in range(N_PAIRS):
        sl = slice(p * LANES, (p + 1) * LANES)
        stf_ref[0, p] = jnp.where(bd, gf[sl, sl], 0.0)
        stb_ref[0, p] = jnp.where(bd, gb[sl, sl], 0.0)


def _ctx_states(ctx, gain, csh, csc, w_kv, a_f, a_b):
    b, n, _ = ctx.shape
    vec = lambda: pl.BlockSpec((1, D_MODEL), lambda i: (0, 0))
    lane = lambda: pl.BlockSpec((1, QK_WIDTH), lambda i: (0, 0))
    st = jax.ShapeDtypeStruct((b, N_PAIRS, LANES, LANES), F32)
    st_spec = lambda: pl.BlockSpec((1, N_PAIRS, LANES, LANES), lambda i: (i, 0, 0, 0))
    return pl.pallas_call(
        _ctx_kernel,
        out_shape=(st, st),
        grid=(b,),
        in_specs=[
            pl.BlockSpec((1, n, D_MODEL), lambda i: (i, 0, 0)),
            vec(), vec(), vec(),
            pl.BlockSpec((D_MODEL, 2 * QK_WIDTH), lambda i: (0, 0)),
            lane(), lane(),
        ],
        out_specs=(st_spec(), st_spec()),
        compiler_params=_cparams(("arbitrary",)),
        name="ctx_states",
    )(ctx, gain, csh, csc, w_kv, a_f, a_b)


def _inproj_kernel(x_ref, sh_ref, sc_ref, g_ref, w_ref, cos_ref, sin_ref,
                   q_ref, k_ref, v_ref, gate_ref, f_ref):
    x = x_ref[0]
    tm = x.shape[0]
    h = (_rms_rows(x, g_ref[...]) * (1.0 + sc_ref[0]) + sh_ref[0]).astype(BF16)

    def proj(lo, hi):
        return jnp.dot(h, w_ref[:, lo:hi], preferred_element_type=F32)

    cos = cos_ref[...]
    sin = sin_ref[...]
    lane = lax.broadcasted_iota(I32, (tm, QK_WIDTH), 1)
    first = (lane % RET_DK) < (RET_DK // 2)

    def rope(t):
        swapped = jnp.where(first,
                            pltpu.roll(t, QK_WIDTH - RET_DK // 2, 1),
                            pltpu.roll(t, RET_DK // 2, 1))
        return t * cos + swapped * sin

    q_ref[0] = rope(proj(0, 512)).astype(BF16)
    k_ref[0] = rope(proj(512, 1024) * (RET_DK ** -0.5)).astype(BF16)
    v_ref[0] = proj(1024, 1536).astype(BF16)
    gate_ref[0] = proj(1536, 2048)
    f_ref[0] = proj(2048, 2560).astype(BF16)


def _inproj(x, sh, sc, gain, w_in, cos, sin, tm):
    b, n, _ = x.shape
    mod = lambda: pl.BlockSpec((1, 1, D_MODEL), lambda i, j: (i, 0, 0))
    tab = lambda: pl.BlockSpec((tm, QK_WIDTH), lambda i, j: (j, 0))
    out = lambda: pl.BlockSpec((1, tm, 512), lambda i, j: (i, j, 0))
    sd = lambda dt: jax.ShapeDtypeStruct((b, n, 512), dt)
    return pl.pallas_call(
        _inproj_kernel,
        out_shape=(sd(BF16), sd(BF16), sd(BF16), sd(F32), sd(BF16)),
        grid=(b, n // tm),
        in_specs=[
            pl.BlockSpec((1, tm, D_MODEL), lambda i, j: (i, j, 0)),
            mod(), mod(),
            pl.BlockSpec((1, D_MODEL), lambda i, j: (0, 0)),
            pl.BlockSpec((D_MODEL, IN_WIDTH), lambda i, j: (0, 0)),
            tab(), tab(),
        ],
        out_specs=(out(), out(), out(), out(), out()),
        compiler_params=_cparams(("arbitrary", "arbitrary")),
        name="inproj",
    )(x, sh, sc, gain, w_in, cos, sin)


def _ret_kernel(af_s, ab_s, q_ref, k_ref, v_ref, gate_ref, stf_ref, stb_ref, af_ref, ab_ref,
                gain_ref, o_ref, mask_scr, qdf_scr, qdb_scr, kdf_scr, kdb_scr, sf_scr, sb_scr,
                sb_store):
    c = RET_CHUNK
    n = q_ref.shape[1]
    nc = n // c
    lgf = _log_gamma(af_ref[...])
    lgb = _log_gamma(ab_ref[...])
    cdf = jnp.exp(lgf * float(c))
    cdb = jnp.exp(lgb * float(c))

    @pl.when(pl.program_id(0) == 0)
    def _():
        r = lax.broadcasted_iota(I32, (c, RET_WIDTH), 0).astype(F32)
        qdf_scr[...] = jnp.exp(lgf * (r + 1.0))
        kdf_scr[...] = jnp.exp(lgf * (c - 1.0 - r))
        qdb_scr[...] = jnp.exp(lgb * (c - r))
        kdb_scr[...] = jnp.exp(lgb * r)
        ri = lax.broadcasted_iota(I32, (c, c), 0)
        ci = lax.broadcasted_iota(I32, (c, c), 1)
        d = (ri - ci).astype(F32)
        for hh in range(RET_HEADS):
            gf = _log_gamma(jnp.full((c, c), af_s[hh], F32))
            gb = _log_gamma(jnp.full((c, c), ab_s[hh], F32))
            mask_scr[hh] = jnp.where(d >= 0.0, jnp.exp(gf * jnp.maximum(d, 0.0)),
                                     jnp.exp(gb * jnp.maximum(-d, 0.0)))

    ri = lax.broadcasted_iota(I32, (LANES, LANES), 0) // RET_DK
    ci = lax.broadcasted_iota(I32, (LANES, LANES), 1) // RET_DK
    bd = ri == ci
    bd_ones = jnp.where(bd, 1.0, 0.0).astype(BF16)
    lane_head = lax.broadcasted_iota(I32, (c, LANES), 1) // RET_DK
    tn = (((0,), (0,)), ((), ()))
    nt = (((1,), (1,)), ((), ()))

    for p in range(N_PAIRS):
        sf_scr[p] = stf_ref[0, p]
        sb_scr[p] = stb_ref[0, p]

    def bwd_states(i, carry):
        cc = nc - 1 - i
        rows = pl.ds(pl.multiple_of(cc * c, c), c)
        grams = []
        for p in range(N_PAIRS):
            sl = slice(p * LANES, (p + 1) * LANES)
            kw = (k_ref[0, rows, sl].astype(F32) * kdb_scr[:, sl]).astype(BF16)
            grams.append(lax.dot_general(kw, v_ref[0, rows, sl], tn, preferred_element_type=F32))
        for p in range(N_PAIRS):
            sl = slice(p * LANES, (p + 1) * LANES)
            sb = sb_scr[p]
            sb_store[cc, p] = sb.astype(BF16)
            sb_scr[p] = sb * cdb[:, sl] + jnp.where(bd, grams[p], 0.0)
        return carry

    lax.fori_loop(0, nc, bwd_states, 0)

    def fwd(cc, carry):
        rows = pl.ds(pl.multiple_of(cc * c, c), c)
        pairs = range(N_PAIRS)
        lanes = [slice(p * LANES, (p + 1) * LANES) for p in pairs]
        zero = jnp.zeros((c, LANES), BF16)
        both_bd = jnp.concatenate([bd_ones, bd_ones], axis=0)
        qs = [q_ref[0, rows, sl] for sl in lanes]
        ks = [k_ref[0, rows, sl] for sl in lanes]
        vs = [v_ref[0, rows, sl] for sl in lanes]
        sfs = [sf_scr[p] for p in pairs]
        scores = []
        for p in pairs:
            q_heads = jnp.concatenate([jnp.where(lane_head == j, qs[p], zero) for j in range(2)], axis=0)
            scores.append(lax.dot_general(q_heads, ks[p], nt, preferred_element_type=F32))
        accs = []
        for p in pairs:
            qf = qs[p].astype(F32)
            q_dec = jnp.concatenate([(qf * qdf_scr[:, lanes[p]]).astype(BF16),
                                     (qf * qdb_scr[:, lanes[p]]).astype(BF16)], axis=1)
            s_both = jnp.concatenate([sfs[p].astype(BF16), sb_store[cc, p]], axis=0)
            accs.append(jnp.dot(q_dec, s_both, preferred_element_type=F32))
        grams = []
        for p in pairs:
            kw = (ks[p].astype(F32) * kdf_scr[:, lanes[p]]).astype(BF16)
            grams.append(lax.dot_general(kw, vs[p], tn, preferred_element_type=F32))
        for p in pairs:
            s = scores[p]
            pm = jnp.concatenate([(s[j * c:(j + 1) * c] * mask_scr[2 * p + j]).astype(BF16) for j in range(2)], axis=1)
            v_heads = jnp.concatenate([jnp.where(lane_head == j, vs[p], zero) for j in range(2)], axis=0)
            accs[p] = accs[p] + jnp.dot(pm, v_heads, preferred_element_type=F32)
        means = []
        for p in pairs:
            sq_hi, sq_lo = _split_bf16(accs[p] * accs[p])
            means.append(jnp.dot(jnp.concatenate([sq_hi, sq_lo], axis=1), both_bd,
                                 preferred_element_type=F32) * (1.0 / RET_DK))
        for p in pairs:
            y = accs[p] * lax.rsqrt(means[p] + EPS) * gain_ref[:, lanes[p]] * _silu(gate_ref[0, rows, lanes[p]])
            o_ref[0, rows, lanes[p]] = y.astype(BF16)
            sf_scr[p] = sfs[p] * cdf[:, lanes[p]] + jnp.where(bd, grams[p], 0.0)
        return carry

    lax.fori_loop(0, nc, fwd, 0, unroll=2)


def _retention(q, k, v, gate, stf, stb, a_f, a_b, a_f_lane, a_b_lane, gain):
    b, n, _ = q.shape
    nc = n // RET_CHUNK
    seq = lambda: pl.BlockSpec((1, n, RET_WIDTH), lambda i, *_: (i, 0, 0))
    st = lambda: pl.BlockSpec((1, N_PAIRS, LANES, LANES), lambda i, *_: (i, 0, 0, 0))
    lane = lambda: pl.BlockSpec((1, RET_WIDTH), lambda i, *_: (0, 0))
    grid_spec = pltpu.PrefetchScalarGridSpec(
        num_scalar_prefetch=2,
        grid=(b,),
        in_specs=[seq(), seq(), seq(), seq(), st(), st(), lane(), lane(), lane()],
        out_specs=seq(),
        scratch_shapes=[
            pltpu.VMEM((RET_HEADS, RET_CHUNK, RET_CHUNK), F32),
            pltpu.VMEM((RET_CHUNK, RET_WIDTH), F32),
            pltpu.VMEM((RET_CHUNK, RET_WIDTH), F32),
            pltpu.VMEM((RET_CHUNK, RET_WIDTH), F32),
            pltpu.VMEM((RET_CHUNK, RET_WIDTH), F32),
            pltpu.VMEM((N_PAIRS, LANES, LANES), F32),
            pltpu.VMEM((N_PAIRS, LANES, LANES), F32),
            pltpu.VMEM((nc, N_PAIRS, LANES, LANES), BF16),
        ],
    )
    return pl.pallas_call(
        _ret_kernel,
        out_shape=jax.ShapeDtypeStruct((b, n, RET_WIDTH), BF16),
        grid_spec=grid_spec,
        compiler_params=_cparams(("arbitrary",)),
        name="retention",
    )(a_f, a_b, q, k, v, gate, stf, stb, a_f_lane, a_b_lane, gain)


def _four_kernel(f_ref, bc_ref, bs_ref, cs_ref, o_ref, z_scr):
    n = f_ref.shape[1]

    @pl.when(pl.program_id(1) == 0)
    def _():
        f = f_ref[0]
        z_scr[0:n, :] = jnp.dot(f, bc_ref[...], preferred_element_type=F32).astype(BF16)
        z_scr[n:2 * n, :] = jnp.dot(f, bs_ref[...], preferred_element_type=F32).astype(BF16)

    o_ref[0] = jnp.dot(cs_ref[...], z_scr[...], preferred_element_type=F32).astype(BF16)


def _fourier(f, bc, bs, cs, tn):
    b, n, _ = f.shape
    return pl.pallas_call(
        _four_kernel,
        out_shape=jax.ShapeDtypeStruct((b, n, FNET_WIDTH), BF16),
        grid=(b, n // tn),
        in_specs=[
            pl.BlockSpec((1, n, FNET_WIDTH), lambda i, j: (i, 0, 0)),
            pl.BlockSpec((FNET_WIDTH, FNET_WIDTH), lambda i, j: (0, 0)),
            pl.BlockSpec((FNET_WIDTH, FNET_WIDTH), lambda i, j: (0, 0)),
            pl.BlockSpec((tn, 2 * n), lambda i, j: (j, 0)),
        ],
        out_specs=pl.BlockSpec((1, tn, FNET_WIDTH), lambda i, j: (i, j, 0)),
        scratch_shapes=[pltpu.VMEM((2 * n, FNET_WIDTH), BF16)],
        compiler_params=_cparams(("arbitrary", "arbitrary")),
        name="fourier",
    )(f, bc, bs, cs)


def _outp_kernel(ret_ref, four_ref, x_ref, g1_ref, sh2_ref, sc2_ref, gpost_ref, gpre_ref,
                 wo_ref, wq_ref, x1_ref, h2_ref, qp_ref):
    mix = jnp.dot(ret_ref[0], wo_ref[0:RET_WIDTH, :], preferred_element_type=F32)
    mix += jnp.dot(four_ref[0], wo_ref[RET_WIDTH:, :], preferred_element_type=F32)
    x1 = x_ref[0] + g1_ref[0] * _rms_rows(mix, gpost_ref[...])
    x1_ref[0] = x1
    h2 = _rms_rows(x1, gpre_ref[...]) * (1.0 + sc2_ref[0]) + sh2_ref[0]
    h2_ref[0] = h2
    qp_ref[0] = jnp.dot(h2.astype(BF16), wq_ref[...], preferred_element_type=F32).astype(BF16)


def _outproj(ret, four, x, g1, sh2, sc2, gpost, gpre, w_out, w_q, tm):
    b, n, _ = x.shape
    qw = w_q.shape[1]
    mod = lambda: pl.BlockSpec((1, 1, D_MODEL), lambda i, j: (i, 0, 0))
    vec = lambda: pl.BlockSpec((1, D_MODEL), lambda i, j: (0, 0))
    half = lambda: pl.BlockSpec((1, tm, 512), lambda i, j: (i, j, 0))
    full = lambda: pl.BlockSpec((1, tm, D_MODEL), lambda i, j: (i, j, 0))
    return pl.pallas_call(
        _outp_kernel,
        out_shape=(jax.ShapeDtypeStruct((b, n, D_MODEL), F32),
                   jax.ShapeDtypeStruct((b, n, D_MODEL), F32),
                   jax.ShapeDtypeStruct((b, n, qw), BF16)),
        grid=(b, n // tm),
        in_specs=[half(), half(), full(), mod(), mod(), mod(), vec(), vec(),
                  pl.BlockSpec(w_out.shape, lambda i, j: (0, 0)),
                  pl.BlockSpec(w_q.shape, lambda i, j: (0, 0))],
        out_specs=(full(), full(), pl.BlockSpec((1, tm, qw), lambda i, j: (i, j, 0))),
        compiler_params=_cparams(("arbitrary", "arbitrary")),
        name="outproj",
    )(ret, four, x, g1, sh2, sc2, gpost, gpre, w_out, w_q)


def _sorting_network(n):
    pairs = []
    p = 1
    while p < n:
        k = p
        while k >= 1:
            for j in range(k % p, n - k, 2 * k):
                for i in range(min(k, n - j - k)):
                    if (i + j) // (2 * p) == (i + j + k) // (2 * p):
                        pairs.append((i + j, i + j + k))
            k //= 2
        p *= 2
    return pairs


def _top_rows_by_columns(s, kk):
    rows, tq = s.shape
    nv = rows // SUBLANES
    assert nv == kk
    sub = lax.broadcasted_iota(I32, (SUBLANES, tq), 0).astype(F32)
    v = [s[k * SUBLANES:(k + 1) * SUBLANES] for k in range(nv)]
    ids = [sub + float(k * SUBLANES) for k in range(nv)]
    for a, b in _sorting_network(nv):
        swap = (v[b] > v[a]) | ((v[b] == v[a]) & (ids[b] < ids[a]))
        v[a], v[b] = jnp.where(swap, v[b], v[a]), jnp.where(swap, v[a], v[b])
        ids[a], ids[b] = jnp.where(swap, ids[b], ids[a]), jnp.where(swap, ids[a], ids[b])
    slot = lax.broadcasted_iota(I32, (kk, tq), 0)
    vals = jnp.zeros((kk, tq), F32)
    idxs = jnp.zeros((kk, tq), F32)
    for it in range(kk):
        m = jnp.max(v[0], axis=0, keepdims=True)
        am = jnp.min(jnp.where(v[0] == m, ids[0], float(rows)), axis=0, keepdims=True)
        vals = jnp.where(slot == it, m, vals)
        idxs = jnp.where(slot == it, am, idxs)
        hit = ids[0] == am
        for k in range(kk - 1 - it):
            v[k] = jnp.where(hit, v[k + 1], v[k])
            ids[k] = jnp.where(hit, ids[k + 1], ids[k])
    return vals, idxs


def _pick_rows(table, sel, kk):
    out = jnp.zeros_like(table)
    for i in range(kk):
        out = jnp.where(sel == float(i), table[i:i + 1, :], out)
    return out


def _top_pairs(v1, v2):
    kk = PEER_TOPK
    tq = v1.shape[1]
    big = float(2 ** 20)
    sub = lax.broadcasted_iota(I32, (SUBLANES, tq), 0)
    subf = sub.astype(F32)
    near = v1[0:SUBLANES]
    depth = []
    for k in range(kk):
        reach = kk // (k + 1)
        row = near + v2[k:k + 1]
        depth.append(row if reach >= SUBLANES else jnp.where(sub < reach, row, -jnp.inf))
    far = v1[SUBLANES:kk] + v2[0:1]
    id_near = subf * float(kk)
    id_far = (subf + float(SUBLANES)) * float(kk)
    slot = lax.broadcasted_iota(I32, (kk, tq), 0)
    vals = jnp.zeros((kk, tq), F32)
    idxs = jnp.zeros((kk, tq), F32)
    for it in range(kk):
        m = jnp.max(jnp.maximum(depth[0], far), axis=0, keepdims=True)
        am = jnp.min(jnp.minimum(jnp.where(depth[0] == m, id_near, big), jnp.where(far == m, id_far, big)),
                     axis=0, keepdims=True)
        vals = jnp.where(slot == it, m, vals)
        idxs = jnp.where(slot == it, am, idxs)
        hit_near = id_near == am
        for k in range(kk - 1 - it):
            depth[k] = jnp.where(hit_near, depth[k + 1], depth[k])
        id_near = jnp.where(hit_near, id_near + 1.0, id_near)
        far = jnp.where(id_far == am, -jnp.inf, far)
    return vals, idxs


def _topk_kernel(qp_ref, k1_ref, k2_ref, off_ref, wts_ref):
    kk = PEER_TOPK
    half = PEER_D_KEY // 2
    nt = (((1,), (1,)), ((), ()))
    k1 = k1_ref[...]
    k2 = k2_ref[...]
    ids_all, w_all = [], []
    for hh in range(PEER_HEADS):
        q1 = qp_ref[:, hh * PEER_D_KEY: hh * PEER_D_KEY + half]
        q2 = qp_ref[:, hh * PEER_D_KEY + half: (hh + 1) * PEER_D_KEY]
        s1 = lax.dot_general(k1, q1, nt, preferred_element_type=F32)
        s2 = lax.dot_general(k2, q2, nt, preferred_element_type=F32)
        v1, i1 = _top_rows_by_columns(s1, kk)
        v2, i2 = _top_rows_by_columns(s2, kk)
        tv, ti = _top_pairs(v1, v2)
        hi = jnp.floor(ti * (1.0 / kk))
        lo = ti - hi * kk
        ids = _pick_rows(i1, hi, kk) * float(PEER_N_KEYS) + _pick_rows(i2, lo, kk)
        p = jnp.exp(tv - tv[0:1, :])
        w = p / jnp.sum(p, axis=0, keepdims=True)
        ids_all.append(ids)
        w_all.append(w)
    ids_t = jnp.concatenate(ids_all, axis=0).T.astype(I32)
    slot = lax.broadcasted_iota(I32, ids_t.shape, 1)
    off_ref[...] = (ids_t * TABLE_ROWS_PER_EXPERT
                    + jnp.where((slot % SUBLANES) < 4, TABLE_PAD_ROWS, TABLE_PAD_ROWS - TABLE_ROWS_PER_EXPERT))
    wts_ref[...] = jnp.concatenate(w_all, axis=0).T


def _topk(qp, k1, k2):
    t, qw = qp.shape
    tq = LANES
    out = lambda: pl.BlockSpec((tq, PEER_SLOTS), lambda i: (i, 0))
    key = lambda: pl.BlockSpec((PEER_N_KEYS, PEER_D_KEY // 2), lambda i: (0, 0))
    return pl.pallas_call(
        _topk_kernel,
        out_shape=(jax.ShapeDtypeStruct((t, PEER_SLOTS), I32),
                   jax.ShapeDtypeStruct((t, PEER_SLOTS), F32)),
        grid=(t // tq,),
        in_specs=[pl.BlockSpec((tq, qw), lambda i: (i, 0)), key(), key()],
        out_specs=(out(), out()),
        compiler_params=_cparams(("arbitrary",)),
        name="peer_topk",
    )(qp, k1, k2)


PEER_SLOTS = PEER_HEADS * PEER_TOPK
PEER_TOKENS_PER_STEP = 256
PEER_TOKENS_PER_HALF = PEER_TOKENS_PER_STEP // 2
TABLE_ROWS_PER_EXPERT = 4
PACK_EXPERTS_PER_STEP = 512
TABLE_PAD_ROWS = PACK_EXPERTS_PER_STEP * TABLE_ROWS_PER_EXPERT
HI_MASK = np.uint32(0xFFFF0000)


def _pack_kernel(w_ref, o_ref):
    step = pl.program_id(0)
    last = pl.num_programs(0) - 1

    @pl.when((step == 0) | (step == last))
    def _():
        o_ref[...] = jnp.zeros(o_ref.shape, U32)

    @pl.when((step > 0) & (step < last))
    def _():
        half = D_MODEL // 2
        x = w_ref[...].astype(BF16).astype(F32)
        bits = lax.bitcast_convert_type(x, U32)
        words = (bits[:, :half] >> 16) | (bits[:, half:] & HI_MASK)
        for s in range(TABLE_ROWS_PER_EXPERT):
            rows = pl.ds(s, PACK_EXPERTS_PER_STEP, stride=TABLE_ROWS_PER_EXPERT)
            o_ref[rows, :] = words[:, s * LANES:(s + 1) * LANES]


def _pack_table(w):
    e = w.shape[0]
    nblk = e // PACK_EXPERTS_PER_STEP
    return pl.pallas_call(
        _pack_kernel,
        out_shape=jax.ShapeDtypeStruct(((nblk + 2) * TABLE_PAD_ROWS, LANES), U32),
        grid=(nblk + 2,),
        in_specs=[pl.BlockSpec((PACK_EXPERTS_PER_STEP, D_MODEL), lambda i: (jnp.clip(i - 1, 0, nblk - 1), 0))],
        out_specs=pl.BlockSpec((TABLE_PAD_ROWS, LANES), lambda i: (i, 0)),
        compiler_params=_cparams(("arbitrary",)),
        name="pack_table",
    )(w)


def _expert_pair(tbl_ref, off_s, j, g, i, low4):
    wa = tbl_ref[pl.ds(off_s[j, g * SUBLANES + i], SUBLANES), :]
    wb = tbl_ref[pl.ds(off_s[j, g * SUBLANES + i + 4], SUBLANES), :]
    return jnp.where(low4, wa, wb)


def _for_token_halves(off_hbm, bufs, sem, half_body):
    step = pl.program_id(0)
    last = pl.num_programs(0) - 1

    def fill(at_step, half):
        start = at_step * PEER_TOKENS_PER_STEP + half * PEER_TOKENS_PER_HALF
        return pltpu.make_async_copy(off_hbm.at[pl.ds(start, PEER_TOKENS_PER_HALF)], bufs[half], sem.at[half])

    @pl.when(step == 0)
    def _():
        for half in range(2):
            fill(0, half).start()

    for half in range(2):
        fill(step, half).wait()
        half_body(half, bufs[half])

        @pl.when(step < last)
        def _():
            fill(step + 1, half).start()


def _offset_scratch():
    return [pltpu.SMEM((PEER_TOKENS_PER_HALF, PEER_SLOTS), I32),
            pltpu.SMEM((PEER_TOKENS_PER_HALF, PEER_SLOTS), I32),
            pltpu.SemaphoreType.DMA((2,))]


def _gelu_exact(x):
    return 0.5 * x * (1.0 + lax.erf(x * (2.0 ** -0.5)))


def _peer_u_slot_sum():
    col = np.arange(PEER_SLOTS * TABLE_ROWS_PER_EXPERT)
    q, r = col // SUBLANES, col % SUBLANES
    slot = SUBLANES * (q // 4) + (q % 4) + np.where(r < 4, 0, 4)
    return jnp.asarray((slot[:, None] == np.arange(PEER_SLOTS)[None, :]).astype(np.float32), BF16)


def _split_bf16(x):
    hi = x.astype(BF16)
    return hi, (x - hi.astype(F32)).astype(BF16)


def _peer_u_kernel(off_hbm, h_ref, wts_ref, slotsum_ref, tbl_ref, coef_ref, off_a, off_b, sem):
    nt = (((1,), (1,)), ((), ()))
    ones = jnp.ones((SUBLANES, LANES), BF16)
    sub = lax.broadcasted_iota(I32, (SUBLANES, LANES), 0)
    low4 = sub < 4
    r4 = sub % TABLE_ROWS_PER_EXPERT

    def token_row_sums(off_s, jj, j):
        def chunk(c):
            return jnp.broadcast_to(h_ref[j:j + 1, c * LANES:(c + 1) * LANES], (SUBLANES, LANES))

        def by_row(c0):
            return jnp.where(r4 == 0, chunk(c0), jnp.where(r4 == 1, chunk(c0 + 1),
                             jnp.where(r4 == 2, chunk(c0 + 2), chunk(c0 + 3))))

        hlo = by_row(0)
        hhi = by_row(TABLE_ROWS_PER_EXPERT)
        prods = []
        for q in range(PEER_SLOTS // 2):
            w = _expert_pair(tbl_ref, off_s, jj, q // 4, q % 4, low4)
            lo = lax.bitcast_convert_type(w << 16, F32)
            hi = lax.bitcast_convert_type(w & HI_MASK, F32)
            prods.append(lo * hlo + hi * hhi)
        stacked = jnp.concatenate(prods, axis=0).astype(BF16)
        return lax.dot_general(ones, stacked, nt, preferred_element_type=F32)[0:1, :]

    def finish(half, sums):
        hi, lo = _split_bf16(sums)
        acts = (jnp.dot(hi, slotsum_ref[...], preferred_element_type=F32)
                + jnp.dot(lo, slotsum_ref[...], preferred_element_type=F32))
        rows = pl.ds(half * PEER_TOKENS_PER_HALF, PEER_TOKENS_PER_HALF)
        coef_ref[rows, :] = wts_ref[rows, :] * _gelu_exact(acts)

    pending = []

    def half_body(half, off_s):
        while pending:
            finish(*pending.pop())
        base = half * PEER_TOKENS_PER_HALF
        sums = [token_row_sums(off_s, jj, base + jj) for jj in range(PEER_TOKENS_PER_HALF)]
        pending.append((half, jnp.concatenate(sums, axis=0)))

    _for_token_halves(off_hbm, (off_a, off_b), sem, half_body)
    finish(*pending.pop())


def _peer_u(off, h, wts, tbl):
    t = h.shape[0]
    tb = PEER_TOKENS_PER_STEP
    slotsum = _peer_u_slot_sum()
    return pl.pallas_call(
        _peer_u_kernel,
        out_shape=jax.ShapeDtypeStruct((t, PEER_SLOTS), F32),
        grid=(t // tb,),
        in_specs=[pl.BlockSpec(memory_space=pl.ANY),
                  pl.BlockSpec((tb, D_MODEL), lambda i: (i, 0)),
                  pl.BlockSpec((tb, PEER_SLOTS), lambda i: (i, 0)),
                  pl.BlockSpec(slotsum.shape, lambda i: (0, 0)),
                  pl.BlockSpec(memory_space=pltpu.VMEM)],
        out_specs=pl.BlockSpec((tb, PEER_SLOTS), lambda i: (i, 0)),
        scratch_shapes=_offset_scratch(),
        compiler_params=_cparams(("arbitrary",)),
        name="peer_u",
    )(off, h, wts, slotsum, tbl)


def _peer_v_tables():
    col = np.arange(PEER_SLOTS * SUBLANES)
    q, r = col // 16, col % 16
    slot = SUBLANES * (q // 4) + (q % 4) + np.where(r < 8, 0, 4)
    expand = (slot[None, :] == np.arange(PEER_SLOTS)[:, None]).astype(np.float32)
    out_row = (r // 2) % 4 + 4 * (r % 2)
    rowsel = (out_row[None, :] == np.arange(SUBLANES)[:, None]).astype(np.float32)
    return jnp.asarray(expand, BF16), jnp.asarray(rowsel, F32)


def _peer_v_kernel(off_hbm, coef_ref, x1_ref, g2_ref, gain_ref, expand_ref, rowsel_ref, tbl_ref, o_ref,
                   out_scr, off_a, off_b, sem):
    sub = lax.broadcasted_iota(I32, (SUBLANES, LANES), 0)
    low4 = sub < 4
    cexp = jnp.dot(coef_ref[...].astype(BF16), expand_ref[...], preferred_element_type=F32)
    rowsel = rowsel_ref[...]

    def half_body(half, off_s):
        for jj in range(PEER_TOKENS_PER_HALF):
            j = half * PEER_TOKENS_PER_HALF + jj
            tiles = []
            for q in range(PEER_SLOTS // 2):
                w = _expert_pair(tbl_ref, off_s, jj, q // 4, q % 4, low4)
                tiles.append(pltpu.bitcast(w, BF16))
            wmat = jnp.concatenate(tiles, axis=0)
            c = (cexp[j:j + 1, :] * rowsel).astype(BF16)
            out = jnp.dot(c, wmat, preferred_element_type=F32)
            for s in range(SUBLANES):
                out_scr[j:j + 1, s * LANES:(s + 1) * LANES] = out[s:s + 1, :]

    _for_token_halves(off_hbm, (off_a, off_b), sem, half_body)
    y = _rms_rows(out_scr[...], gain_ref[...])
    o_ref[...] = x1_ref[...] + g2_ref[0] * y


def _peer_v(off, coef, x1, g2, gain, tbl, tokens_per_sample):
    t = x1.shape[0]
    tb = PEER_TOKENS_PER_STEP
    steps_per_sample = tokens_per_sample // tb
    expand, rowsel = _peer_v_tables()
    tok = lambda: pl.BlockSpec((tb, D_MODEL), lambda i: (i, 0))
    return pl.pallas_call(
        _peer_v_kernel,
        out_shape=jax.ShapeDtypeStruct((t, D_MODEL), F32),
        grid=(t // tb,),
        in_specs=[pl.BlockSpec(memory_space=pl.ANY),
                  pl.BlockSpec((tb, PEER_SLOTS), lambda i: (i, 0)),
                  tok(),
                  pl.BlockSpec((1, 1, D_MODEL), lambda i: (i // steps_per_sample, 0, 0)),
                  pl.BlockSpec((1, D_MODEL), lambda i: (0, 0)),
                  pl.BlockSpec(expand.shape, lambda i: (0, 0)),
                  pl.BlockSpec(rowsel.shape, lambda i: (0, 0)),
                  pl.BlockSpec(memory_space=pltpu.VMEM)],
        out_specs=tok(),
        scratch_shapes=[pltpu.VMEM((tb, D_MODEL), F32)] + _offset_scratch(),
        compiler_params=_cparams(("arbitrary",)),
        name="peer_v",
    )(off, coef, x1, g2, gain, expand, rowsel, tbl)


@functools.lru_cache(maxsize=None)
def _rope_tables_np(n):
    rows = n // GRID_W
    row_ids = np.repeat(np.arange(rows, dtype=np.float64), GRID_W)
    col_ids = np.tile(np.arange(GRID_W, dtype=np.float64), rows)
    freqs = ROPE_BASE ** (-np.arange(ROPE_PAIRS, dtype=np.float64) / ROPE_PAIRS)
    ang = np.concatenate([row_ids[:, None] * freqs[None, :], col_ids[:, None] * freqs[None, :]], axis=-1)
    cos, sin = np.cos(ang), np.sin(ang)
    cos_h = np.concatenate([cos, cos], axis=-1)
    sin_h = np.concatenate([-sin, sin], axis=-1)
    return (np.tile(cos_h, (1, RET_HEADS)).astype(np.float32),
            np.tile(sin_h, (1, RET_HEADS)).astype(np.float32))


@functools.lru_cache(maxsize=None)
def _dft_tables_np(n):
    scale = (n * FNET_GROUP_DIM) ** -0.5
    jn = np.arange(n, dtype=np.int64)
    ang_n = ((jn[:, None] * jn[None, :]) % n).astype(np.float64) * (2.0 * np.pi / n)
    cs = np.concatenate([np.cos(ang_n), -np.sin(ang_n)], axis=1).astype(np.float32)
    jc = np.arange(FNET_GROUP_DIM, dtype=np.int64)
    ang_c = ((jc[:, None] * jc[None, :]) % FNET_GROUP_DIM).astype(np.float64) * (2.0 * np.pi / FNET_GROUP_DIM)
    eye = np.eye(FNET_GROUPS)
    bc = np.kron(eye, np.cos(ang_c) * scale).astype(np.float32)
    bs = np.kron(eye, np.sin(ang_c) * scale).astype(np.float32)
    return bc, bs, cs


def kernel(x, c, ctx, c_ctx, w_ada, b_ada, norm_pre_mix, norm_post_mix, w_in, ret_decay_fwd, ret_decay_bwd, ret_norm_gain, w_out, norm_pre_ffn, norm_post_ffn, peer_w_query, peer_sub_keys_1, peer_sub_keys_2, peer_u, peer_v):
    b, n, d = x.shape
    depth = w_ada.shape[0]
    assert depth == 1 and d == D_MODEL and n % RET_CHUNK == 0 and n % GRID_W == 0
    t = b * n
    tm = min(ROW_TILE, n)
    row = lambda a: a.reshape(1, -1)

    cos, sin = [jnp.asarray(a) for a in _rope_tables_np(n)]
    bc, bs, cs = [jnp.asarray(a).astype(BF16) for a in _dft_tables_np(n)]

    pad = (-(b + 1)) % SUBLANES
    c_all = jnp.concatenate([c, c_ctx[None, :], jnp.zeros((pad, d), F32)], axis=0)
    mod = _ada(c_all, w_ada[0], row(b_ada[0]))
    sh1, sc1, g1, sh2, sc2, g2 = [mod[:b, i * d:(i + 1) * d].reshape(b, 1, d) for i in range(N_MOD)]
    csh1 = mod[b:b + 1, 0:d]
    csc1 = mod[b:b + 1, d:2 * d]

    a_f, a_b = ret_decay_fwd[0], ret_decay_bwd[0]
    a_f_lane = row(jnp.repeat(a_f, RET_DK))
    a_b_lane = row(jnp.repeat(a_b, RET_DK))

    w_in_b = w_in[0].astype(BF16)
    stf, stb = _ctx_states(ctx, row(norm_pre_mix[0]), csh1, csc1,
                           w_in_b[:, QK_WIDTH:QK_WIDTH + 2 * QK_WIDTH], a_f_lane, a_b_lane)

    q, k, v, gate, f = _inproj(x, sh1, sc1, row(norm_pre_mix[0]), w_in_b, cos, sin, tm)
    ret = _retention(q, k, v, gate, stf, stb, a_f, a_b, a_f_lane, a_b_lane, row(ret_norm_gain[0]))
    four = _fourier(f, bc, bs, cs, tm)

    x1, h2, qp = _outproj(ret, four, x, g1, sh2, sc2, row(norm_post_mix[0]), row(norm_pre_ffn[0]),
                          w_out[0].astype(BF16), peer_w_query[0].astype(BF16), tm)

    off, wts = _topk(qp.reshape(t, -1), peer_sub_keys_1[0].astype(BF16), peer_sub_keys_2[0].astype(BF16))

    coef = _peer_u(off, h2.reshape(t, d), wts, _pack_table(peer_u[0]))
    out = _peer_v(off, coef, x1.reshape(t, d), g2, row(norm_post_ffn[0]), _pack_table(peer_v[0]), n)
    return out.reshape(b, n, d)
```

```python
import functools

import numpy as np
import jax
import jax.numpy as jnp
from jax import lax
from jax.experimental import pallas as pl
from jax.experimental.pallas import tpu as pltpu

F32 = jnp.float32
BF16 = jnp.bfloat16
I32 = jnp.int32
U32 = jnp.uint32

D_MODEL = 1024
GRID_W = 64
RET_HEADS = 8
RET_DK = 64
RET_WIDTH = 512
QK_WIDTH = 512
RET_CHUNK = 128
ROPE_PAIRS = RET_DK // 4
ROPE_BASE = 10000.0
FNET_GROUPS = 4
FNET_GROUP_DIM = 128
FNET_WIDTH = 512
IN_WIDTH = 2560
PEER_HEADS = 8
PEER_D_KEY = 256
PEER_N_KEYS = 128
PEER_TOPK = 16
N_MOD = 6
EPS = 1e-6

LANES = 128
SUBLANES = 8
VMEM_LIMIT = 56 * 1024 * 1024
N_PAIRS = RET_HEADS // 2
ROW_TILE = 1024

PEER_SLOTS = PEER_HEADS * PEER_TOPK
PEER_TOKENS_PER_STEP = 256
PEER_TOKENS_PER_HALF = PEER_TOKENS_PER_STEP // 2
TABLE_ROWS_PER_EXPERT = 8
PACK_EXPERTS_PER_STEP = 512
TABLE_PAD_ROWS = 16
HI_MASK = np.uint32(0xFFFF0000)
ID_SENTINEL = float(2 ** 20)


def _cparams(sem):
    return pltpu.CompilerParams(dimension_semantics=sem, vmem_limit_bytes=VMEM_LIMIT)


def _rms_rows(x, gain):
    ms = jnp.mean(x * x, axis=-1, keepdims=True)
    return x * lax.rsqrt(ms + EPS) * gain


def _silu(x):
    return x * jax.nn.sigmoid(x)


def _log_gamma(a):
    return jnp.log1p(-jnp.exp(a))


def _ada_kernel(c_ref, w_ref, b_ref, o_ref):
    s = _silu(c_ref[...])
    o_ref[...] = jnp.dot(s, w_ref[...], preferred_element_type=F32) + b_ref[...]


def _ada(c_all, w_ada, b_ada):
    rows = c_all.shape[0]
    width = w_ada.shape[1]
    tn = 1536
    return pl.pallas_call(
        _ada_kernel,
        out_shape=jax.ShapeDtypeStruct((rows, width), F32),
        grid=(width // tn,),
        in_specs=[
            pl.BlockSpec((rows, D_MODEL), lambda j: (0, 0)),
            pl.BlockSpec((D_MODEL, tn), lambda j: (0, j)),
            pl.BlockSpec((1, tn), lambda j: (0, j)),
        ],
        out_specs=pl.BlockSpec((rows, tn), lambda j: (0, j)),
        compiler_params=_cparams(("arbitrary",)),
        name="ada",
    )(c_all, w_ada, b_ada)


def _ctx_kernel(ctx_ref, g_ref, sh_ref, sc_ref, w_ref, af_ref, ab_ref, stf_ref, stb_ref):
    x = ctx_ref[0]
    n = x.shape[0]
    h = (_rms_rows(x, g_ref[...]) * (1.0 + sc_ref[...]) + sh_ref[...]).astype(BF16)
    kv = jnp.dot(h, w_ref[...], preferred_element_type=F32)
    k = kv[:, :QK_WIDTH] * (RET_DK ** -0.5)
    v = kv[:, QK_WIDTH:].astype(BF16)
    pos = lax.broadcasted_iota(I32, (n, QK_WIDTH), 0).astype(F32)
    lgf = _log_gamma(af_ref[...])
    lgb = _log_gamma(ab_ref[...])
    kf = (k * jnp.exp(lgf * (n - 1.0 - pos))).astype(BF16)
    kb = (k * jnp.exp(lgb * pos)).astype(BF16)
    tn = (((0,), (0,)), ((), ()))
    gf = lax.dot_general(kf, v, tn, preferred_element_type=F32)
    gb = lax.dot_general(kb, v, tn, preferred_element_type=F32)
    ri = lax.broadcasted_iota(I32, (LANES, LANES), 0) // RET_DK
    ci = lax.broadcasted_iota(I32, (LANES, LANES), 1) // RET_DK
    bd = ri == ci
    for p in range(N_PAIRS):
        sl = slice(p * LANES, (p + 1) * LANES)
        stf_ref[0, p] = jnp.where(bd, gf[sl, sl], 0.0)
        stb_ref[0, p] = jnp.where(bd, gb[sl, sl], 0.0)


def _ctx_states(ctx, gain, csh, csc, w_kv, a_f, a_b):
    b, n, _ = ctx.shape
    vec = lambda: pl.BlockSpec((1, D_MODEL), lambda i: (0, 0))
    lane = lambda: pl.BlockSpec((1, QK_WIDTH), lambda i: (0, 0))
    st = jax.ShapeDtypeStruct((b, N_PAIRS, LANES, LANES), F32)
    st_spec = lambda: pl.BlockSpec((1, N_PAIRS, LANES, LANES), lambda i: (i, 0, 0, 0))
    return pl.pallas_call(
        _ctx_kernel,
        out_shape=(st, st),
        grid=(b,),
        in_specs=[
            pl.BlockSpec((1, n, D_MODEL), lambda i: (i, 0, 0)),
            vec(), vec(), vec(),
            pl.BlockSpec((D_MODEL, 2 * QK_WIDTH), lambda i: (0, 0)),
            lane(), lane(),
        ],
        out_specs=(st_spec(), st_spec()),
        compiler_params=_cparams(("arbitrary",)),
        name="ctx_states",
    )(ctx, gain, csh, csc, w_kv, a_f, a_b)


def _inproj_kernel(x_ref, sh_ref, sc_ref, g_ref, w_ref, cos_ref, sin_ref,
                   q_ref, k_ref, v_ref, gate_ref, f_ref):
    x = x_ref[0]
    tm = x.shape[0]
    h = (_rms_rows(x, g_ref[...]) * (1.0 + sc_ref[0]) + sh_ref[0]).astype(BF16)

    def proj(lo, hi):
        return jnp.dot(h, w_ref[:, lo:hi], preferred_element_type=F32)

    cos = cos_ref[...]
    sin = sin_ref[...]
    lane = lax.broadcasted_iota(I32, (tm, QK_WIDTH), 1)
    first = (lane % RET_DK) < (RET_DK // 2)

    def rope(t):
        swapped = jnp.where(first,
                            pltpu.roll(t, QK_WIDTH - RET_DK // 2, 1),
                            pltpu.roll(t, RET_DK // 2, 1))
        return t * cos + swapped * sin

    q_ref[0] = rope(proj(0, 512)).astype(BF16)
    k_ref[0] = rope(proj(512, 1024) * (RET_DK ** -0.5)).astype(BF16)
    v_ref[0] = proj(1024, 1536).astype(BF16)
    gate_ref[0] = proj(1536, 2048)
    f_ref[0] = proj(2048, 2560).astype(BF16)


def _inproj(x, sh, sc, gain, w_in, cos, sin, tm):
    b, n, _ = x.shape
    mod = lambda: pl.BlockSpec((1, 1, D_MODEL), lambda i, j: (i, 0, 0))
    tab = lambda: pl.BlockSpec((tm, QK_WIDTH), lambda i, j: (j, 0))
    out = lambda: pl.BlockSpec((1, tm, 512), lambda i, j: (i, j, 0))
    sd = lambda dt: jax.ShapeDtypeStruct((b, n, 512), dt)
    return pl.pallas_call(
        _inproj_kernel,
        out_shape=(sd(BF16), sd(BF16), sd(BF16), sd(F32), sd(BF16)),
        grid=(b, n // tm),
        in_specs=[
            pl.BlockSpec((1, tm, D_MODEL), lambda i, j: (i, j, 0)),
            mod(), mod(),
            pl.BlockSpec((1, D_MODEL), lambda i, j: (0, 0)),
            pl.BlockSpec((D_MODEL, IN_WIDTH), lambda i, j: (0, 0)),
            tab(), tab(),
        ],
        out_specs=(out(), out(), out(), out(), out()),
        compiler_params=_cparams(("arbitrary", "arbitrary")),
        name="inproj",
    )(x, sh, sc, gain, w_in, cos, sin)


def _ret_kernel(af_s, ab_s, q_ref, k_ref, v_ref, gate_ref, stf_ref, stb_ref, af_ref, ab_ref,
                gain_ref, o_ref, mask_scr, qdf_scr, qdb_scr, kdf_scr, kdb_scr, sf_scr, sb_scr,
                sb_store):
    c = RET_CHUNK
    n = q_ref.shape[1]
    nc = n // c
    lgf = _log_gamma(af_ref[...])
    lgb = _log_gamma(ab_ref[...])
    cdf = jnp.exp(lgf * float(c))
    cdb = jnp.exp(lgb * float(c))

    @pl.when(pl.program_id(0) == 0)
    def _():
        r = lax.broadcasted_iota(I32, (c, RET_WIDTH), 0).astype(F32)
        qdf_scr[...] = jnp.exp(lgf * (r + 1.0))
        kdf_scr[...] = jnp.exp(lgf * (c - 1.0 - r))
        qdb_scr[...] = jnp.exp(lgb * (c - r))
        kdb_scr[...] = jnp.exp(lgb * r)
        ri = lax.broadcasted_iota(I32, (c, c), 0)
        ci = lax.broadcasted_iota(I32, (c, c), 1)
        d = (ri - ci).astype(F32)
        for hh in range(RET_HEADS):
            gf = _log_gamma(jnp.full((c, c), af_s[hh], F32))
            gb = _log_gamma(jnp.full((c, c), ab_s[hh], F32))
            mask_scr[hh] = jnp.where(d >= 0.0, jnp.exp(gf * jnp.maximum(d, 0.0)),
                                     jnp.exp(gb * jnp.maximum(-d, 0.0)))

    ri = lax.broadcasted_iota(I32, (LANES, LANES), 0) // RET_DK
    ci = lax.broadcasted_iota(I32, (LANES, LANES), 1) // RET_DK
    bd = ri == ci
    bd_ones = jnp.where(bd, 1.0, 0.0).astype(BF16)
    lane_head = lax.broadcasted_iota(I32, (c, LANES), 1) // RET_DK
    tn = (((0,), (0,)), ((), ()))
    nt = (((1,), (1,)), ((), ()))

    for p in range(N_PAIRS):
        sf_scr[p] = stf_ref[0, p]
        sb_scr[p] = stb_ref[0, p]

    def bwd_states(i, carry):
        cc = nc - 1 - i
        rows = pl.ds(pl.multiple_of(cc * c, c), c)
        grams = []
        for p in range(N_PAIRS):
            sl = slice(p * LANES, (p + 1) * LANES)
            kw = (k_ref[0, rows, sl].astype(F32) * kdb_scr[:, sl]).astype(BF16)
            grams.append(lax.dot_general(kw, v_ref[0, rows, sl], tn, preferred_element_type=F32))
        for p in range(N_PAIRS):
            sl = slice(p * LANES, (p + 1) * LANES)
            sb = sb_scr[p]
            sb_store[cc, p] = sb.astype(BF16)
            sb_scr[p] = sb * cdb[:, sl] + jnp.where(bd, grams[p], 0.0)
        return carry

    lax.fori_loop(0, nc, bwd_states, 0)

    def fwd(cc, carry):
        rows = pl.ds(pl.multiple_of(cc * c, c), c)
        pairs = range(N_PAIRS)
        lanes = [slice(p * LANES, (p + 1) * LANES) for p in pairs]
        zero = jnp.zeros((c, LANES), BF16)
        both_bd = jnp.concatenate([bd_ones, bd_ones], axis=0)
        qs = [q_ref[0, rows, sl] for sl in lanes]
        ks = [k_ref[0, rows, sl] for sl in lanes]
        vs = [v_ref[0, rows, sl] for sl in lanes]
        sfs = [sf_scr[p] for p in pairs]
        scores = []
        for p in pairs:
            q_heads = jnp.concatenate([jnp.where(lane_head == j, qs[p], zero) for j in range(2)], axis=0)
            scores.append(lax.dot_general(q_heads, ks[p], nt, preferred_element_type=F32))
        accs = []
        for p in pairs:
            qf = qs[p].astype(F32)
            q_dec = jnp.concatenate([(qf * qdf_scr[:, lanes[p]]).astype(BF16),
                                     (qf * qdb_scr[:, lanes[p]]).astype(BF16)], axis=1)
            s_both = jnp.concatenate([sfs[p].astype(BF16), sb_store[cc, p]], axis=0)
            accs.append(jnp.dot(q_dec, s_both, preferred_element_type=F32))
        grams = []
        for p in pairs:
            kw = (ks[p].astype(F32) * kdf_scr[:, lanes[p]]).astype(BF16)
            grams.append(lax.dot_general(kw, vs[p], tn, preferred_element_type=F32))
        for p in pairs:
            s = scores[p]
            pm = jnp.concatenate([(s[j * c:(j + 1) * c] * mask_scr[2 * p + j]).astype(BF16) for j in range(2)], axis=1)
            v_heads = jnp.concatenate([jnp.where(lane_head == j, vs[p], zero) for j in range(2)], axis=0)
            accs[p] = accs[p] + jnp.dot(pm, v_heads, preferred_element_type=F32)
        means = []
        for p in pairs:
            sq_hi, sq_lo = _split_bf16(accs[p] * accs[p])
            means.append(jnp.dot(jnp.concatenate([sq_hi, sq_lo], axis=1), both_bd,
                                 preferred_element_type=F32) * (1.0 / RET_DK))
        for p in pairs:
            y = accs[p] * lax.rsqrt(means[p] + EPS) * gain_ref[:, lanes[p]] * _silu(gate_ref[0, rows, lanes[p]])
            o_ref[0, rows, lanes[p]] = y.astype(BF16)
            sf_scr[p] = sfs[p] * cdf[:, lanes[p]] + jnp.where(bd, grams[p], 0.0)
        return carry

    lax.fori_loop(0, nc, fwd, 0, unroll=2)


def _retention(q, k, v, gate, stf, stb, a_f, a_b, a_f_lane, a_b_lane, gain):
    b, n, _ = q.shape
    nc = n // RET_CHUNK
    seq = lambda: pl.BlockSpec((1, n, RET_WIDTH), lambda i, *_: (i, 0, 0))
    st = lambda: pl.BlockSpec((1, N_PAIRS, LANES, LANES), lambda i, *_: (i, 0, 0, 0))
    lane = lambda: pl.BlockSpec((1, RET_WIDTH), lambda i, *_: (0, 0))
    grid_spec = pltpu.PrefetchScalarGridSpec(
        num_scalar_prefetch=2,
        grid=(b,),
        in_specs=[seq(), seq(), seq(), seq(), st(), st(), lane(), lane(), lane()],
        out_specs=seq(),
        scratch_shapes=[
            pltpu.VMEM((RET_HEADS, RET_CHUNK, RET_CHUNK), F32),
            pltpu.VMEM((RET_CHUNK, RET_WIDTH), F32),
            pltpu.VMEM((RET_CHUNK, RET_WIDTH), F32),
            pltpu.VMEM((RET_CHUNK, RET_WIDTH), F32),
            pltpu.VMEM((RET_CHUNK, RET_WIDTH), F32),
            pltpu.VMEM((N_PAIRS, LANES, LANES), F32),
            pltpu.VMEM((N_PAIRS, LANES, LANES), F32),
            pltpu.VMEM((nc, N_PAIRS, LANES, LANES), BF16),
        ],
    )
    return pl.pallas_call(
        _ret_kernel,
        out_shape=jax.ShapeDtypeStruct((b, n, RET_WIDTH), BF16),
        grid_spec=grid_spec,
        compiler_params=_cparams(("arbitrary",)),
        name="retention",
    )(a_f, a_b, q, k, v, gate, stf, stb, a_f_lane, a_b_lane, gain)


def _four_kernel(f_ref, bc_ref, bs_ref, cs_ref, o_ref, z_scr):
    n = f_ref.shape[1]

    @pl.when(pl.program_id(1) == 0)
    def _():
        f = f_ref[0]
        z_scr[0:n, :] = jnp.dot(f, bc_ref[...], preferred_element_type=F32).astype(BF16)
        z_scr[n:2 * n, :] = jnp.dot(f, bs_ref[...], preferred_element_type=F32).astype(BF16)

    o_ref[0] = jnp.dot(cs_ref[...], z_scr[...], preferred_element_type=F32).astype(BF16)


def _fourier(f, bc, bs, cs, tn):
    b, n, _ = f.shape
    return pl.pallas_call(
        _four_kernel,
        out_shape=jax.ShapeDtypeStruct((b, n, FNET_WIDTH), BF16),
        grid=(b, n // tn),
        in_specs=[
            pl.BlockSpec((1, n, FNET_WIDTH), lambda i, j: (i, 0, 0)),
            pl.BlockSpec((FNET_WIDTH, FNET_WIDTH), lambda i, j: (0, 0)),
            pl.BlockSpec((FNET_WIDTH, FNET_WIDTH), lambda i, j: (0, 0)),
            pl.BlockSpec((tn, 2 * n), lambda i, j: (j, 0)),
        ],
        out_specs=pl.BlockSpec((1, tn, FNET_WIDTH), lambda i, j: (i, j, 0)),
        scratch_shapes=[pltpu.VMEM((2 * n, FNET_WIDTH), BF16)],
        compiler_params=_cparams(("arbitrary", "arbitrary")),
        name="fourier",
    )(f, bc, bs, cs)


def _outp_kernel(ret_ref, four_ref, x_ref, g1_ref, sh2_ref, sc2_ref, gpost_ref, gpre_ref,
                 wo_ref, wq_ref, x1_ref, h2_ref, qp_ref):
    mix = jnp.dot(ret_ref[0], wo_ref[0:RET_WIDTH, :], preferred_element_type=F32)
    mix += jnp.dot(four_ref[0], wo_ref[RET_WIDTH:, :], preferred_element_type=F32)
    x1 = x_ref[0] + g1_ref[0] * _rms_rows(mix, gpost_ref[...])
    x1_ref[0] = x1
    h2 = _rms_rows(x1, gpre_ref[...]) * (1.0 + sc2_ref[0]) + sh2_ref[0]
    h2_ref[0] = h2
    qp_ref[0] = jnp.dot(h2.astype(BF16), wq_ref[...], preferred_element_type=F32).astype(BF16)


def _outproj(ret, four, x, g1, sh2, sc2, gpost, gpre, w_out, w_q, tm):
    b, n, _ = x.shape
    qw = w_q.shape[1]
    mod = lambda: pl.BlockSpec((1, 1, D_MODEL), lambda i, j: (i, 0, 0))
    vec = lambda: pl.BlockSpec((1, D_MODEL), lambda i, j: (0, 0))
    half = lambda: pl.BlockSpec((1, tm, 512), lambda i, j: (i, j, 0))
    full = lambda: pl.BlockSpec((1, tm, D_MODEL), lambda i, j: (i, j, 0))
    return pl.pallas_call(
        _outp_kernel,
        out_shape=(jax.ShapeDtypeStruct((b, n, D_MODEL), F32),
                   jax.ShapeDtypeStruct((b, n, D_MODEL), F32),
                   jax.ShapeDtypeStruct((b, n, qw), BF16)),
        grid=(b, n // tm),
        in_specs=[half(), half(), full(), mod(), mod(), mod(), vec(), vec(),
                  pl.BlockSpec(w_out.shape, lambda i, j: (0, 0)),
                  pl.BlockSpec(w_q.shape, lambda i, j: (0, 0))],
        out_specs=(full(), full(), pl.BlockSpec((1, tm, qw), lambda i, j: (i, j, 0))),
        compiler_params=_cparams(("arbitrary", "arbitrary")),
        name="outproj",
    )(ret, four, x, g1, sh2, sc2, gpost, gpre, w_out, w_q)


def _sorting_network(n):
    pairs = []
    p = 1
    while p < n:
        k = p
        while k >= 1:
            for j in range(k % p, n - k, 2 * k):
                for i in range(min(k, n - j - k)):
                    if (i + j) // (2 * p) == (i + j + k) // (2 * p):
                        pairs.append((i + j, i + j + k))
            k //= 2
        p *= 2
    return pairs


def _top_rows_by_columns(s, kk):
    rows, tq = s.shape
    nv = rows // SUBLANES
    assert nv == kk
    sub = lax.broadcasted_iota(I32, (SUBLANES, tq), 0).astype(F32)
    v = [s[k * SUBLANES:(k + 1) * SUBLANES] for k in range(nv)]
    ids = [sub + float(k * SUBLANES) for k in range(nv)]
    for a, b in _sorting_network(nv):
        swap = (v[b] > v[a]) | ((v[b] == v[a]) & (ids[b] < ids[a]))
        v[a], v[b] = jnp.where(swap, v[b], v[a]), jnp.where(swap, v[a], v[b])
        ids[a], ids[b] = jnp.where(swap, ids[b], ids[a]), jnp.where(swap, ids[a], ids[b])
    slot = lax.broadcasted_iota(I32, (kk, tq), 0)
    vals = jnp.zeros((kk, tq), F32)
    idxs = jnp.zeros((kk, tq), F32)
    for it in range(kk):
        m = jnp.max(v[0], axis=0, keepdims=True)
        am = jnp.min(jnp.where(v[0] == m, ids[0], float(rows)), axis=0, keepdims=True)
        vals = jnp.where(slot == it, m, vals)
        idxs = jnp.where(slot == it, am, idxs)
        hit = ids[0] == am
        for k in range(kk - 1 - it):
            v[k] = jnp.where(hit, v[k + 1], v[k])
            ids[k] = jnp.where(hit, ids[k + 1], ids[k])
    return vals, idxs


def _pick_rows(table, sel, kk):
    out = jnp.zeros_like(table)
    for i in range(kk):
        out = jnp.where(sel == float(i), table[i:i + 1, :], out)
    return out


def _top_pairs(v1, v2):
    kk = PEER_TOPK
    tq = v1.shape[1]
    big = ID_SENTINEL
    sub = lax.broadcasted_iota(I32, (SUBLANES, tq), 0)
    subf = sub.astype(F32)
    near = v1[0:SUBLANES]
    depth = []
    for k in range(kk):
        reach = kk // (k + 1)
        row = near + v2[k:k + 1]
        depth.append(row if reach >= SUBLANES else jnp.where(sub < reach, row, -jnp.inf))
    far = v1[SUBLANES:kk] + v2[0:1]
    id_near = subf * float(kk)
    id_far = (subf + float(SUBLANES)) * float(kk)
    slot = lax.broadcasted_iota(I32, (kk, tq), 0)
    vals = jnp.zeros((kk, tq), F32)
    idxs = jnp.zeros((kk, tq), F32)
    for it in range(kk):
        m = jnp.max(jnp.maximum(depth[0], far), axis=0, keepdims=True)
        am = jnp.min(jnp.minimum(jnp.where(depth[0] == m, id_near, big), jnp.where(far == m, id_far, big)),
                     axis=0, keepdims=True)
        vals = jnp.where(slot == it, m, vals)
        idxs = jnp.where(slot == it, am, idxs)
        hit_near = id_near == am
        for k in range(kk - 1 - it):
            depth[k] = jnp.where(hit_near, depth[k + 1], depth[k])
        id_near = jnp.where(hit_near, id_near + 1.0, id_near)
        far = jnp.where(id_far == am, -jnp.inf, far)
    return vals, idxs


def _topk_kernel(qp_ref, k1_ref, k2_ref, off_ref, wts_ref):
    kk = PEER_TOPK
    half = PEER_D_KEY // 2
    nt = (((1,), (1,)), ((), ()))
    k1 = k1_ref[...]
    k2 = k2_ref[...]
    ids_all, w_all = [], []
    for hh in range(PEER_HEADS):
        q1 = qp_ref[:, hh * PEER_D_KEY: hh * PEER_D_KEY + half]
        q2 = qp_ref[:, hh * PEER_D_KEY + half: (hh + 1) * PEER_D_KEY]
        s1 = lax.dot_general(k1, q1, nt, preferred_element_type=F32)
        s2 = lax.dot_general(k2, q2, nt, preferred_element_type=F32)
        v1, i1 = _top_rows_by_columns(s1, kk)
        v2, i2 = _top_rows_by_columns(s2, kk)
        tv, ti = _top_pairs(v1, v2)
        hi = jnp.floor(ti * (1.0 / kk))
        lo = ti - hi * kk
        ids = _pick_rows(i1, hi, kk) * float(PEER_N_KEYS) + _pick_rows(i2, lo, kk)
        p = jnp.exp(tv - tv[0:1, :])
        w = p / jnp.sum(p, axis=0, keepdims=True)
        ids_all.append(ids)
        w_all.append(w)
    ids_t = jnp.concatenate(ids_all, axis=0).T.astype(I32)
    slot = lax.broadcasted_iota(I32, ids_t.shape, 1)
    off_ref[...] = (ids_t * TABLE_ROWS_PER_EXPERT
                    + jnp.where((slot % SUBLANES) < 4, TABLE_PAD_ROWS, TABLE_PAD_ROWS - TABLE_ROWS_PER_EXPERT))
    wts_ref[...] = jnp.concatenate(w_all, axis=0).T


def _topk(qp, k1, k2):
    t, qw = qp.shape
    tq = LANES
    out = lambda: pl.BlockSpec((tq, PEER_SLOTS), lambda i: (i, 0))
    key = lambda: pl.BlockSpec((PEER_N_KEYS, PEER_D_KEY // 2), lambda i: (0, 0))
    return pl.pallas_call(
        _topk_kernel,
        out_shape=(jax.ShapeDtypeStruct((t, PEER_SLOTS), I32),
                   jax.ShapeDtypeStruct((t, PEER_SLOTS), F32)),
        grid=(t // tq,),
        in_specs=[pl.BlockSpec((tq, qw), lambda i: (i, 0)), key(), key()],
        out_specs=(out(), out()),
        compiler_params=_cparams(("arbitrary",)),
        name="peer_topk",
    )(qp, k1, k2)


def _pack_table(w):
    e = w.shape[0]
    rows = w.astype(BF16).reshape(e * TABLE_ROWS_PER_EXPERT, LANES)
    return jnp.pad(rows, ((TABLE_PAD_ROWS, TABLE_PAD_ROWS), (0, 0)))


def _expert_pair(tbl_ref, off_s, j, g, i, first8):
    wa = tbl_ref[pl.ds(pl.multiple_of(off_s[j, g * SUBLANES + i], SUBLANES), 2 * SUBLANES), :]
    wb = tbl_ref[pl.ds(pl.multiple_of(off_s[j, g * SUBLANES + i + 4], SUBLANES), 2 * SUBLANES), :]
    return jnp.where(first8, wa, wb)


def _for_token_halves(off_hbm, bufs, sem, half_body):
    step = pl.program_id(0)
    last = pl.num_programs(0) - 1

    def fill(at_step, half):
        start = at_step * PEER_TOKENS_PER_STEP + half * PEER_TOKENS_PER_HALF
        return pltpu.make_async_copy(off_hbm.at[pl.ds(start, PEER_TOKENS_PER_HALF)], bufs[half], sem.at[half])

    @pl.when(step == 0)
    def _():
        for half in range(2):
            fill(0, half).start()

    for half in range(2):
        fill(step, half).wait()
        half_body(half, bufs[half])

        @pl.when(step < last)
        def _():
            fill(step + 1, half).start()


def _offset_scratch():
    return [pltpu.SMEM((PEER_TOKENS_PER_HALF, PEER_SLOTS), I32),
            pltpu.SMEM((PEER_TOKENS_PER_HALF, PEER_SLOTS), I32),
            pltpu.SemaphoreType.DMA((2,))]


def _gelu_exact(x):
    return 0.5 * x * (1.0 + lax.erf(x * (2.0 ** -0.5)))


def _peer_u_slot_sum():
    col = np.arange(PEER_SLOTS * TABLE_ROWS_PER_EXPERT)
    q, r = col // 16, col % 16
    slot = SUBLANES * (q // 4) + (q % 4) + np.where(r < 8, 0, 4)
    return jnp.asarray((slot[:, None] == np.arange(PEER_SLOTS)[None, :]).astype(np.float32), BF16)


def _split_bf16(x):
    hi = x.astype(BF16)
    return hi, (x - hi.astype(F32)).astype(BF16)


def _peer_u_kernel(off_hbm, h_ref, wts_ref, slotsum_ref, tbl_ref, coef_ref, off_a, off_b, sem):
    nt = (((1,), (1,)), ((), ()))
    ones = jnp.ones((SUBLANES, LANES), BF16)
    first8 = lax.broadcasted_iota(I32, (2 * SUBLANES, LANES), 0) < SUBLANES

    def token_row_sums(off_s, jj, j):
        h8 = jnp.concatenate([h_ref[j:j + 1, c * LANES:(c + 1) * LANES] for c in range(SUBLANES)], axis=0)
        h16 = jnp.concatenate([h8, h8], axis=0)
        prods = []
        for q in range(PEER_SLOTS // 2):
            w = _expert_pair(tbl_ref, off_s, jj, q // 4, q % 4, first8)
            prods.append(w.astype(F32) * h16)
        stacked = jnp.concatenate(prods, axis=0).astype(BF16)
        return lax.dot_general(ones, stacked, nt, preferred_element_type=F32)[0:1, :]

    def finish(half, sums):
        hi, lo = _split_bf16(sums)
        acts = (jnp.dot(hi, slotsum_ref[...], preferred_element_type=F32)
                + jnp.dot(lo, slotsum_ref[...], preferred_element_type=F32))
        rows = pl.ds(half * PEER_TOKENS_PER_HALF, PEER_TOKENS_PER_HALF)
        coef_ref[rows, :] = wts_ref[rows, :] * _gelu_exact(acts)

    pending = []

    def half_body(half, off_s):
        while pending:
            finish(*pending.pop())
        base = half * PEER_TOKENS_PER_HALF
        sums = [token_row_sums(off_s, jj, base + jj) for jj in range(PEER_TOKENS_PER_HALF)]
        pending.append((half, jnp.concatenate(sums, axis=0)))

    _for_token_halves(off_hbm, (off_a, off_b), sem, half_body)
    finish(*pending.pop())


def _peer_u(off, h, wts, tbl):
    t = h.shape[0]
    tb = PEER_TOKENS_PER_STEP
    slotsum = _peer_u_slot_sum()
    return pl.pallas_call(
        _peer_u_kernel,
        out_shape=jax.ShapeDtypeStruct((t, PEER_SLOTS), F32),
        grid=(t // tb,),
        in_specs=[pl.BlockSpec(memory_space=pl.ANY),
                  pl.BlockSpec((tb, D_MODEL), lambda i: (i, 0)),
                  pl.BlockSpec((tb, PEER_SLOTS), lambda i: (i, 0)),
                  pl.BlockSpec(slotsum.shape, lambda i: (0, 0)),
                  pl.BlockSpec(memory_space=pltpu.VMEM)],
        out_specs=pl.BlockSpec((tb, PEER_SLOTS), lambda i: (i, 0)),
        scratch_shapes=_offset_scratch(),
        compiler_params=_cparams(("arbitrary",)),
        name="peer_u",
    )(off, h, wts, slotsum, tbl)


def _peer_v_tables():
    col = np.arange(PEER_SLOTS * SUBLANES)
    q, r = col // 16, col % 16
    slot = SUBLANES * (q // 4) + (q % 4) + np.where(r < 8, 0, 4)
    expand = (slot[None, :] == np.arange(PEER_SLOTS)[:, None]).astype(np.float32)
    out_row = r % SUBLANES
    rowsel = (out_row[None, :] == np.arange(SUBLANES)[:, None]).astype(np.float32)
    return jnp.asarray(expand, BF16), jnp.asarray(rowsel, F32)


def _peer_v_kernel(off_hbm, coef_ref, x1_ref, g2_ref, gain_ref, expand_ref, rowsel_ref, tbl_ref, o_ref,
                   out_scr, off_a, off_b, sem):
    first8 = lax.broadcasted_iota(I32, (2 * SUBLANES, LANES), 0) < SUBLANES
    cexp = jnp.dot(coef_ref[...].astype(BF16), expand_ref[...], preferred_element_type=F32)
    rowsel = rowsel_ref[...]

    def half_body(half, off_s):
        for jj in range(PEER_TOKENS_PER_HALF):
            j = half * PEER_TOKENS_PER_HALF + jj
            tiles = []
            for q in range(PEER_SLOTS // 2):
                tiles.append(_expert_pair(tbl_ref, off_s, jj, q // 4, q % 4, first8))
            wmat = jnp.concatenate(tiles, axis=0)
            c = (cexp[j:j + 1, :] * rowsel).astype(BF16)
            out = jnp.dot(c, wmat, preferred_element_type=F32)
            for s in range(SUBLANES):
                out_scr[j:j + 1, s * LANES:(s + 1) * LANES] = out[s:s + 1, :]

    _for_token_halves(off_hbm, (off_a, off_b), sem, half_body)
    y = _rms_rows(out_scr[...], gain_ref[...])
    o_ref[...] = x1_ref[...] + g2_ref[0] * y


def _peer_v(off, coef, x1, g2, gain, tbl, tokens_per_sample):
    t = x1.shape[0]
    tb = PEER_TOKENS_PER_STEP
    steps_per_sample = tokens_per_sample // tb
    expand, rowsel = _peer_v_tables()
    tok = lambda: pl.BlockSpec((tb, D_MODEL), lambda i: (i, 0))
    return pl.pallas_call(
        _peer_v_kernel,
        out_shape=jax.ShapeDtypeStruct((t, D_MODEL), F32),
        grid=(t // tb,),
        in_specs=[pl.BlockSpec(memory_space=pl.ANY),
                  pl.BlockSpec((tb, PEER_SLOTS), lambda i: (i, 0)),
                  tok(),
                  pl.BlockSpec((1, 1, D_MODEL), lambda i: (i // steps_per_sample, 0, 0)),
                  pl.BlockSpec((1, D_MODEL), lambda i: (0, 0)),
                  pl.BlockSpec(expand.shape, lambda i: (0, 0)),
                  pl.BlockSpec(rowsel.shape, lambda i: (0, 0)),
                  pl.BlockSpec(memory_space=pltpu.VMEM)],
        out_specs=tok(),
        scratch_shapes=[pltpu.VMEM((tb, D_MODEL), F32)] + _offset_scratch(),
        compiler_params=_cparams(("arbitrary",)),
        name="peer_v",
    )(off, coef, x1, g2, gain, expand, rowsel, tbl)


@functools.lru_cache(maxsize=None)
def _rope_tables_np(n):
    rows = n // GRID_W
    row_ids = np.repeat(np.arange(rows, dtype=np.float64), GRID_W)
    col_ids = np.tile(np.arange(GRID_W, dtype=np.float64), rows)
    freqs = ROPE_BASE ** (-np.arange(ROPE_PAIRS, dtype=np.float64) / ROPE_PAIRS)
    ang = np.concatenate([row_ids[:, None] * freqs[None, :], col_ids[:, None] * freqs[None, :]], axis=-1)
    cos, sin = np.cos(ang), np.sin(ang)
    cos_h = np.concatenate([cos, cos], axis=-1)
    sin_h = np.concatenate([-sin, sin], axis=-1)
    return (np.tile(cos_h, (1, RET_HEADS)).astype(np.float32),
            np.tile(sin_h, (1, RET_HEADS)).astype(np.float32))


@functools.lru_cache(maxsize=None)
def _dft_tables_np(n):
    scale = (n * FNET_GROUP_DIM) ** -0.5
    jn = np.arange(n, dtype=np.int64)
    ang_n = ((jn[:, None] * jn[None, :]) % n).astype(np.float64) * (2.0 * np.pi / n)
    cs = np.concatenate([np.cos(ang_n), -np.sin(ang_n)], axis=1).astype(np.float32)
    jc = np.arange(FNET_GROUP_DIM, dtype=np.int64)
    ang_c = ((jc[:, None] * jc[None, :]) % FNET_GROUP_DIM).astype(np.float64) * (2.0 * np.pi / FNET_GROUP_DIM)
    eye = np.eye(FNET_GROUPS)
    bc = np.kron(eye, np.cos(ang_c) * scale).astype(np.float32)
    bs = np.kron(eye, np.sin(ang_c) * scale).astype(np.float32)
    return bc, bs, cs


def kernel(x, c, ctx, c_ctx, w_ada, b_ada, norm_pre_mix, norm_post_mix, w_in, ret_decay_fwd, ret_decay_bwd, ret_norm_gain, w_out, norm_pre_ffn, norm_post_ffn, peer_w_query, peer_sub_keys_1, peer_sub_keys_2, peer_u, peer_v):
    b, n, d = x.shape
    depth = w_ada.shape[0]
    assert depth == 1 and d == D_MODEL and n % RET_CHUNK == 0 and n % GRID_W == 0
    t = b * n
    tm = min(ROW_TILE, n)
    assert n % tm == 0 and n % PEER_TOKENS_PER_STEP == 0 and peer_u.shape[1] % PACK_EXPERTS_PER_STEP == 0
    row = lambda a: a.reshape(1, -1)

    cos, sin = [jnp.asarray(a) for a in _rope_tables_np(n)]
    bc, bs, cs = [jnp.asarray(a).astype(BF16) for a in _dft_tables_np(n)]

    pad = (-(b + 1)) % SUBLANES
    c_all = jnp.concatenate([c, c_ctx[None, :], jnp.zeros((pad, d), F32)], axis=0)
    mod = _ada(c_all, w_ada[0], row(b_ada[0]))
    sh1, sc1, g1, sh2, sc2, g2 = [mod[:b, i * d:(i + 1) * d].reshape(b, 1, d) for i in range(N_MOD)]
    csh1 = mod[b:b + 1, 0:d]
    csc1 = mod[b:b + 1, d:2 * d]

    a_f, a_b = ret_decay_fwd[0], ret_decay_bwd[0]
    a_f_lane = row(jnp.repeat(a_f, RET_DK))
    a_b_lane = row(jnp.repeat(a_b, RET_DK))

    w_in_b = w_in[0].astype(BF16)
    stf, stb = _ctx_states(ctx, row(norm_pre_mix[0]), csh1, csc1,
                           w_in_b[:, QK_WIDTH:QK_WIDTH + 2 * QK_WIDTH], a_f_lane, a_b_lane)

    q, k, v, gate, f = _inproj(x, sh1, sc1, row(norm_pre_mix[0]), w_in_b, cos, sin, tm)
    ret = _retention(q, k, v, gate, stf, stb, a_f, a_b, a_f_lane, a_b_lane, row(ret_norm_gain[0]))
    four = _fourier(f, bc, bs, cs, tm)

    x1, h2, qp = _outproj(ret, four, x, g1, sh2, sc2, row(norm_post_mix[0]), row(norm_pre_ffn[0]),
                          w_out[0].astype(BF16), peer_w_query[0].astype(BF16), tm)

    off, wts = _topk(qp.reshape(t, -1), peer_sub_keys_1[0].astype(BF16), peer_sub_keys_2[0].astype(BF16))

    coef = _peer_u(off, h2.reshape(t, d), wts, _pack_table(peer_u[0]))
    out = _peer_v(off, coef, x1.reshape(t, d), g2, row(norm_post_ffn[0]), _pack_table(peer_v[0]), n)
    return out.reshape(b, n, d)
```
